```python
import math
import jax, jax.numpy as jnp
from jax import lax
import numpy as np


D_MODEL = 2048
BATCH = 4
SEQ = 8192
DEPTH = 1

CHUNK = 64
PLE_DIM = 256
EPS = 1e-6
GLA_HEADS = 8
GLA_DK = 64
GLA_DV = 128
GLA_GATE_RANK = 16
GLA_GATE_TEMP = 16.0
GLA_QK = GLA_HEADS * GLA_DK
GLA_V = GLA_HEADS * GLA_DV
DSA_HEADS = 8
DSA_HD = 128
DSA_W = DSA_HEADS * DSA_HD
IDX_HEADS = 16
IDX_DIM = 64
IDX_Q = IDX_HEADS * IDX_DIM
INDEX_TOPK = 256
Q_BLOCK = 64
REL_BUCKETS = 32
REL_MAX_DIST = 128
N_GROUPS = 8
EXPERTS_PER_GROUP = 8
N_EXPERTS = N_GROUPS * EXPERTS_PER_GROUP
EXPERT_FF = 512
TOP_K_INNER = 2
MOE_BLOCK = 128
SPLITS = (GLA_QK, GLA_QK, GLA_V, GLA_V, GLA_GATE_RANK, DSA_W, DSA_W, DSA_W, IDX_Q, IDX_DIM, IDX_HEADS)
IN_COLS = sum(SPLITS)

kernel_name = 'hybrid_gla_dsa_hmoe_streaming_block'


def rms_norm(t, g):
    tf = t.astype(jnp.float32)
    tf = tf * lax.rsqrt(jnp.mean(tf * tf, axis=-1, keepdims=True) + EPS)
    return (tf * g.astype(jnp.float32)).astype(t.dtype)


def split_cols(z):
    offs, acc = [], 0
    for w in SPLITS[:-1]:
        acc += w
        offs.append(acc)
    return jnp.split(z, offs, axis=-1)


def t5_bucket(rel):
    half = REL_BUCKETS // 2
    exact = half // 2
    sign = jnp.where(rel > 0, half, 0)
    n = jnp.abs(rel)
    nf = jnp.maximum(n, 1).astype(jnp.float32)
    large = exact + (jnp.log(nf / exact) / math.log(REL_MAX_DIST / exact) * (half - exact)).astype(jnp.int32)
    large = jnp.minimum(large, half - 1)
    return sign + jnp.where(n < exact, n, large)


def gla_mixer(q, k, v, g):
    B, S = q.shape[:2]
    n = S // CHUNK

    def to_chunks(t):
        return t.reshape(B, n, CHUNK, *t.shape[2:]).swapaxes(0, 1)

    G = jnp.cumsum(g.reshape(B, n, CHUNK, GLA_HEADS, GLA_DK), axis=2).swapaxes(0, 1)
    causal = jnp.tril(jnp.ones((CHUNK, CHUNK), dtype=bool))

    def step(state, inp):
        qb, kb, vb, Gb = inp
        o_inter = jnp.einsum('bihk,bhkv->bihv', qb * jnp.exp(Gb), state)
        diff = jnp.where(causal[None, :, :, None, None], Gb[:, :, None] - Gb[:, None, :], -jnp.inf)
        scores = jnp.einsum('bihk,bjhk,bijhk->bhij', qb, kb, jnp.exp(diff))
        o_intra = jnp.einsum('bhij,bjhv->bihv', scores, vb)
        G_last = Gb[:, -1]
        k_dec = kb * jnp.exp(G_last[:, None] - Gb)
        state = state * jnp.exp(G_last)[..., None] + jnp.einsum('bjhk,bjhv->bhkv', k_dec, vb)
        return state, o_inter + o_intra

    state0 = jnp.zeros((B, GLA_HEADS, GLA_DK, GLA_DV), q.dtype)
    _, o = lax.scan(step, state0, (to_chunks(q), to_chunks(k), to_chunks(v), G))
    return o.swapaxes(0, 1).reshape(B, S, GLA_HEADS, GLA_DV)


def dsa_mixer(q, k, v, qi, ki, wi, rel_bias):
    B, S = q.shape[:2]
    topk = min(INDEX_TOPK, S // 4)
    nblk = S // Q_BLOCK
    key_chunk = jnp.arange(S) // CHUNK

    def blocks(t):
        return t.reshape(B, nblk, Q_BLOCK, *t.shape[2:]).swapaxes(0, 1)

    def one_block(args):
        blk, qb, qib, wib = args
        qpos = blk * Q_BLOCK + jnp.arange(Q_BLOCK)
        dots = jnp.einsum('bqjd,bsd->bqjs', qib, ki)
        score = jnp.einsum('bqj,bqjs->bqs', wib, jax.nn.relu(dots)).astype(jnp.float32)
        adm = key_chunk[None, :] <= (qpos // CHUNK)[:, None]
        score = jnp.where(adm[None], score, -jnp.inf)
        top_val, top_idx = lax.top_k(score, topk)
        valid = top_val > -jnp.inf
        k_sel = jax.vmap(lambda kk, ii: kk[ii])(k, top_idx)
        v_sel = jax.vmap(lambda vv, ii: vv[ii])(v, top_idx)
        logits = jnp.einsum('bqhd,bqkhd->bhqk', qb, k_sel).astype(jnp.float32) * (DSA_HD ** -0.5)
        bias = rel_bias[t5_bucket(top_idx - qpos[None, :, None])]
        logits = logits + jnp.moveaxis(bias, -1, 1).astype(jnp.float32)
        logits = jnp.where(valid[:, None], logits, -jnp.inf)
        probs = jax.nn.softmax(logits, axis=-1).astype(v.dtype)
        return jnp.einsum('bhqk,bqkhd->bqhd', probs, v_sel)

    out = lax.map(one_block, (jnp.arange(nblk), blocks(q), blocks(qi), blocks(wi)))
    return out.swapaxes(0, 1).reshape(B, S, DSA_W)


def hier_moe(h, w_group_router, b_group_router, w_expert_router, b_expert_router, w_gate, w_up, w_down):
    B, S, D = h.shape
    T = B * S
    hf = h.reshape(T, D)
    grp_logits = (hf @ w_group_router + b_group_router).astype(jnp.float32)
    grp_prob = jax.nn.softmax(grp_logits, axis=-1)
    grp = jnp.argmax(grp_logits, axis=-1)
    grp_w = jnp.take_along_axis(grp_prob, grp[:, None], axis=1)
    exp_logits = (hf @ w_expert_router + b_expert_router).astype(jnp.float32)
    exp_logits = exp_logits.reshape(T, N_GROUPS, EXPERTS_PER_GROUP)
    in_grp = jnp.take_along_axis(exp_logits, grp[:, None, None], axis=1)[:, 0]
    top_val, top_loc = lax.top_k(in_grp, TOP_K_INNER)
    wts = (grp_w * jax.nn.softmax(top_val, axis=-1)).astype(h.dtype)
    eid = grp[:, None] * EXPERTS_PER_GROUP + top_loc

    A = T * TOP_K_INNER
    flat_e = eid.reshape(A)
    flat_tok = jnp.repeat(jnp.arange(T), TOP_K_INNER)
    flat_w = wts.reshape(A)
    order = jnp.argsort(flat_e)
    se, stok, sw = flat_e[order], flat_tok[order], flat_w[order]
    counts = jnp.bincount(flat_e, length=N_EXPERTS)
    padded = ((counts + MOE_BLOCK - 1) // MOE_BLOCK) * MOE_BLOCK
    start = jnp.cumsum(counts) - counts
    pend = jnp.cumsum(padded)
    pstart = pend - padded
    dest = pstart[se] + (jnp.arange(A) - start[se])
    P = A + N_EXPERTS * MOE_BLOCK
    nblk = P // MOE_BLOCK
    row_tok = jnp.zeros((P,), jnp.int32).at[dest].set(stok)
    row_w = jnp.zeros((P,), h.dtype).at[dest].set(sw)
    blk_expert = jnp.clip(jnp.searchsorted(pend, jnp.arange(nblk) * MOE_BLOCK, side='right'), 0, N_EXPERTS - 1)

    def expert_block(args):
        e, tok = args
        xb = hf[tok]
        return (jax.nn.silu(xb @ w_gate[e]) * (xb @ w_up[e])) @ w_down[e]

    y = lax.map(expert_block, (blk_expert, row_tok.reshape(nblk, MOE_BLOCK))).reshape(P, D)
    out = jax.ops.segment_sum(y * row_w[:, None], row_tok, num_segments=T)
    return out.reshape(B, S, D)


def setup_inputs(seed: int = 0) -> dict:
    key = jax.random.key(seed)
    ks = iter(jax.random.split(key, 32))
    L, D = DEPTH, D_MODEL

    def nrm(shape, scale):
        return jax.random.normal(next(ks), shape, jnp.float32) * scale

    return {
        'x': nrm((BATCH, SEQ, D), 1.0),
        'p': nrm((DEPTH, BATCH, SEQ, PLE_DIM), 1.0),
        'norm_mix_g': 1.0 + nrm((L, D), 0.02),
        'w_in': nrm((L, D, IN_COLS), D ** -0.5),
        'gla_w_alpha': nrm((L, GLA_GATE_RANK, GLA_QK), GLA_GATE_RANK ** -0.5),
        'gla_b_alpha': nrm((L, GLA_QK), 0.1),
        'gla_norm_g': 1.0 + nrm((L, GLA_DV), 0.02),
        'w_out_gla': nrm((L, GLA_V, D), GLA_V ** -0.5),
        'q_norm_g': 1.0 + nrm((L, DSA_HD), 0.02),
        'k_norm_g': 1.0 + nrm((L, DSA_HD), 0.02),
        'rel_bias': nrm((REL_BUCKETS, DSA_HEADS), 0.5),
        'w_out_dsa': nrm((L, DSA_W, D), DSA_W ** -0.5),
        'w_branch_gate': nrm((L, D, 2 * D), D ** -0.5),
        'b_branch_gate': nrm((L, 2 * D), 0.01),
        'w_out': nrm((L, D, D), D ** -0.5),
        'norm_ffn_g': 1.0 + nrm((L, D), 0.02),
        'w_group_router': nrm((L, D, N_GROUPS), D ** -0.5),
        'b_group_router': nrm((L, N_GROUPS), 0.01),
        'w_expert_router': nrm((L, D, N_EXPERTS), D ** -0.5),
        'b_expert_router': nrm((L, N_EXPERTS), 0.01),
        'w_exp_gate': nrm((L, N_EXPERTS, D, EXPERT_FF), D ** -0.5),
        'w_exp_up': nrm((L, N_EXPERTS, D, EXPERT_FF), D ** -0.5),
        'w_exp_down': nrm((L, N_EXPERTS, EXPERT_FF, D), EXPERT_FF ** -0.5),
        'norm_ple_g': 1.0 + nrm((L, D), 0.02),
        'w_ple_gate': nrm((L, D, D), D ** -0.5),
        'b_ple_gate': nrm((L, D), 0.01),
        'w_ple_proj': nrm((L, PLE_DIM, D), PLE_DIM ** -0.5),
    }


def reference(x, p, norm_mix_g, w_in, gla_w_alpha, gla_b_alpha, gla_norm_g, w_out_gla, q_norm_g, k_norm_g, rel_bias, w_out_dsa, w_branch_gate, b_branch_gate, w_out, norm_ffn_g, w_group_router, b_group_router, w_expert_router, b_expert_router, w_exp_gate, w_exp_up, w_exp_down, norm_ple_g, w_ple_gate, b_ple_gate, w_ple_proj):
    B, S, D = x.shape
    for i in range(DEPTH):
        h = rms_norm(x, norm_mix_g[i])
        z = h @ w_in[i]
        gq, gk, gv, gr, glr, dq, dk, dv, iq, ik, iw = split_cols(z)
        q_a = gq.reshape(B, S, GLA_HEADS, GLA_DK) * (GLA_DK ** -0.5)
        k_a = gk.reshape(B, S, GLA_HEADS, GLA_DK)
        v_a = gv.reshape(B, S, GLA_HEADS, GLA_DV)
        g_a = (jax.nn.log_sigmoid(glr @ gla_w_alpha[i] + gla_b_alpha[i]) / GLA_GATE_TEMP).reshape(B, S, GLA_HEADS, GLA_DK)
        o_a = gla_mixer(q_a, k_a, v_a, g_a)
        o_a = rms_norm(o_a, gla_norm_g[i]).reshape(B, S, GLA_V) * jax.nn.silu(gr)
        y_a = o_a @ w_out_gla[i]
        q_b = rms_norm(dq.reshape(B, S, DSA_HEADS, DSA_HD), q_norm_g[i])
        k_b = rms_norm(dk.reshape(B, S, DSA_HEADS, DSA_HD), k_norm_g[i])
        v_b = dv.reshape(B, S, DSA_HEADS, DSA_HD)
        q_i = iq.reshape(B, S, IDX_HEADS, IDX_DIM)
        w_i = iw * ((IDX_HEADS ** -0.5) * (IDX_DIM ** -0.5))
        o_b = dsa_mixer(q_b, k_b, v_b, q_i, ik, w_i, rel_bias)
        y_b = o_b @ w_out_dsa[i]
        gates = jax.nn.sigmoid(h @ w_branch_gate[i] + b_branch_gate[i])
        gate_a, gate_b = jnp.split(gates, 2, axis=-1)
        x = x + (gate_a * y_a + gate_b * y_b) @ w_out[i]
        x = x + hier_moe(rms_norm(x, norm_ffn_g[i]), w_group_router[i], b_group_router[i], w_expert_router[i], b_expert_router[i], w_exp_gate[i], w_exp_up[i], w_exp_down[i])
        ple = p[i] @ w_ple_proj[i]
        ple_gate = jax.nn.sigmoid(rms_norm(x, norm_ple_g[i]) @ w_ple_gate[i] + b_ple_gate[i])
        x = x + ple_gate * ple
    return x
```

```python
import functools
import math

import jax
import jax.numpy as jnp
from jax import lax
from jax.experimental import pallas as pl
from jax.experimental.pallas import tpu as pltpu

F32 = jnp.float32
BF16 = jnp.bfloat16
I32 = jnp.int32

EPS = 1e-6
CHUNK = 64
GLA_HEADS = 8
GLA_DK = 64
GLA_DV = 128
GLA_GATE_RANK = 16
GLA_GATE_TEMP = 16.0
GLA_QK = GLA_HEADS * GLA_DK
GLA_V = GLA_HEADS * GLA_DV
DSA_HEADS = 8
DSA_HD = 128
DSA_W = DSA_HEADS * DSA_HD
IDX_HEADS = 16
IDX_DIM = 64
IDX_Q = IDX_HEADS * IDX_DIM
INDEX_TOPK = 256
REL_BUCKETS = 32
REL_MAX_DIST = 128
N_GROUPS = 8
EXPERTS_PER_GROUP = 8
N_EXPERTS = N_GROUPS * EXPERTS_PER_GROUP
EXPERT_FF = 512
TOP_K_INNER = 2

LANES = 128
VMEM_LIMIT = 56 * 1024 * 1024
NEG_BIG = -1e30
INT_MIN = -(2 ** 31)
KEY_NEG_INF = -2139095041

GLA_CT = 512
IDX_QG = 256
ATT_T = 512
MOE_BLK = 256
RANK_RB = 1024
ROW_T = 256


def _cp(sem, **kw):
    return pltpu.CompilerParams(dimension_semantics=sem, vmem_limit_bytes=VMEM_LIMIT, **kw)


def _dot(a, b):
    return jnp.dot(a, b, preferred_element_type=F32)


def _dot_nt(a, b):
    return lax.dot_general(a, b, (((1,), (1,)), ((), ())), preferred_element_type=F32)


def _dot_tn(a, b):
    return lax.dot_general(a, b, (((0,), (0,)), ((), ())), preferred_element_type=F32)


def _sigmoid(x):
    return 1.0 / (1.0 + jnp.exp(-x))


def _rmsnorm_kernel(x_ref, g_ref, o_ref):
    x = x_ref[...]
    ms = jnp.mean(x * x, axis=-1, keepdims=True)
    o_ref[...] = (x * lax.rsqrt(ms + EPS) * g_ref[...]).astype(o_ref.dtype)


def rmsnorm(x, g, out_dtype, tm=512):
    m, d = x.shape
    tm = min(tm, m)
    return pl.pallas_call(
        _rmsnorm_kernel,
        grid=(m // tm,),
        in_specs=[pl.BlockSpec((tm, d), lambda i: (i, 0)), pl.BlockSpec((1, d), lambda i: (0, 0))],
        out_specs=pl.BlockSpec((tm, d), lambda i: (i, 0)),
        out_shape=jax.ShapeDtypeStruct((m, d), out_dtype),
        compiler_params=_cp(("parallel",)),
        name="rmsnorm",
    )(x, g.reshape(1, d).astype(F32))


def _mm_kernel(*refs, n_extra, epilogue):
    a_ref, w_ref = refs[0], refs[1]
    extras = refs[2:2 + n_extra]
    o_ref = refs[2 + n_extra]
    acc = _dot(a_ref[...], w_ref[...])
    epilogue(acc, o_ref, *extras)


def matmul(a, w, epilogue, extras, out_dtype, name, tm=1024, tn=512):
    m, k = a.shape
    n = w.shape[1]
    tm, tn = min(tm, m), min(tn, n)
    specs = [pl.BlockSpec((tm, k), lambda i, j: (i, 0)), pl.BlockSpec((k, tn), lambda i, j: (0, j))]
    args = [a, w]
    for arr, kind in extras:
        if kind == "tile":
            specs.append(pl.BlockSpec((tm, tn), lambda i, j: (i, j)))
        elif kind == "row":
            specs.append(pl.BlockSpec((1, tn), lambda i, j: (0, j)))
        elif kind == "mk":
            specs.append(pl.BlockSpec((tm, arr.shape[1]), lambda i, j: (i, 0)))
        elif kind == "kn":
            specs.append(pl.BlockSpec((arr.shape[0], tn), lambda i, j: (0, j)))
        else:
            raise ValueError(kind)
        args.append(arr)
    return pl.pallas_call(
        functools.partial(_mm_kernel, n_extra=len(extras), epilogue=epilogue),
        grid=(m // tm, n // tn),
        in_specs=specs,
        out_specs=pl.BlockSpec((tm, tn), lambda i, j: (i, j)),
        out_shape=jax.ShapeDtypeStruct((m, n), out_dtype),
        compiler_params=_cp(("parallel", "arbitrary")),
        name=name,
    )(*args)


def _epi_plain(acc, o_ref):
    o_ref[...] = acc.astype(o_ref.dtype)


def _epi_headnorm(acc, o_ref, g_ref):
    for c in range(acc.shape[1] // DSA_HD):
        sl = slice(c * DSA_HD, (c + 1) * DSA_HD)
        blk = acc[:, sl]
        ms = jnp.mean(blk * blk, axis=-1, keepdims=True)
        o_ref[:, sl] = (blk * lax.rsqrt(ms + EPS) * g_ref[:, sl]).astype(o_ref.dtype)


def _epi_sigmoid_bias(acc, o_ref, b_ref):
    o_ref[...] = _sigmoid(acc + b_ref[...]).astype(o_ref.dtype)


def _epi_residual(acc, o_ref, x_ref):
    o_ref[...] = (x_ref[...] + acc).astype(o_ref.dtype)


def _epi_ple(acc, o_ref, b_ref, x_ref, p_ref, wp_ref):
    ple = _dot(p_ref[...], wp_ref[...])
    o_ref[...] = (x_ref[...] + _sigmoid(acc + b_ref[...]) * ple).astype(o_ref.dtype)


def _mix_kernel(oa_ref, ob_ref, wa_ref, wb_ref, ga_ref, gb_ref, o_ref):
    ya = _dot(oa_ref[...], wa_ref[...])
    yb = _dot(ob_ref[...], wb_ref[...])
    o_ref[...] = (ga_ref[...].astype(F32) * ya + gb_ref[...].astype(F32) * yb).astype(o_ref.dtype)


def mix(oa, ob, wa, wb, gates, tm=1024, tn=512):
    m, k = oa.shape
    n = wa.shape[1]
    tm, tn = min(tm, m), min(tn, n)
    nj = n // tn
    return pl.pallas_call(
        _mix_kernel,
        grid=(m // tm, nj),
        in_specs=[
            pl.BlockSpec((tm, k), lambda i, j: (i, 0)),
            pl.BlockSpec((tm, k), lambda i, j: (i, 0)),
            pl.BlockSpec((k, tn), lambda i, j: (0, j)),
            pl.BlockSpec((k, tn), lambda i, j: (0, j)),
            pl.BlockSpec((tm, tn), lambda i, j: (i, j)),
            pl.BlockSpec((tm, tn), lambda i, j: (i, j + nj)),
        ],
        out_specs=pl.BlockSpec((tm, tn), lambda i, j: (i, j)),
        out_shape=jax.ShapeDtypeStruct((m, n), BF16),
        compiler_params=_cp(("parallel", "arbitrary")),
        name="mix",
    )(oa, ob, wa, wb, gates, gates)


def _gla_kernel(q_ref, k_ref, v_ref, gr_ref, zs_ref, wa_ref, ba_ref, gn_ref, o_ref, st_ref, *, nchunk):
    @pl.when(pl.program_id(2) == 0)
    def _():
        st_ref[...] = jnp.zeros_like(st_ref)

    glr = zs_ref[:, 0:GLA_GATE_RANK].astype(BF16)
    logit = _dot(glr, wa_ref[...]) + ba_ref[...]
    g = (jnp.minimum(logit, 0.0) - jnp.log(1.0 + jnp.exp(-jnp.abs(logit)))) * (1.0 / GLA_GATE_TEMP)
    g_hi = g.astype(BF16)
    g_lo = (g - g_hi.astype(F32)).astype(BF16)
    row = lax.broadcasted_iota(I32, (CHUNK, CHUNK), 0)
    col = lax.broadcasted_iota(I32, (CHUNK, CHUNK), 1)
    causal = row >= col
    tril = jnp.where(causal, 1.0, 0.0).astype(BF16)
    gn = gn_ref[...]
    mid = CHUNK // 2 - 1
    for c in range(nchunk):
        sl = slice(c * CHUNK, (c + 1) * CHUNK)
        G = _dot(tril, g_hi[sl]) + _dot(tril, g_lo[sl])
        qc = q_ref[sl, :].astype(F32) * (GLA_DK ** -0.5)
        kc = k_ref[sl, :].astype(F32)
        vc = v_ref[sl, :]
        c0 = G[mid:mid + 1, :]
        gl = G[CHUNK - 1:CHUNK, :]
        qt = qc * jnp.exp(G - c0)
        kt = kc * jnp.exp(c0 - G)
        sc = jnp.where(causal, _dot_nt(qt.astype(BF16), kt.astype(BF16)), 0.0)
        o = _dot(sc.astype(BF16), vc)
        st = st_ref[...]
        qg = qt * jnp.exp(c0)
        o = o + _dot_nt(qg.astype(BF16), st.astype(BF16))
        kd = kt * jnp.exp(gl - c0)
        st_ref[...] = st * jnp.exp(gl) + _dot_tn(vc, kd.astype(BF16))
        ms = jnp.mean(o * o, axis=-1, keepdims=True)
        gr = gr_ref[sl, :].astype(F32)
        o_ref[sl, :] = (o * lax.rsqrt(ms + EPS) * gn * (gr * _sigmoid(gr))).astype(o_ref.dtype)


def gla(q_hm, k_hm, zg, zs, wa, ba, gn, b, s):
    ct = min(GLA_CT, s)
    ns = s // ct
    v_blk0 = (2 * GLA_QK) // GLA_DV
    gr_blk0 = (2 * GLA_QK + GLA_V) // GLA_DV
    return pl.pallas_call(
        functools.partial(_gla_kernel, nchunk=ct // CHUNK),
        grid=(b, GLA_HEADS, ns),
        in_specs=[
            pl.BlockSpec((None, None, ct, GLA_DK), lambda bi, h, si: (bi, h, si, 0)),
            pl.BlockSpec((None, None, ct, GLA_DK), lambda bi, h, si: (bi, h, si, 0)),
            pl.BlockSpec((ct, GLA_DV), lambda bi, h, si: (bi * ns + si, v_blk0 + h)),
            pl.BlockSpec((ct, GLA_DV), lambda bi, h, si: (bi * ns + si, gr_blk0 + h)),
            pl.BlockSpec((ct, LANES), lambda bi, h, si: (bi * ns + si, 0)),
            pl.BlockSpec((None, GLA_GATE_RANK, GLA_DK), lambda bi, h, si: (h, 0, 0)),
            pl.BlockSpec((None, 1, GLA_DK), lambda bi, h, si: (h, 0, 0)),
            pl.BlockSpec((1, GLA_DV), lambda bi, h, si: (0, 0)),
        ],
        out_specs=pl.BlockSpec((ct, GLA_DV), lambda bi, h, si: (bi * ns + si, h)),
        out_shape=jax.ShapeDtypeStruct((b * s, GLA_V), BF16),
        scratch_shapes=[pltpu.VMEM((GLA_DV, GLA_DK), F32)],
        compiler_params=_cp(("parallel", "parallel", "arbitrary")),
        name="gla",
    )(q_hm, k_hm, zg, zg, zs, wa, ba, gn)


def _idx_kernel(iq_ref, zs_ref, ik_ref, mask_ref, key_ref, wb_ref, *, qg, s):
    g = pl.program_id(1)
    nt = g + 1
    half = qg // 2
    w = zs_ref[:, GLA_GATE_RANK:GLA_GATE_RANK + IDX_HEADS] * ((IDX_HEADS ** -0.5) * (IDX_DIM ** -0.5))
    for j in range(IDX_HEADS):
        wb_ref[j] = jnp.broadcast_to(w[:, j:j + 1], (qg, LANES))

    qchunk = lax.broadcasted_iota(I32, (qg, qg), 0) // CHUNK
    kchunk = lax.broadcasted_iota(I32, (qg, qg), 1) // CHUNK
    adm_diag = kchunk <= qchunk

    def score_tile(t, carry):
        off = pl.multiple_of(t * qg, qg)
        ikt = ik_ref[pl.ds(off, qg), :]
        accs = [jnp.zeros((qg, LANES), F32) for _ in range(qg // LANES)]
        for j in range(IDX_HEADS):
            d = _dot_nt(iq_ref[j], ikt)
            wj = wb_ref[j]
            for c in range(qg // LANES):
                accs[c] = accs[c] + jnp.maximum(d[:, c * LANES:(c + 1) * LANES], 0.0) * wj
        acc = jnp.concatenate(accs, axis=1)
        bits = lax.bitcast_convert_type(acc, I32)
        key = bits ^ ((bits >> 31) & 0x7FFFFFFF)
        key = jnp.where(jnp.logical_or(t < g, adm_diag), key, KEY_NEG_INF)
        key_ref[:, pl.ds(off, qg)] = key
        return carry

    lax.fori_loop(0, nt, score_tile, 0)

    kf = float(min(INDEX_TOPK, s // 4))

    def count_ge(cand):
        def body(t, acc):
            off = pl.multiple_of(t * qg, qg)
            kk = key_ref[:, pl.ds(off, qg)]
            m = jnp.where(kk >= cand, 1.0, 0.0)
            for c in range(qg // LANES):
                acc = acc + m[:, c * LANES:(c + 1) * LANES]
            return acc
        acc = lax.fori_loop(0, nt, body, jnp.zeros((qg, LANES), F32))
        return jnp.sum(acc, axis=1, keepdims=True)

    zero = jnp.zeros((qg, 1), I32)
    t0 = jnp.where(count_ge(zero) >= kf, zero, zero + INT_MIN)

    def bit_body(i, thr):
        cand = thr + jnp.left_shift(jnp.int32(1), 30 - i)
        return jnp.where(count_ge(cand) >= kf, cand, thr)

    thr = lax.fori_loop(0, 31, bit_body, t0)
    thr = jnp.maximum(thr, KEY_NEG_INF + 1)

    def write_tile(t, carry):
        off = pl.multiple_of(t * qg, qg)
        kk = key_ref[:, pl.ds(off, qg)]
        mask_ref[:, pl.ds(off, qg)] = jnp.where(kk >= thr, 0.0, NEG_BIG).astype(BF16)
        return carry

    lax.fori_loop(0, nt, write_tile, 0)

    def fill_tile(t, carry):
        off = pl.multiple_of(t * qg, qg)
        mask_ref[:, pl.ds(off, qg)] = jnp.full((qg, qg), NEG_BIG, BF16)
        return carry

    lax.fori_loop(nt, s // qg, fill_tile, 0)


def dsa_index(iq_hm, zs, ik, b, s):
    qg = min(IDX_QG, s)
    ng = s // qg
    return pl.pallas_call(
        functools.partial(_idx_kernel, qg=qg, s=s),
        grid=(b, ng),
        in_specs=[
            pl.BlockSpec((None, IDX_HEADS, qg, IDX_DIM), lambda bi, gi: (bi, 0, gi, 0)),
            pl.BlockSpec((qg, LANES), lambda bi, gi: (bi * ng + gi, 0)),
            pl.BlockSpec((None, s, IDX_DIM), lambda bi, gi: (bi, 0, 0)),
        ],
        out_specs=pl.BlockSpec((qg, s), lambda bi, gi: (bi * ng + gi, 0)),
        out_shape=jax.ShapeDtypeStruct((b * s, s), BF16),
        scratch_shapes=[pltpu.VMEM((qg, s), I32), pltpu.VMEM((IDX_HEADS, qg, LANES), F32)],
        compiler_params=_cp(("parallel", "arbitrary")),
        name="dsa_index",
    )(iq_hm, zs, ik)


def _attn_kernel(qi_ref, ki_ref, q_ref, k_ref, v_ref, mask_ref, nbd_ref, nbp_ref, o_ref, m_ref, l_ref, acc_ref):
    p = pl.program_id(1)
    qi = qi_ref[p]
    ki = ki_ref[p]

    @pl.when(ki == 0)
    def _():
        m_ref[...] = jnp.full_like(m_ref, NEG_BIG)
        l_ref[...] = jnp.zeros_like(l_ref)
        acc_ref[...] = jnp.zeros_like(acc_ref)

    def heads(bias_of_head):
        for h in range(DSA_HEADS):
            sl = slice(h * DSA_HD, (h + 1) * DSA_HD)
            sc = _dot_nt(q_ref[:, sl], k_ref[:, sl]) + bias_of_head(h)
            m_old = m_ref[h]
            m_new = jnp.maximum(m_old, jnp.max(sc, axis=1, keepdims=True))
            alpha = jnp.exp(m_old - m_new)
            pr = jnp.exp(sc - m_new)
            l_ref[h] = alpha * l_ref[h] + jnp.sum(pr, axis=1, keepdims=True)
            acc_ref[:, sl] = alpha * acc_ref[:, sl] + _dot(pr.astype(BF16), v_ref[:, sl])
            m_ref[h] = m_new

    @pl.when(ki < qi - 1)
    def _():
        mb = mask_ref[...].astype(F32)
        heads(lambda h: mb)

    @pl.when(ki == qi - 1)
    def _():
        mb = mask_ref[...].astype(F32)
        heads(lambda h: mb + nbp_ref[h].astype(F32))

    @pl.when(ki == qi)
    def _():
        mb = mask_ref[...].astype(F32)
        heads(lambda h: mb + nbd_ref[h].astype(F32))
        for h in range(DSA_HEADS):
            sl = slice(h * DSA_HD, (h + 1) * DSA_HD)
            o_ref[:, sl] = (acc_ref[:, sl] / l_ref[h]).astype(o_ref.dtype)


def dsa_attention(zqk, zvi, mask, nbd, nbp, b, s):
    t = min(ATT_T, s)
    ng = s // t
    pairs = [(qi, ki) for qi in range(ng) for ki in range(qi + 1)]
    qi_tab = jnp.asarray([p[0] for p in pairs], I32)
    ki_tab = jnp.asarray([p[1] for p in pairs], I32)
    grid_spec = pltpu.PrefetchScalarGridSpec(
        num_scalar_prefetch=2,
        grid=(b, len(pairs)),
        in_specs=[
            pl.BlockSpec((t, DSA_W), lambda bi, p, qt, kt: (bi * ng + qt[p], 0)),
            pl.BlockSpec((t, DSA_W), lambda bi, p, qt, kt: (bi * ng + kt[p], 1)),
            pl.BlockSpec((t, DSA_W), lambda bi, p, qt, kt: (bi * ng + kt[p], 0)),
            pl.BlockSpec((t, t), lambda bi, p, qt, kt: (bi * ng + qt[p], kt[p])),
            pl.BlockSpec((DSA_HEADS, t, t), lambda bi, p, qt, kt: (0, 0, 0)),
            pl.BlockSpec((DSA_HEADS, t, t), lambda bi, p, qt, kt: (0, 0, 0)),
        ],
        out_specs=pl.BlockSpec((t, DSA_W), lambda bi, p, qt, kt: (bi * ng + qt[p], 0)),
        scratch_shapes=[
            pltpu.VMEM((DSA_HEADS, t, 1), F32),
            pltpu.VMEM((DSA_HEADS, t, 1), F32),
            pltpu.VMEM((t, DSA_W), F32),
        ],
    )
    return pl.pallas_call(
        _attn_kernel,
        grid_spec=grid_spec,
        out_shape=jax.ShapeDtypeStruct((b * s, DSA_W), BF16),
        compiler_params=_cp(("parallel", "arbitrary")),
        name="dsa_attention",
    )(qi_tab, ki_tab, zqk, zqk, zvi, mask, nbd, nbp)


def _t5_bucket(rel):
    half = REL_BUCKETS // 2
    exact = half // 2
    sign = jnp.where(rel > 0, half, 0)
    n = jnp.abs(rel)
    nf = jnp.maximum(n, 1).astype(F32)
    large = exact + (jnp.log(nf / exact) / math.log(REL_MAX_DIST / exact) * (half - exact)).astype(I32)
    large = jnp.minimum(large, half - 1)
    return sign + jnp.where(n < exact, n, large)


def _near_bias_tables(rel_bias, t):
    qpos = jnp.arange(t, dtype=I32)[:, None]
    kpos = jnp.arange(t, dtype=I32)[None, :]
    far = rel_bias[REL_BUCKETS // 2 - 1]
    nbd = rel_bias[_t5_bucket(kpos - qpos)] - far
    nbp = rel_bias[_t5_bucket(kpos - t - qpos)] - far
    return (jnp.moveaxis(nbd, -1, 0).astype(BF16), jnp.moveaxis(nbp, -1, 0).astype(BF16))


def _router_kernel(x_ref, g_ref, w_ref, b_ref, hn_ref, ri_ref, rw_ref):
    x = x_ref[...]
    ms = jnp.mean(x * x, axis=-1, keepdims=True)
    hn = x * lax.rsqrt(ms + EPS) * g_ref[...]
    hn_ref[...] = hn
    logits = _dot(hn.astype(BF16), w_ref[...]) + b_ref[...]
    lane = lax.broadcasted_iota(I32, logits.shape, 1)
    lanef = lane.astype(F32)
    big = 1e9
    is_grp = lane < N_GROUPS
    gl = jnp.where(is_grp, logits, -jnp.inf)
    gmax = jnp.max(gl, axis=-1, keepdims=True)
    grp = jnp.min(jnp.where(gl == gmax, lanef, big), axis=-1, keepdims=True)
    gsum = jnp.sum(jnp.where(is_grp, jnp.exp(logits - gmax), 0.0), axis=-1, keepdims=True)
    grp_w = 1.0 / gsum
    elo = N_GROUPS + grp * EXPERTS_PER_GROUP
    in_grp = jnp.logical_and(lanef >= elo, lanef < elo + EXPERTS_PER_GROUP)
    el = jnp.where(in_grp, logits, -jnp.inf)
    v1 = jnp.max(el, axis=-1, keepdims=True)
    i1 = jnp.min(jnp.where(el == v1, lanef, big), axis=-1, keepdims=True)
    el2 = jnp.where(lanef == i1, -jnp.inf, el)
    v2 = jnp.max(el2, axis=-1, keepdims=True)
    i2 = jnp.min(jnp.where(el2 == v2, lanef, big), axis=-1, keepdims=True)
    e2 = jnp.exp(v2 - v1)
    w1 = grp_w * (1.0 / (1.0 + e2))
    w2 = grp_w * (e2 / (1.0 + e2))
    ri_ref[...] = jnp.where(lane == 0, i1 - N_GROUPS, jnp.where(lane == 1, i2 - N_GROUPS, 0.0)).astype(I32)
    rw_ref[...] = jnp.where(lane == 0, w1, jnp.where(lane == 1, w2, 0.0))


def router(x1, g, w_r, b_r, tm=256):
    m, d = x1.shape
    tm = min(tm, m)
    return pl.pallas_call(
        _router_kernel,
        grid=(m // tm,),
        in_specs=[
            pl.BlockSpec((tm, d), lambda i: (i, 0)),
            pl.BlockSpec((1, d), lambda i: (0, 0)),
            pl.BlockSpec((d, LANES), lambda i: (0, 0)),
            pl.BlockSpec((1, LANES), lambda i: (0, 0)),
        ],
        out_specs=[
            pl.BlockSpec((tm, d), lambda i: (i, 0)),
            pl.BlockSpec((tm, LANES), lambda i: (i, 0)),
            pl.BlockSpec((tm, LANES), lambda i: (i, 0)),
        ],
        out_shape=[
            jax.ShapeDtypeStruct((m, d), F32),
            jax.ShapeDtypeStruct((m, LANES), I32),
            jax.ShapeDtypeStruct((m, LANES), F32),
        ],
        compiler_params=_cp(("parallel",)),
        name="router",
    )(x1, g.reshape(1, d).astype(F32), w_r, b_r)


def _rank_kernel(e_ref, rank_ref, cnt_ref, run_ref, *, rb):
    @pl.when(pl.program_id(0) == 0)
    def _():
        run_ref[...] = jnp.zeros_like(run_ref)

    e = e_ref[...]
    sub = lax.broadcasted_iota(I32, (N_EXPERTS, rb), 0)
    hit = sub == e
    oh = jnp.where(hit, 1.0, 0.0)
    a = lax.broadcasted_iota(I32, (rb, rb), 0)
    bcol = lax.broadcasted_iota(I32, (rb, rb), 1)
    upper = jnp.where(a < bcol, 1.0, 0.0).astype(BF16)
    prefix = _dot(oh.astype(BF16), upper)
    run = run_ref[...]
    rank = jnp.sum(jnp.where(hit, prefix + run[:, 0:1], 0.0), axis=0, keepdims=True)
    rank_ref[...] = rank.astype(I32)
    new = run + jnp.sum(oh, axis=1, keepdims=True)
    run_ref[...] = new
    cnt_ref[...] = new


def moe_rank(eid_flat):
    a = eid_flat.shape[0]
    rb = min(RANK_RB, a)
    nb = a // rb
    rank, cnt = pl.pallas_call(
        functools.partial(_rank_kernel, rb=rb),
        grid=(nb,),
        in_specs=[pl.BlockSpec((None, 1, rb), lambda i: (i, 0, 0))],
        out_specs=[
            pl.BlockSpec((None, 1, rb), lambda i: (i, 0, 0)),
            pl.BlockSpec((N_EXPERTS, LANES), lambda i: (0, 0)),
        ],
        out_shape=[
            jax.ShapeDtypeStruct((nb, 1, rb), I32),
            jax.ShapeDtypeStruct((N_EXPERTS, LANES), F32),
        ],
        scratch_shapes=[pltpu.VMEM((N_EXPERTS, LANES), F32)],
        compiler_params=_cp(("arbitrary",)),
        name="moe_rank",
    )(eid_flat.reshape(nb, 1, rb))
    return rank.reshape(a), cnt[:, 0].astype(I32)


def _row_copy(src, dst, sem):
    return pltpu.make_async_copy(src, dst, sem)


def _dispatch_kernel(dest_ref, hn_ref, xs_in_ref, xs_ref, sem, *, tm):
    del xs_in_ref
    base = pl.program_id(0) * (tm * TOP_K_INNER)

    def body(r, carry):
        for slot in range(TOP_K_INNER):
            d = dest_ref[base + r * TOP_K_INNER + slot]
            _row_copy(hn_ref.at[pl.ds(r, 1), :], xs_ref.at[pl.ds(d, 1), :], sem).start()
        return carry

    lax.fori_loop(0, tm, body, 0)
    for _ in range(TOP_K_INNER):
        _row_copy(hn_ref, xs_ref.at[pl.ds(0, tm), :], sem).wait()


def moe_dispatch(dest, hn, xs0):
    m, d = hn.shape
    tm = min(ROW_T, m)
    grid_spec = pltpu.PrefetchScalarGridSpec(
        num_scalar_prefetch=1,
        grid=(m // tm,),
        in_specs=[
            pl.BlockSpec((tm, d), lambda i, dst: (i, 0)),
            pl.BlockSpec(memory_space=pl.ANY),
        ],
        out_specs=pl.BlockSpec(memory_space=pl.ANY),
        scratch_shapes=[pltpu.SemaphoreType.DMA(())],
    )
    return pl.pallas_call(
        functools.partial(_dispatch_kernel, tm=tm),
        grid_spec=grid_spec,
        out_shape=jax.ShapeDtypeStruct(xs0.shape, xs0.dtype),
        input_output_aliases={2: 0},
        compiler_params=_cp(("arbitrary",), has_side_effects=True),
        name="moe_dispatch",
    )(dest, hn, xs0)


def _expert_kernel(be_ref, nv_ref, xs_ref, wg_ref, wu_ref, wd_ref, y_ref, wg_bf, wu_bf, wd_bf):
    i = pl.program_id(0)
    nv = nv_ref[0]
    prev = be_ref[jnp.maximum(i - 1, 0)]
    fresh = jnp.logical_or(i == 0, be_ref[i] != prev)

    @pl.when(jnp.logical_and(i < nv, fresh))
    def _():
        wg_bf[...] = wg_ref[...].astype(BF16)
        wu_bf[...] = wu_ref[...].astype(BF16)
        wd_bf[...] = wd_ref[...].astype(BF16)

    @pl.when(i < nv)
    def _():
        xb = xs_ref[...].astype(BF16)
        hg = _dot(xb, wg_bf[...])
        hu = _dot(xb, wu_bf[...])
        act = (hg * _sigmoid(hg) * hu).astype(BF16)
        y_ref[...] = _dot(act, wd_bf[...])

    @pl.when(i >= nv)
    def _():
        y_ref[...] = jnp.zeros_like(y_ref)


def moe_experts(blk_expert, nvalid, xs, w_gate, w_up, w_down):
    p, d = xs.shape
    nblk = p // MOE_BLK
    ff = w_gate.shape[2]

    def blk(i, be, nv):
        return jnp.minimum(i, nv[0] - 1)

    grid_spec = pltpu.PrefetchScalarGridSpec(
        num_scalar_prefetch=2,
        grid=(nblk,),
        in_specs=[
            pl.BlockSpec((MOE_BLK, d), lambda i, be, nv: (blk(i, be, nv), 0)),
            pl.BlockSpec((None, d, ff), lambda i, be, nv: (be[blk(i, be, nv)], 0, 0)),
            pl.BlockSpec((None, d, ff), lambda i, be, nv: (be[blk(i, be, nv)], 0, 0)),
            pl.BlockSpec((None, ff, d), lambda i, be, nv: (be[blk(i, be, nv)], 0, 0)),
        ],
        out_specs=pl.BlockSpec((MOE_BLK, d), lambda i, be, nv: (i, 0)),
        scratch_shapes=[pltpu.VMEM((d, ff), BF16), pltpu.VMEM((d, ff), BF16), pltpu.VMEM((ff, d), BF16)],
    )
    return pl.pallas_call(
        _expert_kernel,
        grid_spec=grid_spec,
        out_shape=jax.ShapeDtypeStruct((p, d), F32),
        compiler_params=_cp(("arbitrary",)),
        name="moe_experts",
    )(blk_expert, nvalid, xs, w_gate, w_up, w_down)


def _combine_kernel(dest_ref, x_ref, rw_ref, y_ref, o_ref, ybuf, sem, *, tm):
    base = pl.program_id(0) * (tm * TOP_K_INNER)

    def body(r, carry):
        for slot in range(TOP_K_INNER):
            d = dest_ref[base + r * TOP_K_INNER + slot]
            _row_copy(y_ref.at[pl.ds(d, 1), :], ybuf.at[slot, pl.ds(r, 1), :], sem).start()
        return carry

    lax.fori_loop(0, tm, body, 0)
    for slot in range(TOP_K_INNER):
        _row_copy(y_ref.at[pl.ds(0, tm), :], ybuf.at[slot], sem).wait()
    rw = rw_ref[...]
    o_ref[...] = x_ref[...] + rw[:, 0:1] * ybuf[0] + rw[:, 1:2] * ybuf[1]


def moe_combine(dest, x1, rw, y):
    m, d = x1.shape
    tm = min(ROW_T, m)
    grid_spec = pltpu.PrefetchScalarGridSpec(
        num_scalar_prefetch=1,
        grid=(m // tm,),
        in_specs=[
            pl.BlockSpec((tm, d), lambda i, dst: (i, 0)),
            pl.BlockSpec((tm, LANES), lambda i, dst: (i, 0)),
            pl.BlockSpec(memory_space=pl.ANY),
        ],
        out_specs=pl.BlockSpec((tm, d), lambda i, dst: (i, 0)),
        scratch_shapes=[pltpu.VMEM((TOP_K_INNER, tm, d), F32), pltpu.SemaphoreType.DMA(())],
    )
    return pl.pallas_call(
        functools.partial(_combine_kernel, tm=tm),
        grid_spec=grid_spec,
        out_shape=jax.ShapeDtypeStruct((m, d), F32),
        compiler_params=_cp(("arbitrary",)),
        name="moe_combine",
    )(dest, x1, rw, y)


def _layer(x, p, norm_mix_g, w_in, gla_w_alpha, gla_b_alpha, gla_norm_g, w_out_gla, q_norm_g, k_norm_g, rel_bias,
           w_out_dsa, w_branch_gate, b_branch_gate, w_out, norm_ffn_g, w_group_router, b_group_router,
           w_expert_router, b_expert_router, w_exp_gate, w_exp_up, w_exp_down, norm_ple_g, w_ple_gate,
           b_ple_gate, w_ple_proj):
    b, s, d = x.shape
    t = b * s
    x2d = x.reshape(t, d)

    o_gq, o_gk, o_gv, o_gr = 0, GLA_QK, 2 * GLA_QK, 2 * GLA_QK + GLA_V
    o_glr = o_gr + GLA_V
    o_dq = o_glr + GLA_GATE_RANK
    o_dk, o_dv = o_dq + DSA_W, o_dq + 2 * DSA_W
    o_iq = o_dv + DSA_W
    o_ik = o_iq + IDX_Q
    o_iw = o_ik + IDX_DIM
    w_g = w_in[:, o_gq:o_glr].astype(BF16)
    w_qk = w_in[:, o_dq:o_dv].astype(BF16)
    w_vi = w_in[:, o_dv:o_ik].astype(BF16)
    pad = jnp.zeros((d, LANES - GLA_GATE_RANK - IDX_HEADS - IDX_DIM), F32)
    w_s = jnp.concatenate([w_in[:, o_glr:o_dq], w_in[:, o_iw:o_iw + IDX_HEADS], pad,
                           w_in[:, o_ik:o_iw]], axis=1).astype(BF16)

    h = rmsnorm(x2d, norm_mix_g, BF16)
    zg = matmul(h, w_g, _epi_plain, [], BF16, "proj_gla")
    qk_gain = jnp.concatenate([jnp.tile(q_norm_g * (DSA_HD ** -0.5), DSA_HEADS),
                               jnp.tile(k_norm_g, DSA_HEADS)]).reshape(1, 2 * DSA_W).astype(F32)
    zqk = matmul(h, w_qk, _epi_headnorm, [(qk_gain, "row")], BF16, "proj_dsa_qk")
    zvi = matmul(h, w_vi, _epi_plain, [], BF16, "proj_dsa_vi")
    zs = matmul(h, w_s, _epi_plain, [], F32, "proj_small")
    gates = matmul(h, w_branch_gate.astype(BF16), _epi_sigmoid_bias,
                   [(b_branch_gate.reshape(1, -1).astype(F32), "row")], BF16, "branch_gates")

    q_hm = zg[:, o_gq:o_gk].reshape(b, s, GLA_HEADS, GLA_DK).transpose(0, 2, 1, 3)
    k_hm = zg[:, o_gk:o_gv].reshape(b, s, GLA_HEADS, GLA_DK).transpose(0, 2, 1, 3)
    wa = gla_w_alpha.reshape(GLA_GATE_RANK, GLA_HEADS, GLA_DK).transpose(1, 0, 2).astype(BF16)
    ba = gla_b_alpha.reshape(GLA_HEADS, 1, GLA_DK).astype(F32)
    o_a = gla(q_hm, k_hm, zg, zs, wa, ba, gla_norm_g.reshape(1, GLA_DV).astype(F32), b, s)

    iq_hm = zvi[:, DSA_W:].reshape(b, s, IDX_HEADS, IDX_DIM).transpose(0, 2, 1, 3)
    ik = zs[:, LANES - IDX_DIM:].astype(BF16).reshape(b, s, IDX_DIM)
    mask = dsa_index(iq_hm, zs, ik, b, s)
    nbd, nbp = _near_bias_tables(rel_bias, min(ATT_T, s))
    o_b = dsa_attention(zqk, zvi, mask, nbd, nbp, b, s)

    m = mix(o_a, o_b, w_out_gla.astype(BF16), w_out_dsa.astype(BF16), gates)
    x1 = matmul(m, w_out.astype(BF16), _epi_residual, [(x2d, "tile")], F32, "out_proj")

    w_r = jnp.concatenate([w_group_router, w_expert_router,
                           jnp.zeros((d, LANES - N_GROUPS - N_EXPERTS), F32)], axis=1).astype(BF16)
    b_r = jnp.concatenate([b_group_router, b_expert_router,
                           jnp.zeros((LANES - N_GROUPS - N_EXPERTS,), F32)]).reshape(1, LANES)
    hn, ri, rw = router(x1, norm_ffn_g, w_r, b_r)
    a = t * TOP_K_INNER
    eid = ri[:, :TOP_K_INNER].reshape(a)
    rank, counts = moe_rank(eid)
    padded = ((counts + MOE_BLK - 1) // MOE_BLK) * MOE_BLK
    pend = jnp.cumsum(padded)
    pstart = pend - padded
    dest = (pstart[eid] + rank).astype(I32)
    prow = a + N_EXPERTS * MOE_BLK
    nblk = prow // MOE_BLK
    blk_expert = jnp.clip(jnp.searchsorted(pend, jnp.arange(nblk, dtype=I32) * MOE_BLK, side="right"),
                          0, N_EXPERTS - 1).astype(I32)
    nvalid = (pend[-1:] // MOE_BLK).astype(I32)
    xs = moe_dispatch(dest, hn, jnp.zeros((prow, d), F32))
    y = moe_experts(blk_expert, nvalid, xs, w_exp_gate, w_exp_up, w_exp_down)
    x2 = moe_combine(dest, x1, rw, y)

    hp = rmsnorm(x2, norm_ple_g, BF16)
    out = matmul(hp, w_ple_gate.astype(BF16), _epi_ple,
                 [(b_ple_gate.reshape(1, -1).astype(F32), "row"), (x2, "tile"),
                  (p.reshape(t, -1).astype(BF16), "mk"), (w_ple_proj.astype(BF16), "kn")],
                 F32, "ple")
    return out.reshape(b, s, d)


def kernel(x, p, norm_mix_g, w_in, gla_w_alpha, gla_b_alpha, gla_norm_g, w_out_gla, q_norm_g, k_norm_g, rel_bias, w_out_dsa, w_branch_gate, b_branch_gate, w_out, norm_ffn_g, w_group_router, b_group_router, w_expert_router, b_expert_router, w_exp_gate, w_exp_up, w_exp_down, norm_ple_g, w_ple_gate, b_ple_gate, w_ple_proj):
    depth = w_in.shape[0]
    for i in range(depth):
        x = _layer(x, p[i], norm_mix_g[i], w_in[i], gla_w_alpha[i], gla_b_alpha[i], gla_norm_g[i], w_out_gla[i],
                   q_norm_g[i], k_norm_g[i], rel_bias, w_out_dsa[i], w_branch_gate[i], b_branch_gate[i], w_out[i],
                   norm_ffn_g[i], w_group_router[i], b_group_router[i], w_expert_router[i], b_expert_router[i],
                   w_exp_gate[i], w_exp_up[i], w_exp_down[i], norm_ple_g[i], w_ple_gate[i], b_ple_gate[i],
                   w_ple_proj[i])
    return x
```

```python
import functools
import math

import jax
import jax.numpy as jnp
from jax import lax
from jax.experimental import pallas as pl
from jax.experimental.pallas import tpu as pltpu

F32 = jnp.float32
BF16 = jnp.bfloat16
I32 = jnp.int32

EPS = 1e-6
CHUNK = 64
GLA_HEADS = 8
GLA_DK = 64
GLA_DV = 128
GLA_GATE_RANK = 16
GLA_GATE_TEMP = 16.0
GLA_QK = GLA_HEADS * GLA_DK
GLA_V = GLA_HEADS * GLA_DV
DSA_HEADS = 8
DSA_HD = 128
DSA_W = DSA_HEADS * DSA_HD
IDX_HEADS = 16
IDX_DIM = 64
IDX_Q = IDX_HEADS * IDX_DIM
INDEX_TOPK = 256
REL_BUCKETS = 32
REL_MAX_DIST = 128
N_GROUPS = 8
EXPERTS_PER_GROUP = 8
N_EXPERTS = N_GROUPS * EXPERTS_PER_GROUP
EXPERT_FF = 512
TOP_K_INNER = 2

LANES = 128
VMEM_LIMIT = 56 * 1024 * 1024
NEG_BIG = -1e30
INT_MIN = -(2 ** 31)
KEY_NEG_INF = -2139095041

GLA_CT = 512
IDX_QG = 256
ATT_T = 512
MOE_BLK = 256
RANK_RB = 1024
ROW_T = 256


def _cp(sem, **kw):
    return pltpu.CompilerParams(dimension_semantics=sem, vmem_limit_bytes=VMEM_LIMIT, **kw)


def _dot(a, b):
    return jnp.dot(a, b, preferred_element_type=F32)


def _dot_nt(a, b):
    return lax.dot_general(a, b, (((1,), (1,)), ((), ())), preferred_element_type=F32)


def _dot_tn(a, b):
    return lax.dot_general(a, b, (((0,), (0,)), ((), ())), preferred_element_type=F32)


def _sigmoid(x):
    return 1.0 / (1.0 + jnp.exp(-x))


def _rmsnorm_kernel(x_ref, g_ref, o_ref):
    x = x_ref[...]
    ms = jnp.mean(x * x, axis=-1, keepdims=True)
    o_ref[...] = (x * lax.rsqrt(ms + EPS) * g_ref[...]).astype(o_ref.dtype)


def rmsnorm(x, g, out_dtype, tm=512):
    m, d = x.shape
    tm = min(tm, m)
    return pl.pallas_call(
        _rmsnorm_kernel,
        grid=(m // tm,),
        in_specs=[pl.BlockSpec((tm, d), lambda i: (i, 0)), pl.BlockSpec((1, d), lambda i: (0, 0))],
        out_specs=pl.BlockSpec((tm, d), lambda i: (i, 0)),
        out_shape=jax.ShapeDtypeStruct((m, d), out_dtype),
        compiler_params=_cp(("parallel",)),
        name="rmsnorm",
    )(x, g.reshape(1, d).astype(F32))


def _mm_kernel(*refs, n_extra, epilogue):
    a_ref, w_ref = refs[0], refs[1]
    extras = refs[2:2 + n_extra]
    o_ref = refs[2 + n_extra]
    acc = _dot(a_ref[...], w_ref[...])
    epilogue(acc, o_ref, *extras)


def matmul(a, w, epilogue, extras, out_dtype, name, tm=1024, tn=512):
    m, k = a.shape
    n = w.shape[1]
    tm, tn = min(tm, m), min(tn, n)
    specs = [pl.BlockSpec((tm, k), lambda i, j: (i, 0)), pl.BlockSpec((k, tn), lambda i, j: (0, j))]
    args = [a, w]
    for arr, kind in extras:
        if kind == "tile":
            specs.append(pl.BlockSpec((tm, tn), lambda i, j: (i, j)))
        elif kind == "row":
            specs.append(pl.BlockSpec((1, tn), lambda i, j: (0, j)))
        elif kind == "mk":
            specs.append(pl.BlockSpec((tm, arr.shape[1]), lambda i, j: (i, 0)))
        elif kind == "kn":
            specs.append(pl.BlockSpec((arr.shape[0], tn), lambda i, j: (0, j)))
        else:
            raise ValueError(kind)
        args.append(arr)
    return pl.pallas_call(
        functools.partial(_mm_kernel, n_extra=len(extras), epilogue=epilogue),
        grid=(m // tm, n // tn),
        in_specs=specs,
        out_specs=pl.BlockSpec((tm, tn), lambda i, j: (i, j)),
        out_shape=jax.ShapeDtypeStruct((m, n), out_dtype),
        compiler_params=_cp(("parallel", "arbitrary")),
        name=name,
    )(*args)


def _epi_plain(acc, o_ref):
    o_ref[...] = acc.astype(o_ref.dtype)


def _epi_headnorm(acc, o_ref, g_ref):
    for c in range(acc.shape[1] // DSA_HD):
        sl = slice(c * DSA_HD, (c + 1) * DSA_HD)
        blk = acc[:, sl]
        ms = jnp.mean(blk * blk, axis=-1, keepdims=True)
        o_ref[:, sl] = (blk * lax.rsqrt(ms + EPS) * g_ref[:, sl]).astype(o_ref.dtype)


def _epi_sigmoid_bias(acc, o_ref, b_ref):
    o_ref[...] = _sigmoid(acc + b_ref[...]).astype(o_ref.dtype)


def _epi_residual(acc, o_ref, x_ref):
    o_ref[...] = (x_ref[...] + acc).astype(o_ref.dtype)


def _epi_ple(acc, o_ref, b_ref, x_ref, p_ref, wp_ref):
    ple = _dot(p_ref[...], wp_ref[...])
    o_ref[...] = (x_ref[...] + _sigmoid(acc + b_ref[...]) * ple).astype(o_ref.dtype)


def _mix_kernel(oa_ref, ob_ref, wa_ref, wb_ref, ga_ref, gb_ref, o_ref):
    ya = _dot(oa_ref[...], wa_ref[...])
    yb = _dot(ob_ref[...], wb_ref[...])
    o_ref[...] = (ga_ref[...].astype(F32) * ya + gb_ref[...].astype(F32) * yb).astype(o_ref.dtype)


def mix(oa, ob, wa, wb, gates, tm=1024, tn=512):
    m, k = oa.shape
    n = wa.shape[1]
    tm, tn = min(tm, m), min(tn, n)
    nj = n // tn
    return pl.pallas_call(
        _mix_kernel,
        grid=(m // tm, nj),
        in_specs=[
            pl.BlockSpec((tm, k), lambda i, j: (i, 0)),
            pl.BlockSpec((tm, k), lambda i, j: (i, 0)),
            pl.BlockSpec((k, tn), lambda i, j: (0, j)),
            pl.BlockSpec((k, tn), lambda i, j: (0, j)),
            pl.BlockSpec((tm, tn), lambda i, j: (i, j)),
            pl.BlockSpec((tm, tn), lambda i, j: (i, j + nj)),
        ],
        out_specs=pl.BlockSpec((tm, tn), lambda i, j: (i, j)),
        out_shape=jax.ShapeDtypeStruct((m, n), BF16),
        compiler_params=_cp(("parallel", "arbitrary")),
        name="mix",
    )(oa, ob, wa, wb, gates, gates)


def _gla_kernel(q_ref, k_ref, v_ref, gr_ref, zs_ref, wa_ref, ba_ref, gn_ref, o_ref, st_ref, *, nchunk):
    @pl.when(pl.program_id(2) == 0)
    def _():
        st_ref[...] = jnp.zeros_like(st_ref)

    glr = zs_ref[:, 0:GLA_GATE_RANK].astype(BF16)
    logit = _dot(glr, wa_ref[...]) + ba_ref[...]
    g = (jnp.minimum(logit, 0.0) - jnp.log(1.0 + jnp.exp(-jnp.abs(logit)))) * (1.0 / GLA_GATE_TEMP)
    g_hi = g.astype(BF16)
    g_lo = (g - g_hi.astype(F32)).astype(BF16)
    row = lax.broadcasted_iota(I32, (CHUNK, CHUNK), 0)
    col = lax.broadcasted_iota(I32, (CHUNK, CHUNK), 1)
    causal = row >= col
    tril = jnp.where(causal, 1.0, 0.0).astype(BF16)
    gn = gn_ref[...]
    mid = CHUNK // 2 - 1
    for c in range(nchunk):
        sl = slice(c * CHUNK, (c + 1) * CHUNK)
        G = _dot(tril, g_hi[sl]) + _dot(tril, g_lo[sl])
        qc = q_ref[sl, :].astype(F32) * (GLA_DK ** -0.5)
        kc = k_ref[sl, :].astype(F32)
        vc = v_ref[sl, :]
        c0 = G[mid:mid + 1, :]
        gl = G[CHUNK - 1:CHUNK, :]
        qt = qc * jnp.exp(G - c0)
        kt = kc * jnp.exp(c0 - G)
        sc = jnp.where(causal, _dot_nt(qt.astype(BF16), kt.astype(BF16)), 0.0)
        o = _dot(sc.astype(BF16), vc)
        st = st_ref[...]
        qg = qt * jnp.exp(c0)
        o = o + _dot_nt(qg.astype(BF16), st.astype(BF16))
        kd = kt * jnp.exp(gl - c0)
        st_ref[...] = st * jnp.exp(gl) + _dot_tn(vc, kd.astype(BF16))
        ms = jnp.mean(o * o, axis=-1, keepdims=True)
        gr = gr_ref[sl, :].astype(F32)
        o_ref[sl, :] = (o * lax.rsqrt(ms + EPS) * gn * (gr * _sigmoid(gr))).astype(o_ref.dtype)


def gla(q_hm, k_hm, zg, zs, wa, ba, gn, b, s):
    ct = min(GLA_CT, s)
    ns = s // ct
    v_blk0 = (2 * GLA_QK) // GLA_DV
    gr_blk0 = (2 * GLA_QK + GLA_V) // GLA_DV
    return pl.pallas_call(
        functools.partial(_gla_kernel, nchunk=ct // CHUNK),
        grid=(b, GLA_HEADS, ns),
        in_specs=[
            pl.BlockSpec((None, None, ct, GLA_DK), lambda bi, h, si: (bi, h, si, 0)),
            pl.BlockSpec((None, None, ct, GLA_DK), lambda bi, h, si: (bi, h, si, 0)),
            pl.BlockSpec((ct, GLA_DV), lambda bi, h, si: (bi * ns + si, v_blk0 + h)),
            pl.BlockSpec((ct, GLA_DV), lambda bi, h, si: (bi * ns + si, gr_blk0 + h)),
            pl.BlockSpec((ct, LANES), lambda bi, h, si: (bi * ns + si, 0)),
            pl.BlockSpec((None, GLA_GATE_RANK, GLA_DK), lambda bi, h, si: (h, 0, 0)),
            pl.BlockSpec((None, 1, GLA_DK), lambda bi, h, si: (h, 0, 0)),
            pl.BlockSpec((1, GLA_DV), lambda bi, h, si: (0, 0)),
        ],
        out_specs=pl.BlockSpec((ct, GLA_DV), lambda bi, h, si: (bi * ns + si, h)),
        out_shape=jax.ShapeDtypeStruct((b * s, GLA_V), BF16),
        scratch_shapes=[pltpu.VMEM((GLA_DV, GLA_DK), F32)],
        compiler_params=_cp(("parallel", "parallel", "arbitrary")),
        name="gla",
    )(q_hm, k_hm, zg, zg, zs, wa, ba, gn)


def _order_key(x):
    bits = lax.bitcast_convert_type(x, I32)
    return bits ^ ((bits >> 31) & 0x7FFFFFFF)


def _idx_kernel(iq_ref, w_ref, ik_ref, mask_ref, key_ref, *, qg, s):
    g = pl.program_id(1)
    nt = g + 1
    w = w_ref[...] * ((IDX_HEADS ** -0.5) * (IDX_DIM ** -0.5))
    kchunk = lax.broadcasted_iota(I32, (qg, qg), 0) // CHUNK
    qchunk = lax.broadcasted_iota(I32, (qg, qg), 1) // CHUNK
    adm_diag = kchunk <= qchunk

    def score_tile(t, segmax):
        off = pl.multiple_of(t * qg, qg)
        ikt = ik_ref[pl.ds(off, qg), :]
        acc = jnp.zeros((qg, qg), F32)
        for j in range(IDX_HEADS):
            acc = acc + jnp.maximum(_dot_nt(ikt, iq_ref[j]), 0.0) * w[j:j + 1, :]
        acc = jnp.where(jnp.logical_or(t < g, adm_diag), acc, -jnp.inf)
        key_ref[pl.ds(off, qg), :] = _order_key(acc)
        return jnp.maximum(segmax, acc)

    segmax = lax.fori_loop(0, nt, score_tile, jnp.full((qg, qg), -jnp.inf, F32))

    topk = min(INDEX_TOPK, s // 4)
    kf = float(topk)
    lb = _order_key(jnp.min(segmax, axis=0, keepdims=True))
    mx = _order_key(jnp.max(segmax, axis=0, keepdims=True))
    qcol = lax.broadcasted_iota(I32, (1, qg), 1)
    n_adm = g * qg + (qcol // CHUNK + 1) * CHUNK
    small = n_adm <= topk
    diff = jnp.where(small, 0, mx ^ lb)
    nbq = jnp.where(diff < 0, 32, 0)
    for b in range(31):
        nbq = nbq + jnp.where(diff >= (1 << b), 1, 0)
    nb = jnp.max(nbq.astype(F32)).astype(I32)
    nbc = jnp.minimum(nb, 31)
    t0 = jnp.where(nb >= 32, INT_MIN, jnp.left_shift(jnp.right_shift(lb, nbc), nbc))

    def count_ge(cand):
        def body(t, acc):
            off = pl.multiple_of(t * qg, qg)
            kk = key_ref[pl.ds(off, qg), :]
            m = jnp.where(kk >= cand, 1.0, 0.0)
            return acc + jnp.sum(m.reshape(qg // 8, 8, qg), axis=0)
        acc = lax.fori_loop(0, nt, body, jnp.zeros((8, qg), F32))
        return jnp.sum(acc, axis=0, keepdims=True)

    def cond(c):
        i, _, done = c
        return jnp.logical_and(i < nb, jnp.min(done) < 0.5)

    def body(c):
        i, thr, done = c
        cand = thr + jnp.left_shift(jnp.int32(1), nb - 1 - i)
        cnt = count_ge(cand)
        take = jnp.logical_and(cnt >= kf, done < 0.5)
        thr = jnp.where(take, cand, thr)
        done = jnp.where(jnp.logical_and(take, cnt == kf), 1.0, done)
        return i + 1, thr, done

    done0 = jnp.where(small, 1.0, 0.0)
    _, thr, _ = lax.while_loop(cond, body, (jnp.int32(0), t0, done0))
    thr = jnp.where(small, KEY_NEG_INF + 1, jnp.maximum(thr, KEY_NEG_INF + 1))

    def write_tile(t, carry):
        off = pl.multiple_of(t * qg, qg)
        kk = key_ref[pl.ds(off, qg), :]
        mask_ref[pl.ds(off, qg), :] = jnp.where(kk >= thr, 0.0, NEG_BIG).astype(BF16)
        return carry

    lax.fori_loop(0, nt, write_tile, 0)

    def fill_tile(t, carry):
        off = pl.multiple_of(t * qg, qg)
        mask_ref[pl.ds(off, qg), :] = jnp.full((qg, qg), NEG_BIG, BF16)
        return carry

    lax.fori_loop(nt, s // qg, fill_tile, 0)


def dsa_index(iq_hm, w_hm, ik, b, s):
    qg = min(IDX_QG, s)
    ng = s // qg
    return pl.pallas_call(
        functools.partial(_idx_kernel, qg=qg, s=s),
        grid=(b, ng),
        in_specs=[
            pl.BlockSpec((None, IDX_HEADS, qg, IDX_DIM), lambda bi, gi: (bi, 0, gi, 0)),
            pl.BlockSpec((None, IDX_HEADS, qg), lambda bi, gi: (bi, 0, gi)),
            pl.BlockSpec((None, s, IDX_DIM), lambda bi, gi: (bi, 0, 0)),
        ],
        out_specs=pl.BlockSpec((None, s, qg), lambda bi, gi: (bi, 0, gi)),
        out_shape=jax.ShapeDtypeStruct((b, s, s), BF16),
        scratch_shapes=[pltpu.VMEM((s, qg), I32)],
        compiler_params=_cp(("parallel", "arbitrary")),
        name="dsa_index",
    )(iq_hm, w_hm, ik)


def _attn_kernel(qi_ref, ki_ref, q_ref, k_ref, vt_ref, mask_ref, nbd_ref, nbp_ref, o_ref, m_ref, l_ref, acc_ref):
    p = pl.program_id(1)
    qi = qi_ref[p]
    ki = ki_ref[p]

    @pl.when(ki == 0)
    def _():
        m_ref[...] = jnp.full_like(m_ref, NEG_BIG)
        l_ref[...] = jnp.zeros_like(l_ref)
        acc_ref[...] = jnp.zeros_like(acc_ref)

    def heads(bias_of_head):
        for h in range(DSA_HEADS):
            sl = slice(h * DSA_HD, (h + 1) * DSA_HD)
            sc = _dot_nt(k_ref[:, sl], q_ref[:, sl]) + bias_of_head(h)
            m_old = m_ref[h]
            m_new = jnp.maximum(m_old, jnp.max(sc, axis=0, keepdims=True))
            alpha = jnp.exp(m_old - m_new)
            pr = jnp.exp(sc - m_new)
            l_ref[h] = alpha * l_ref[h] + jnp.sum(pr, axis=0, keepdims=True)
            acc_ref[sl, :] = alpha * acc_ref[sl, :] + _dot(vt_ref[sl, :], pr.astype(BF16))
            m_ref[h] = m_new

    @pl.when(ki < qi - 1)
    def _():
        mb = mask_ref[...].astype(F32)
        heads(lambda h: mb)

    @pl.when(ki == qi - 1)
    def _():
        mb = mask_ref[...].astype(F32)
        heads(lambda h: mb + nbp_ref[h].astype(F32))

    @pl.when(ki == qi)
    def _():
        mb = mask_ref[...].astype(F32)
        heads(lambda h: mb + nbd_ref[h].astype(F32))
        for h in range(DSA_HEADS):
            sl = slice(h * DSA_HD, (h + 1) * DSA_HD)
            o_ref[sl, :] = (acc_ref[sl, :] / l_ref[h]).astype(o_ref.dtype)


def dsa_attention(zqk, vt, mask, nbd, nbp, b, s):
    t = min(ATT_T, s)
    ng = s // t
    pairs = [(qi, ki) for qi in range(ng) for ki in range(qi + 1)]
    qi_tab = jnp.asarray([p[0] for p in pairs], I32)
    ki_tab = jnp.asarray([p[1] for p in pairs], I32)
    grid_spec = pltpu.PrefetchScalarGridSpec(
        num_scalar_prefetch=2,
        grid=(b, len(pairs)),
        in_specs=[
            pl.BlockSpec((t, DSA_W), lambda bi, p, qt, kt: (bi * ng + qt[p], 0)),
            pl.BlockSpec((t, DSA_W), lambda bi, p, qt, kt: (bi * ng + kt[p], 1)),
            pl.BlockSpec((None, DSA_W, t), lambda bi, p, qt, kt: (bi, 0, kt[p])),
            pl.BlockSpec((None, t, t), lambda bi, p, qt, kt: (bi, kt[p], qt[p])),
            pl.BlockSpec((DSA_HEADS, t, t), lambda bi, p, qt, kt: (0, 0, 0)),
            pl.BlockSpec((DSA_HEADS, t, t), lambda bi, p, qt, kt: (0, 0, 0)),
        ],
        out_specs=pl.BlockSpec((None, DSA_W, t), lambda bi, p, qt, kt: (bi, 0, qt[p])),
        scratch_shapes=[
            pltpu.VMEM((DSA_HEADS, 1, t), F32),
            pltpu.VMEM((DSA_HEADS, 1, t), F32),
            pltpu.VMEM((DSA_W, t), F32),
        ],
    )
    return pl.pallas_call(
        _attn_kernel,
        grid_spec=grid_spec,
        out_shape=jax.ShapeDtypeStruct((b, DSA_W, s), BF16),
        compiler_params=_cp(("parallel", "arbitrary")),
        name="dsa_attention",
    )(qi_tab, ki_tab, zqk, zqk, vt, mask, nbd, nbp)


def _t5_bucket(rel):
    half = REL_BUCKETS // 2
    exact = half // 2
    sign = jnp.where(rel > 0, half, 0)
    n = jnp.abs(rel)
    nf = jnp.maximum(n, 1).astype(F32)
    large = exact + (jnp.log(nf / exact) / math.log(REL_MAX_DIST / exact) * (half - exact)).astype(I32)
    large = jnp.minimum(large, half - 1)
    return sign + jnp.where(n < exact, n, large)


def _near_bias_tables(rel_bias, t):
    kpos = jnp.arange(t, dtype=I32)[:, None]
    qpos = jnp.arange(t, dtype=I32)[None, :]
    shifted = rel_bias - rel_bias[REL_BUCKETS // 2 - 1]

    def table(rel):
        onehot = (_t5_bucket(rel)[:, :, None] == jnp.arange(REL_BUCKETS, dtype=I32)).astype(F32)
        return jnp.einsum("kqb,bh->hkq", onehot, shifted, precision=lax.Precision.HIGHEST).astype(BF16)

    return table(kpos - qpos), table(kpos - t - qpos)


def _router_kernel(x_ref, g_ref, w_ref, b_ref, hn_ref, ri_ref, rw_ref):
    x = x_ref[...]
    ms = jnp.mean(x * x, axis=-1, keepdims=True)
    hn = x * lax.rsqrt(ms + EPS) * g_ref[...]
    hn_ref[...] = hn
    logits = _dot(hn.astype(BF16), w_ref[...]) + b_ref[...]
    lane = lax.broadcasted_iota(I32, logits.shape, 1)
    lanef = lane.astype(F32)
    big = 1e9
    is_grp = lane < N_GROUPS
    gl = jnp.where(is_grp, logits, -jnp.inf)
    gmax = jnp.max(gl, axis=-1, keepdims=True)
    grp = jnp.min(jnp.where(gl == gmax, lanef, big), axis=-1, keepdims=True)
    gsum = jnp.sum(jnp.where(is_grp, jnp.exp(logits - gmax), 0.0), axis=-1, keepdims=True)
    grp_w = 1.0 / gsum
    elo = N_GROUPS + grp * EXPERTS_PER_GROUP
    in_grp = jnp.logical_and(lanef >= elo, lanef < elo + EXPERTS_PER_GROUP)
    el = jnp.where(in_grp, logits, -jnp.inf)
    v1 = jnp.max(el, axis=-1, keepdims=True)
    i1 = jnp.min(jnp.where(el == v1, lanef, big), axis=-1, keepdims=True)
    el2 = jnp.where(lanef == i1, -jnp.inf, el)
    v2 = jnp.max(el2, axis=-1, keepdims=True)
    i2 = jnp.min(jnp.where(el2 == v2, lanef, big), axis=-1, keepdims=True)
    e2 = jnp.exp(v2 - v1)
    w1 = grp_w * (1.0 / (1.0 + e2))
    w2 = grp_w * (e2 / (1.0 + e2))
    ri_ref[...] = jnp.where(lane == 0, i1 - N_GROUPS, jnp.where(lane == 1, i2 - N_GROUPS, 0.0)).astype(I32)
    rw_ref[...] = jnp.where(lane == 0, w1, jnp.where(lane == 1, w2, 0.0))


def router(x1, g, w_r, b_r, tm=256):
    m, d = x1.shape
    tm = min(tm, m)
    return pl.pallas_call(
        _router_kernel,
        grid=(m // tm,),
        in_specs=[
            pl.BlockSpec((tm, d), lambda i: (i, 0)),
            pl.BlockSpec((1, d), lambda i: (0, 0)),
            pl.BlockSpec((d, LANES), lambda i: (0, 0)),
            pl.BlockSpec((1, LANES), lambda i: (0, 0)),
        ],
        out_specs=[
            pl.BlockSpec((tm, d), lambda i: (i, 0)),
            pl.BlockSpec((tm, LANES), lambda i: (i, 0)),
            pl.BlockSpec((tm, LANES), lambda i: (i, 0)),
        ],
        out_shape=[
            jax.ShapeDtypeStruct((m, d), F32),
            jax.ShapeDtypeStruct((m, LANES), I32),
            jax.ShapeDtypeStruct((m, LANES), F32),
        ],
        compiler_params=_cp(("parallel",)),
        name="router",
    )(x1, g.reshape(1, d).astype(F32), w_r, b_r)


def _rank_kernel(e_ref, rank_ref, cnt_ref, run_ref, *, rb):
    @pl.when(pl.program_id(0) == 0)
    def _():
        run_ref[...] = jnp.zeros_like(run_ref)

    e = e_ref[...]
    sub = lax.broadcasted_iota(I32, (N_EXPERTS, rb), 0)
    hit = sub == e
    oh = jnp.where(hit, 1.0, 0.0)
    a = lax.broadcasted_iota(I32, (rb, rb), 0)
    bcol = lax.broadcasted_iota(I32, (rb, rb), 1)
    upper = jnp.where(a < bcol, 1.0, 0.0).astype(BF16)
    prefix = _dot(oh.astype(BF16), upper)
    run = run_ref[...]
    rank = jnp.sum(jnp.where(hit, prefix + run[:, 0:1], 0.0), axis=0, keepdims=True)
    rank_ref[...] = rank.astype(I32)
    new = run + jnp.sum(oh, axis=1, keepdims=True)
    run_ref[...] = new
    cnt_ref[...] = new


def moe_rank(eid_flat):
    a = eid_flat.shape[0]
    rb = min(RANK_RB, a)
    nb = a // rb
    rank, cnt = pl.pallas_call(
        functools.partial(_rank_kernel, rb=rb),
        grid=(nb,),
        in_specs=[pl.BlockSpec((None, 1, rb), lambda i: (i, 0, 0))],
        out_specs=[
            pl.BlockSpec((None, 1, rb), lambda i: (i, 0, 0)),
            pl.BlockSpec((N_EXPERTS, LANES), lambda i: (0, 0)),
        ],
        out_shape=[
            jax.ShapeDtypeStruct((nb, 1, rb), I32),
            jax.ShapeDtypeStruct((N_EXPERTS, LANES), F32),
        ],
        scratch_shapes=[pltpu.VMEM((N_EXPERTS, LANES), F32)],
        compiler_params=_cp(("arbitrary",)),
        name="moe_rank",
    )(eid_flat.reshape(nb, 1, rb))
    return rank.reshape(a), cnt[:, 0].astype(I32)


def _row_copy(src, dst, sem):
    return pltpu.make_async_copy(src, dst, sem)


def _dispatch_kernel(dest_ref, hn_ref, xs_in_ref, xs_ref, sem, *, tm):
    del xs_in_ref
    base = pl.program_id(0) * (tm * TOP_K_INNER)

    def body(r, carry):
        for slot in range(TOP_K_INNER):
            d = dest_ref[base + r * TOP_K_INNER + slot]
            _row_copy(hn_ref.at[pl.ds(r, 1), :], xs_ref.at[pl.ds(d, 1), :], sem).start()
        return carry

    lax.fori_loop(0, tm, body, 0)
    for _ in range(TOP_K_INNER):
        _row_copy(hn_ref, xs_ref.at[pl.ds(0, tm), :], sem).wait()


def moe_dispatch(dest, hn, xs0):
    m, d = hn.shape
    tm = min(ROW_T, m)
    grid_spec = pltpu.PrefetchScalarGridSpec(
        num_scalar_prefetch=1,
        grid=(m // tm,),
        in_specs=[
            pl.BlockSpec((tm, d), lambda i, dst: (i, 0)),
            pl.BlockSpec(memory_space=pl.ANY),
        ],
        out_specs=pl.BlockSpec(memory_space=pl.ANY),
        scratch_shapes=[pltpu.SemaphoreType.DMA(())],
    )
    return pl.pallas_call(
        functools.partial(_dispatch_kernel, tm=tm),
        grid_spec=grid_spec,
        out_shape=jax.ShapeDtypeStruct(xs0.shape, xs0.dtype),
        input_output_aliases={2: 0},
        compiler_params=_cp(("arbitrary",), has_side_effects=True),
        name="moe_dispatch",
    )(dest, hn, xs0)


def _expert_kernel(be_ref, nv_ref, xs_ref, wg_ref, wu_ref, wd_ref, y_ref, wg_bf, wu_bf, wd_bf):
    i = pl.program_id(0)
    nv = nv_ref[0]
    prev = be_ref[jnp.maximum(i - 1, 0)]
    fresh = jnp.logical_or(i == 0, be_ref[i] != prev)

    @pl.when(jnp.logical_and(i < nv, fresh))
    def _():
        wg_bf[...] = wg_ref[...].astype(BF16)
        wu_bf[...] = wu_ref[...].astype(BF16)
        wd_bf[...] = wd_ref[...].astype(BF16)

    @pl.when(i < nv)
    def _():
        xb = xs_ref[...].astype(BF16)
        hg = _dot(xb, wg_bf[...])
        hu = _dot(xb, wu_bf[...])
        act = (hg * _sigmoid(hg) * hu).astype(BF16)
        y_ref[...] = _dot(act, wd_bf[...])

    @pl.when(i >= nv)
    def _():
        y_ref[...] = jnp.zeros_like(y_ref)


def moe_experts(blk_expert, nvalid, xs, w_gate, w_up, w_down):
    p, d = xs.shape
    nblk = p // MOE_BLK
    ff = w_gate.shape[2]

    def blk(i, be, nv):
        return jnp.minimum(i, nv[0] - 1)

    grid_spec = pltpu.PrefetchScalarGridSpec(
        num_scalar_prefetch=2,
        grid=(nblk,),
        in_specs=[
            pl.BlockSpec((MOE_BLK, d), lambda i, be, nv: (blk(i, be, nv), 0)),
            pl.BlockSpec((None, d, ff), lambda i, be, nv: (be[blk(i, be, nv)], 0, 0)),
            pl.BlockSpec((None, d, ff), lambda i, be, nv: (be[blk(i, be, nv)], 0, 0)),
            pl.BlockSpec((None, ff, d), lambda i, be, nv: (be[blk(i, be, nv)], 0, 0)),
        ],
        out_specs=pl.BlockSpec((MOE_BLK, d), lambda i, be, nv: (i, 0)),
        scratch_shapes=[pltpu.VMEM((d, ff), BF16), pltpu.VMEM((d, ff), BF16), pltpu.VMEM((ff, d), BF16)],
    )
    return pl.pallas_call(
        _expert_kernel,
        grid_spec=grid_spec,
        out_shape=jax.ShapeDtypeStruct((p, d), F32),
        compiler_params=_cp(("arbitrary",)),
        name="moe_experts",
    )(blk_expert, nvalid, xs, w_gate, w_up, w_down)


def _combine_kernel(dest_ref, x_ref, rw_ref, y_ref, o_ref, ybuf, sem, *, tm):
    base = pl.program_id(0) * (tm * TOP_K_INNER)

    def body(r, carry):
        for slot in range(TOP_K_INNER):
            d = dest_ref[base + r * TOP_K_INNER + slot]
            _row_copy(y_ref.at[pl.ds(d, 1), :], ybuf.at[slot, pl.ds(r, 1), :], sem).start()
        return carry

    lax.fori_loop(0, tm, body, 0)
    for slot in range(TOP_K_INNER):
        _row_copy(y_ref.at[pl.ds(0, tm), :], ybuf.at[slot], sem).wait()
    rw = rw_ref[...]
    o_ref[...] = x_ref[...] + rw[:, 0:1] * ybuf[0] + rw[:, 1:2] * ybuf[1]


def moe_combine(dest, x1, rw, y):
    m, d = x1.shape
    tm = min(ROW_T, m)
    grid_spec = pltpu.PrefetchScalarGridSpec(
        num_scalar_prefetch=1,
        grid=(m // tm,),
        in_specs=[
            pl.BlockSpec((tm, d), lambda i, dst: (i, 0)),
            pl.BlockSpec((tm, LANES), lambda i, dst: (i, 0)),
            pl.BlockSpec(memory_space=pl.ANY),
        ],
        out_specs=pl.BlockSpec((tm, d), lambda i, dst: (i, 0)),
        scratch_shapes=[pltpu.VMEM((TOP_K_INNER, tm, d), F32), pltpu.SemaphoreType.DMA(())],
    )
    return pl.pallas_call(
        functools.partial(_combine_kernel, tm=tm),
        grid_spec=grid_spec,
        out_shape=jax.ShapeDtypeStruct((m, d), F32),
        compiler_params=_cp(("arbitrary",)),
        name="moe_combine",
    )(dest, x1, rw, y)


def _layer(x, p, norm_mix_g, w_in, gla_w_alpha, gla_b_alpha, gla_norm_g, w_out_gla, q_norm_g, k_norm_g, rel_bias,
           w_out_dsa, w_branch_gate, b_branch_gate, w_out, norm_ffn_g, w_group_router, b_group_router,
           w_expert_router, b_expert_router, w_exp_gate, w_exp_up, w_exp_down, norm_ple_g, w_ple_gate,
           b_ple_gate, w_ple_proj):
    b, s, d = x.shape
    t = b * s
    x2d = x.reshape(t, d)

    o_gq, o_gk, o_gv, o_gr = 0, GLA_QK, 2 * GLA_QK, 2 * GLA_QK + GLA_V
    o_glr = o_gr + GLA_V
    o_dq = o_glr + GLA_GATE_RANK
    o_dk, o_dv = o_dq + DSA_W, o_dq + 2 * DSA_W
    o_iq = o_dv + DSA_W
    o_ik = o_iq + IDX_Q
    o_iw = o_ik + IDX_DIM
    w_g = w_in[:, o_gq:o_glr].astype(BF16)
    w_qk = w_in[:, o_dq:o_dv].astype(BF16)
    w_vi = w_in[:, o_dv:o_ik].astype(BF16)
    pad = jnp.zeros((d, LANES - GLA_GATE_RANK - IDX_HEADS - IDX_DIM), F32)
    w_s = jnp.concatenate([w_in[:, o_glr:o_dq], w_in[:, o_iw:o_iw + IDX_HEADS], pad,
                           w_in[:, o_ik:o_iw]], axis=1).astype(BF16)

    h = rmsnorm(x2d, norm_mix_g, BF16)
    zg = matmul(h, w_g, _epi_plain, [], BF16, "proj_gla")
    qk_gain = jnp.concatenate([jnp.tile(q_norm_g * (DSA_HD ** -0.5), DSA_HEADS),
                               jnp.tile(k_norm_g, DSA_HEADS)]).reshape(1, 2 * DSA_W).astype(F32)
    zqk = matmul(h, w_qk, _epi_headnorm, [(qk_gain, "row")], BF16, "proj_dsa_qk")
    zvi = matmul(h, w_vi, _epi_plain, [], BF16, "proj_dsa_vi")
    zs = matmul(h, w_s, _epi_plain, [], F32, "proj_small")
    gates = matmul(h, w_branch_gate.astype(BF16), _epi_sigmoid_bias,
                   [(b_branch_gate.reshape(1, -1).astype(F32), "row")], BF16, "branch_gates")

    q_hm = zg[:, o_gq:o_gk].reshape(b, s, GLA_HEADS, GLA_DK).transpose(0, 2, 1, 3)
    k_hm = zg[:, o_gk:o_gv].reshape(b, s, GLA_HEADS, GLA_DK).transpose(0, 2, 1, 3)
    wa = gla_w_alpha.reshape(GLA_GATE_RANK, GLA_HEADS, GLA_DK).transpose(1, 0, 2).astype(BF16)
    ba = gla_b_alpha.reshape(GLA_HEADS, 1, GLA_DK).astype(F32)
    o_a = gla(q_hm, k_hm, zg, zs, wa, ba, gla_norm_g.reshape(1, GLA_DV).astype(F32), b, s)

    iq_hm = zvi[:, DSA_W:].reshape(b, s, IDX_HEADS, IDX_DIM).transpose(0, 2, 1, 3)
    ik = zs[:, LANES - IDX_DIM:].astype(BF16).reshape(b, s, IDX_DIM)
    w_hm = zs[:, GLA_GATE_RANK:GLA_GATE_RANK + IDX_HEADS].reshape(b, s, IDX_HEADS).transpose(0, 2, 1)
    mask = dsa_index(iq_hm, w_hm, ik, b, s)
    nbd, nbp = _near_bias_tables(rel_bias, min(ATT_T, s))
    vt = zvi[:, :DSA_W].reshape(b, s, DSA_W).transpose(0, 2, 1)
    o_b = dsa_attention(zqk, vt, mask, nbd, nbp, b, s).transpose(0, 2, 1).reshape(t, DSA_W)

    m = mix(o_a, o_b, w_out_gla.astype(BF16), w_out_dsa.astype(BF16), gates)
    x1 = matmul(m, w_out.astype(BF16), _epi_residual, [(x2d, "tile")], F32, "out_proj")

    w_r = jnp.concatenate([w_group_router, w_expert_router,
                           jnp.zeros((d, LANES - N_GROUPS - N_EXPERTS), F32)], axis=1).astype(BF16)
    b_r = jnp.concatenate([b_group_router, b_expert_router,
                           jnp.zeros((LANES - N_GROUPS - N_EXPERTS,), F32)]).reshape(1, LANES)
    hn, ri, rw = router(x1, norm_ffn_g, w_r, b_r)
    a = t * TOP_K_INNER
    eid = ri[:, :TOP_K_INNER].reshape(a)
    rank, counts = moe_rank(eid)
    padded = ((counts + MOE_BLK - 1) // MOE_BLK) * MOE_BLK
    pend = jnp.cumsum(padded)
    pstart = pend - padded
    dest = (pstart[eid] + rank).astype(I32)
    prow = a + N_EXPERTS * MOE_BLK
    nblk = prow // MOE_BLK
    blk_row0 = jnp.arange(nblk, dtype=I32)[:, None] * MOE_BLK
    blk_expert = jnp.minimum(jnp.sum((pend[None, :] <= blk_row0).astype(I32), axis=1), N_EXPERTS - 1)
    nvalid = (pend[-1:] // MOE_BLK).astype(I32)
    xs = moe_dispatch(dest, hn, jnp.zeros((prow, d), F32))
    y = moe_experts(blk_expert, nvalid, xs, w_exp_gate, w_exp_up, w_exp_down)
    x2 = moe_combine(dest, x1, rw, y)

    hp = rmsnorm(x2, norm_ple_g, BF16)
    out = matmul(hp, w_ple_gate.astype(BF16), _epi_ple,
                 [(b_ple_gate.reshape(1, -1).astype(F32), "row"), (x2, "tile"),
                  (p.reshape(t, -1).astype(BF16), "mk"), (w_ple_proj.astype(BF16), "kn")],
                 F32, "ple")
    return out.reshape(b, s, d)


def kernel(x, p, norm_mix_g, w_in, gla_w_alpha, gla_b_alpha, gla_norm_g, w_out_gla, q_norm_g, k_norm_g, rel_bias, w_out_dsa, w_branch_gate, b_branch_gate, w_out, norm_ffn_g, w_group_router, b_group_router, w_expert_router, b_expert_router, w_exp_gate, w_exp_up, w_exp_down, norm_ple_g, w_ple_gate, b_ple_gate, w_ple_proj):
    depth = w_in.shape[0]
    for i in range(depth):
        x = _layer(x, p[i], norm_mix_g[i], w_in[i], gla_w_alpha[i], gla_b_alpha[i], gla_norm_g[i], w_out_gla[i],
                   q_norm_g[i], k_norm_g[i], rel_bias, w_out_dsa[i], w_branch_gate[i], b_branch_gate[i], w_out[i],
                   norm_ffn_g[i], w_group_router[i], b_group_router[i], w_expert_router[i], b_expert_router[i],
                   w_exp_gate[i], w_exp_up[i], w_exp_down[i], norm_ple_g[i], w_ple_gate[i], b_ple_gate[i],
                   w_ple_proj[i])
    return x
```

```python
import functools
import math

import jax
import jax.numpy as jnp
from jax import lax
from jax.experimental import pallas as pl
from jax.experimental.pallas import tpu as pltpu

F32 = jnp.float32
BF16 = jnp.bfloat16
I32 = jnp.int32

EPS = 1e-6
CHUNK = 64
GLA_HEADS = 8
GLA_DK = 64
GLA_DV = 128
GLA_GATE_RANK = 16
GLA_GATE_TEMP = 16.0
GLA_QK = GLA_HEADS * GLA_DK
GLA_V = GLA_HEADS * GLA_DV
DSA_HEADS = 8
DSA_HD = 128
DSA_W = DSA_HEADS * DSA_HD
IDX_HEADS = 16
IDX_DIM = 64
IDX_Q = IDX_HEADS * IDX_DIM
INDEX_TOPK = 256
REL_BUCKETS = 32
REL_MAX_DIST = 128
N_GROUPS = 8
EXPERTS_PER_GROUP = 8
N_EXPERTS = N_GROUPS * EXPERTS_PER_GROUP
EXPERT_FF = 512
TOP_K_INNER = 2

LANES = 128
VMEM_LIMIT = 56 * 1024 * 1024
NEG_BIG = -1e30
LOG2E = math.log2(math.e)
INT_MIN = -(2 ** 31)
KEY_NEG_INF = -2139095041

GLA_CT = 512
IDX_QG = 256
ATT_T = 512
MOE_BLK = 256
RANK_RB = 1024
ROW_T = 256


def _cp(sem, **kw):
    return pltpu.CompilerParams(dimension_semantics=sem, vmem_limit_bytes=VMEM_LIMIT, **kw)


def _dot(a, b):
    return jnp.dot(a, b, preferred_element_type=F32)


def _dot_nt(a, b):
    return lax.dot_general(a, b, (((1,), (1,)), ((), ())), preferred_element_type=F32)


def _dot_tn(a, b):
    return lax.dot_general(a, b, (((0,), (0,)), ((), ())), preferred_element_type=F32)


def _sigmoid(x):
    return 1.0 / (1.0 + jnp.exp(-x))


def _rmsnorm_kernel(x_ref, g_ref, o_ref):
    x = x_ref[...]
    ms = jnp.mean(x * x, axis=-1, keepdims=True)
    o_ref[...] = (x * lax.rsqrt(ms + EPS) * g_ref[...]).astype(o_ref.dtype)


def rmsnorm(x, g, out_dtype, tm=512):
    m, d = x.shape
    tm = min(tm, m)
    return pl.pallas_call(
        _rmsnorm_kernel,
        grid=(m // tm,),
        in_specs=[pl.BlockSpec((tm, d), lambda i: (i, 0)), pl.BlockSpec((1, d), lambda i: (0, 0))],
        out_specs=pl.BlockSpec((tm, d), lambda i: (i, 0)),
        out_shape=jax.ShapeDtypeStruct((m, d), out_dtype),
        compiler_params=_cp(("parallel",)),
        name="rmsnorm",
    )(x, g.reshape(1, d).astype(F32))


def _mm_kernel(*refs, n_extra, epilogue):
    a_ref, w_ref = refs[0], refs[1]
    extras = refs[2:2 + n_extra]
    o_ref = refs[2 + n_extra]
    acc = _dot(a_ref[...], w_ref[...])
    epilogue(acc, o_ref, *extras)


def matmul(a, w, epilogue, extras, out_dtype, name, tm=1024, tn=512):
    m, k = a.shape
    n = w.shape[1]
    tm, tn = min(tm, m), min(tn, n)
    specs = [pl.BlockSpec((tm, k), lambda i, j: (i, 0)), pl.BlockSpec((k, tn), lambda i, j: (0, j))]
    args = [a, w]
    for arr, kind in extras:
        if kind == "tile":
            specs.append(pl.BlockSpec((tm, tn), lambda i, j: (i, j)))
        elif kind == "row":
            specs.append(pl.BlockSpec((1, tn), lambda i, j: (0, j)))
        elif kind == "mk":
            specs.append(pl.BlockSpec((tm, arr.shape[1]), lambda i, j: (i, 0)))
        elif kind == "kn":
            specs.append(pl.BlockSpec((arr.shape[0], tn), lambda i, j: (0, j)))
        else:
            raise ValueError(kind)
        args.append(arr)
    return pl.pallas_call(
        functools.partial(_mm_kernel, n_extra=len(extras), epilogue=epilogue),
        grid=(m // tm, n // tn),
        in_specs=specs,
        out_specs=pl.BlockSpec((tm, tn), lambda i, j: (i, j)),
        out_shape=jax.ShapeDtypeStruct((m, n), out_dtype),
        compiler_params=_cp(("parallel", "arbitrary")),
        name=name,
    )(*args)


def _epi_plain(acc, o_ref):
    o_ref[...] = acc.astype(o_ref.dtype)


def _epi_headnorm(acc, o_ref, g_ref):
    for c in range(acc.shape[1] // DSA_HD):
        sl = slice(c * DSA_HD, (c + 1) * DSA_HD)
        blk = acc[:, sl]
        ms = jnp.mean(blk * blk, axis=-1, keepdims=True)
        o_ref[:, sl] = (blk * lax.rsqrt(ms + EPS) * g_ref[:, sl]).astype(o_ref.dtype)


def _epi_sigmoid_bias(acc, o_ref, b_ref):
    o_ref[...] = _sigmoid(acc + b_ref[...]).astype(o_ref.dtype)


def _epi_residual(acc, o_ref, x_ref):
    o_ref[...] = (x_ref[...] + acc).astype(o_ref.dtype)


def _epi_ple(acc, o_ref, b_ref, x_ref, p_ref, wp_ref):
    ple = _dot(p_ref[...], wp_ref[...])
    o_ref[...] = (x_ref[...] + _sigmoid(acc + b_ref[...]) * ple).astype(o_ref.dtype)


def _mix_kernel(oa_ref, ob_ref, wa_ref, wb_ref, ga_ref, gb_ref, o_ref):
    ya = _dot(oa_ref[...], wa_ref[...])
    yb = _dot(ob_ref[...], wb_ref[...])
    o_ref[...] = (ga_ref[...].astype(F32) * ya + gb_ref[...].astype(F32) * yb).astype(o_ref.dtype)


def mix(oa, ob, wa, wb, gates, tm=1024, tn=512):
    m, k = oa.shape
    n = wa.shape[1]
    tm, tn = min(tm, m), min(tn, n)
    nj = n // tn
    return pl.pallas_call(
        _mix_kernel,
        grid=(m // tm, nj),
        in_specs=[
            pl.BlockSpec((tm, k), lambda i, j: (i, 0)),
            pl.BlockSpec((tm, k), lambda i, j: (i, 0)),
            pl.BlockSpec((k, tn), lambda i, j: (0, j)),
            pl.BlockSpec((k, tn), lambda i, j: (0, j)),
            pl.BlockSpec((tm, tn), lambda i, j: (i, j)),
            pl.BlockSpec((tm, tn), lambda i, j: (i, j + nj)),
        ],
        out_specs=pl.BlockSpec((tm, tn), lambda i, j: (i, j)),
        out_shape=jax.ShapeDtypeStruct((m, n), BF16),
        compiler_params=_cp(("parallel", "arbitrary")),
        name="mix",
    )(oa, ob, wa, wb, gates, gates)


def _gla_kernel(q_ref, k_ref, v_ref, gr_ref, zs_ref, wa_ref, ba_ref, gn_ref, o_ref, st_ref, *, nchunk):
    @pl.when(pl.program_id(2) == 0)
    def _():
        st_ref[...] = jnp.zeros_like(st_ref)

    glr = zs_ref[:, 0:GLA_GATE_RANK].astype(BF16)
    logit = _dot(glr, wa_ref[...]) + ba_ref[...]
    g = (jnp.minimum(logit, 0.0) - jnp.log(1.0 + jnp.exp(-jnp.abs(logit)))) * (1.0 / GLA_GATE_TEMP)
    g_hi = g.astype(BF16)
    g_lo = (g - g_hi.astype(F32)).astype(BF16)
    row = lax.broadcasted_iota(I32, (CHUNK, CHUNK), 0)
    col = lax.broadcasted_iota(I32, (CHUNK, CHUNK), 1)
    causal = row >= col
    tril = jnp.where(causal, 1.0, 0.0).astype(BF16)
    gn = gn_ref[...]
    mid = CHUNK // 2 - 1
    sls = [slice(c * CHUNK, (c + 1) * CHUNK) for c in range(nchunk)]
    cums = [_dot(tril, g_hi[sl]) + _dot(tril, g_lo[sl]) for sl in sls]
    qts, kts, qgs, kds, decs = [], [], [], [], []
    for c, sl in enumerate(sls):
        cum = cums[c]
        c0 = cum[mid:mid + 1, :]
        gl = cum[CHUNK - 1:CHUNK, :]
        qt = q_ref[sl, :].astype(F32) * (GLA_DK ** -0.5) * jnp.exp(cum - c0)
        kt = k_ref[sl, :].astype(F32) * jnp.exp(c0 - cum)
        qts.append(qt.astype(BF16))
        kts.append(kt.astype(BF16))
        qgs.append((qt * jnp.exp(c0)).astype(BF16))
        kds.append((kt * jnp.exp(gl - c0)).astype(BF16))
        decs.append(jnp.exp(gl))
    scs = [_dot_nt(qts[c], kts[c]) for c in range(nchunk)]
    kvs = [_dot_tn(v_ref[sl, :], kds[c]) for c, sl in enumerate(sls)]
    intra = [_dot(jnp.where(causal, scs[c], 0.0).astype(BF16), v_ref[sl, :]) for c, sl in enumerate(sls)]
    st = st_ref[...]
    sts = []
    for c in range(nchunk):
        sts.append(st.astype(BF16))
        st = st * decs[c] + kvs[c]
    st_ref[...] = st
    inter = [_dot_nt(qgs[c], sts[c]) for c in range(nchunk)]
    for c, sl in enumerate(sls):
        o = intra[c] + inter[c]
        ms = jnp.mean(o * o, axis=-1, keepdims=True)
        gr = gr_ref[sl, :].astype(F32)
        o_ref[sl, :] = (o * lax.rsqrt(ms + EPS) * gn * (gr * _sigmoid(gr))).astype(o_ref.dtype)


def gla(q_hm, k_hm, zg, zs, wa, ba, gn, b, s):
    ct = min(GLA_CT, s)
    ns = s // ct
    v_blk0 = (2 * GLA_QK) // GLA_DV
    gr_blk0 = (2 * GLA_QK + GLA_V) // GLA_DV
    return pl.pallas_call(
        functools.partial(_gla_kernel, nchunk=ct // CHUNK),
        grid=(b, GLA_HEADS, ns),
        in_specs=[
            pl.BlockSpec((None, None, ct, GLA_DK), lambda bi, h, si: (bi, h, si, 0)),
            pl.BlockSpec((None, None, ct, GLA_DK), lambda bi, h, si: (bi, h, si, 0)),
            pl.BlockSpec((ct, GLA_DV), lambda bi, h, si: (bi * ns + si, v_blk0 + h)),
            pl.BlockSpec((ct, GLA_DV), lambda bi, h, si: (bi * ns + si, gr_blk0 + h)),
            pl.BlockSpec((ct, LANES), lambda bi, h, si: (bi * ns + si, 0)),
            pl.BlockSpec((None, GLA_GATE_RANK, GLA_DK), lambda bi, h, si: (h, 0, 0)),
            pl.BlockSpec((None, 1, GLA_DK), lambda bi, h, si: (h, 0, 0)),
            pl.BlockSpec((1, GLA_DV), lambda bi, h, si: (0, 0)),
        ],
        out_specs=pl.BlockSpec((ct, GLA_DV), lambda bi, h, si: (bi * ns + si, h)),
        out_shape=jax.ShapeDtypeStruct((b * s, GLA_V), BF16),
        scratch_shapes=[pltpu.VMEM((GLA_DV, GLA_DK), F32)],
        compiler_params=_cp(("parallel", "parallel", "arbitrary")),
        name="gla",
    )(q_hm, k_hm, zg, zg, zs, wa, ba, gn)


def _order_key(x):
    bits = lax.bitcast_convert_type(x, I32)
    return bits ^ ((bits >> 31) & 0x7FFFFFFF)


def _idx_kernel(iq_ref, w_ref, ik_ref, mask_ref, key_ref, *, qg, s):
    g = pl.program_id(1)
    nt = g + 1
    w = w_ref[...] * ((IDX_HEADS ** -0.5) * (IDX_DIM ** -0.5))
    kchunk = lax.broadcasted_iota(I32, (qg, qg), 0) // CHUNK
    qchunk = lax.broadcasted_iota(I32, (qg, qg), 1) // CHUNK
    adm_diag = kchunk <= qchunk

    def score_tile(t, segmax):
        off = pl.multiple_of(t * qg, qg)
        ikt = ik_ref[pl.ds(off, qg), :]
        acc = jnp.zeros((qg, qg), F32)
        for j in range(IDX_HEADS):
            acc = acc + jnp.maximum(_dot_nt(ikt, iq_ref[j]), 0.0) * w[j:j + 1, :]
        acc = jnp.where(jnp.logical_or(t < g, adm_diag), acc, -jnp.inf)
        key_ref[pl.ds(off, qg), :] = _order_key(acc)
        return jnp.maximum(segmax, acc)

    segmax = lax.fori_loop(0, nt, score_tile, jnp.full((qg, qg), -jnp.inf, F32))

    topk = min(INDEX_TOPK, s // 4)
    kf = float(topk)
    lb = _order_key(jnp.min(segmax, axis=0, keepdims=True))
    mx = _order_key(jnp.max(segmax, axis=0, keepdims=True))
    qcol = lax.broadcasted_iota(I32, (1, qg), 1)
    n_adm = g * qg + (qcol // CHUNK + 1) * CHUNK
    small = n_adm <= topk

    def count_ge(cand):
        def body(t, acc):
            off = pl.multiple_of(t * qg, qg)
            kk = key_ref[pl.ds(off, qg), :]
            m = jnp.where(kk >= cand, 1.0, 0.0)
            return acc + jnp.sum(m.reshape(qg // 8, 8, qg), axis=0)
        acc = lax.fori_loop(0, nt, body, jnp.zeros((8, qg), F32))
        return jnp.sum(acc, axis=0, keepdims=True)

    hi0 = mx + 1
    c_lo0 = count_ge(lb)
    done0 = jnp.where(jnp.logical_or(small, jnp.logical_or(c_lo0 == kf, hi0 == lb + 1)), 1.0, 0.0)

    def cond(c):
        i, _, _, _, _, done = c
        return jnp.logical_and(i < 80, jnp.min(done) < 0.5)

    def body(c):
        i, lo, hi, c_lo, c_hi, done = c
        width = hi.astype(F32) - lo.astype(F32)
        frac = (c_lo - kf + 0.5) / jnp.maximum(c_lo - c_hi, 1.0)
        frac = jnp.where(i % 2 == 0, frac, 0.5)
        delta = jnp.clip(frac * width, 1.0, float(2 ** 30)).astype(I32)
        cand = lo + delta
        cand = jnp.where(cand < lo, hi - 1, cand)
        cand = jnp.maximum(jnp.minimum(cand, hi - 1), lo + 1)
        cnt = count_ge(cand)
        live = done < 0.5
        up = jnp.logical_and(live, cnt >= kf)
        down = jnp.logical_and(live, cnt < kf)
        lo = jnp.where(up, cand, lo)
        c_lo = jnp.where(up, cnt, c_lo)
        hi = jnp.where(down, cand, hi)
        c_hi = jnp.where(down, cnt, c_hi)
        done = jnp.where(jnp.logical_or(c_lo == kf, hi == lo + 1), 1.0, done)
        return i + 1, lo, hi, c_lo, c_hi, done

    _, thr, _, _, _, _ = lax.while_loop(
        cond, body, (jnp.int32(0), lb, hi0, c_lo0, jnp.zeros((1, qg), F32), done0))
    thr = jnp.where(small, KEY_NEG_INF + 1, jnp.maximum(thr, KEY_NEG_INF + 1))

    def write_tile(t, carry):
        off = pl.multiple_of(t * qg, qg)
        kk = key_ref[pl.ds(off, qg), :]
        mask_ref[pl.ds(off, qg), :] = jnp.where(kk >= thr, 0.0, NEG_BIG).astype(BF16)
        return carry

    lax.fori_loop(0, nt, write_tile, 0)

    def fill_tile(t, carry):
        off = pl.multiple_of(t * qg, qg)
        mask_ref[pl.ds(off, qg), :] = jnp.full((qg, qg), NEG_BIG, BF16)
        return carry

    lax.fori_loop(nt, s // qg, fill_tile, 0)


def dsa_index(iq_hm, w_hm, ik, b, s):
    qg = min(IDX_QG, s)
    ng = s // qg
    return pl.pallas_call(
        functools.partial(_idx_kernel, qg=qg, s=s),
        grid=(b, ng),
        in_specs=[
            pl.BlockSpec((None, IDX_HEADS, qg, IDX_DIM), lambda bi, gi: (bi, 0, gi, 0)),
            pl.BlockSpec((None, IDX_HEADS, qg), lambda bi, gi: (bi, 0, gi)),
            pl.BlockSpec((None, s, IDX_DIM), lambda bi, gi: (bi, 0, 0)),
        ],
        out_specs=pl.BlockSpec((None, s, qg), lambda bi, gi: (bi, 0, gi)),
        out_shape=jax.ShapeDtypeStruct((b, s, s), BF16),
        scratch_shapes=[pltpu.VMEM((s, qg), I32)],
        compiler_params=_cp(("parallel", "arbitrary")),
        name="dsa_index",
    )(iq_hm, w_hm, ik)


def _attn_kernel(qi_ref, ki_ref, q_ref, k_ref, vt_ref, mask_ref, nbd_ref, nbp_ref, o_ref, m_ref, l_ref, acc_ref):
    p = pl.program_id(1)
    qi = qi_ref[p]
    ki = ki_ref[p]

    @pl.when(ki == 0)
    def _():
        m_ref[...] = jnp.full_like(m_ref, NEG_BIG)
        l_ref[...] = jnp.zeros_like(l_ref)
        acc_ref[...] = jnp.zeros_like(acc_ref)

    def hs(h):
        return slice(h * DSA_HD, (h + 1) * DSA_HD)

    def heads(bias_of_head):
        def scores(h):
            return _dot_nt(k_ref[:, hs(h)], q_ref[:, hs(h)])

        pending = {0: scores(0), 1: scores(1)}
        for h in range(DSA_HEADS):
            sc = pending.pop(h) + bias_of_head(h)
            m_old = m_ref[h]
            m_new = jnp.maximum(m_old, jnp.max(sc, axis=0, keepdims=True))
            alpha = jnp.exp2(m_old - m_new)
            pr = jnp.exp2(sc - m_new)
            l_ref[h] = alpha * l_ref[h] + jnp.sum(pr, axis=0, keepdims=True)
            m_ref[h] = m_new
            if h + 2 < DSA_HEADS:
                pending[h + 2] = scores(h + 2)
            acc_ref[hs(h), :] = alpha * acc_ref[hs(h), :] + _dot(vt_ref[hs(h), :], pr.astype(BF16))

    @pl.when(ki < qi - 1)
    def _():
        mb = mask_ref[...].astype(F32)
        heads(lambda h: mb)

    @pl.when(ki == qi - 1)
    def _():
        mb = mask_ref[...].astype(F32)
        heads(lambda h: mb + nbp_ref[h].astype(F32))

    @pl.when(ki == qi)
    def _():
        mb = mask_ref[...].astype(F32)
        heads(lambda h: mb + nbd_ref[h].astype(F32))
        for h in range(DSA_HEADS):
            o_ref[hs(h), :] = (acc_ref[hs(h), :] / l_ref[h]).astype(o_ref.dtype)


def dsa_attention(zqk, vt, mask, nbd, nbp, b, s):
    t = min(ATT_T, s)
    ng = s // t
    pairs = [(qi, ki) for qi in range(ng) for ki in range(qi + 1)]
    qi_tab = jnp.asarray([p[0] for p in pairs], I32)
    ki_tab = jnp.asarray([p[1] for p in pairs], I32)
    grid_spec = pltpu.PrefetchScalarGridSpec(
        num_scalar_prefetch=2,
        grid=(b, len(pairs)),
        in_specs=[
            pl.BlockSpec((t, DSA_W), lambda bi, p, qt, kt: (bi * ng + qt[p], 0)),
            pl.BlockSpec((t, DSA_W), lambda bi, p, qt, kt: (bi * ng + kt[p], 1)),
            pl.BlockSpec((None, DSA_W, t), lambda bi, p, qt, kt: (bi, 0, kt[p])),
            pl.BlockSpec((None, t, t), lambda bi, p, qt, kt: (bi, kt[p], qt[p])),
            pl.BlockSpec((DSA_HEADS, t, t), lambda bi, p, qt, kt: (0, 0, 0)),
            pl.BlockSpec((DSA_HEADS, t, t), lambda bi, p, qt, kt: (0, 0, 0)),
        ],
        out_specs=pl.BlockSpec((None, DSA_W, t), lambda bi, p, qt, kt: (bi, 0, qt[p])),
        scratch_shapes=[
            pltpu.VMEM((DSA_HEADS, 1, t), F32),
            pltpu.VMEM((DSA_HEADS, 1, t), F32),
            pltpu.VMEM((DSA_W, t), F32),
        ],
    )
    return pl.pallas_call(
        _attn_kernel,
        grid_spec=grid_spec,
        out_shape=jax.ShapeDtypeStruct((b, DSA_W, s), BF16),
        compiler_params=_cp(("parallel", "arbitrary")),
        name="dsa_attention",
    )(qi_tab, ki_tab, zqk, zqk, vt, mask, nbd, nbp)


def _t5_bucket(rel):
    half = REL_BUCKETS // 2
    exact = half // 2
    sign = jnp.where(rel > 0, half, 0)
    n = jnp.abs(rel)
    nf = jnp.maximum(n, 1).astype(F32)
    large = exact + (jnp.log(nf / exact) / math.log(REL_MAX_DIST / exact) * (half - exact)).astype(I32)
    large = jnp.minimum(large, half - 1)
    return sign + jnp.where(n < exact, n, large)


def _near_bias_tables(rel_bias, t):
    kpos = jnp.arange(t, dtype=I32)[:, None]
    qpos = jnp.arange(t, dtype=I32)[None, :]
    shifted = (rel_bias - rel_bias[REL_BUCKETS // 2 - 1]) * LOG2E

    def table(rel):
        onehot = (_t5_bucket(rel)[:, :, None] == jnp.arange(REL_BUCKETS, dtype=I32)).astype(F32)
        return jnp.einsum("kqb,bh->hkq", onehot, shifted, precision=lax.Precision.HIGHEST).astype(BF16)

    return table(kpos - qpos), table(kpos - t - qpos)


def _router_kernel(x_ref, g_ref, w_ref, b_ref, hn_ref, ri_ref, rw_ref):
    x = x_ref[...]
    ms = jnp.mean(x * x, axis=-1, keepdims=True)
    hn = x * lax.rsqrt(ms + EPS) * g_ref[...]
    hn_ref[...] = hn
    logits = _dot(hn.astype(BF16), w_ref[...]) + b_ref[...]
    lane = lax.broadcasted_iota(I32, logits.shape, 1)
    lanef = lane.astype(F32)
    big = 1e9
    is_grp = lane < N_GROUPS
    gl = jnp.where(is_grp, logits, -jnp.inf)
    gmax = jnp.max(gl, axis=-1, keepdims=True)
    grp = jnp.min(jnp.where(gl == gmax, lanef, big), axis=-1, keepdims=True)
    gsum = jnp.sum(jnp.where(is_grp, jnp.exp(logits - gmax), 0.0), axis=-1, keepdims=True)
    grp_w = 1.0 / gsum
    elo = N_GROUPS + grp * EXPERTS_PER_GROUP
    in_grp = jnp.logical_and(lanef >= elo, lanef < elo + EXPERTS_PER_GROUP)
    el = jnp.where(in_grp, logits, -jnp.inf)
    v1 = jnp.max(el, axis=-1, keepdims=True)
    i1 = jnp.min(jnp.where(el == v1, lanef, big), axis=-1, keepdims=True)
    el2 = jnp.where(lanef == i1, -jnp.inf, el)
    v2 = jnp.max(el2, axis=-1, keepdims=True)
    i2 = jnp.min(jnp.where(el2 == v2, lanef, big), axis=-1, keepdims=True)
    e2 = jnp.exp(v2 - v1)
    w1 = grp_w * (1.0 / (1.0 + e2))
    w2 = grp_w * (e2 / (1.0 + e2))
    ri_ref[...] = jnp.where(lane == 0, i1 - N_GROUPS, jnp.where(lane == 1, i2 - N_GROUPS, 0.0)).astype(I32)
    rw_ref[...] = jnp.where(lane == 0, w1, jnp.where(lane == 1, w2, 0.0))


def router(x1, g, w_r, b_r, tm=256):
    m, d = x1.shape
    tm = min(tm, m)
    return pl.pallas_call(
        _router_kernel,
        grid=(m // tm,),
        in_specs=[
            pl.BlockSpec((tm, d), lambda i: (i, 0)),
            pl.BlockSpec((1, d), lambda i: (0, 0)),
            pl.BlockSpec((d, LANES), lambda i: (0, 0)),
            pl.BlockSpec((1, LANES), lambda i: (0, 0)),
        ],
        out_specs=[
            pl.BlockSpec((tm, d), lambda i: (i, 0)),
            pl.BlockSpec((tm, LANES), lambda i: (i, 0)),
            pl.BlockSpec((tm, LANES), lambda i: (i, 0)),
        ],
        out_shape=[
            jax.ShapeDtypeStruct((m, d), F32),
            jax.ShapeDtypeStruct((m, LANES), I32),
            jax.ShapeDtypeStruct((m, LANES), F32),
        ],
        compiler_params=_cp(("parallel",)),
        name="router",
    )(x1, g.reshape(1, d).astype(F32), w_r, b_r)


def _rank_kernel(e_ref, rank_ref, cnt_ref, run_ref, *, rb):
    @pl.when(pl.program_id(0) == 0)
    def _():
        run_ref[...] = jnp.zeros_like(run_ref)

    e = e_ref[...]
    sub = lax.broadcasted_iota(I32, (N_EXPERTS, rb), 0)
    hit = sub == e
    oh = jnp.where(hit, 1.0, 0.0)
    a = lax.broadcasted_iota(I32, (rb, rb), 0)
    bcol = lax.broadcasted_iota(I32, (rb, rb), 1)
    upper = jnp.where(a < bcol, 1.0, 0.0).astype(BF16)
    prefix = _dot(oh.astype(BF16), upper)
    run = run_ref[...]
    rank = jnp.sum(jnp.where(hit, prefix + run[:, 0:1], 0.0), axis=0, keepdims=True)
    rank_ref[...] = rank.astype(I32)
    new = run + jnp.sum(oh, axis=1, keepdims=True)
    run_ref[...] = new
    cnt_ref[...] = new


def moe_rank(eid_flat):
    a = eid_flat.shape[0]
    rb = min(RANK_RB, a)
    nb = a // rb
    rank, cnt = pl.pallas_call(
        functools.partial(_rank_kernel, rb=rb),
        grid=(nb,),
        in_specs=[pl.BlockSpec((None, 1, rb), lambda i: (i, 0, 0))],
        out_specs=[
            pl.BlockSpec((None, 1, rb), lambda i: (i, 0, 0)),
            pl.BlockSpec((N_EXPERTS, LANES), lambda i: (0, 0)),
        ],
        out_shape=[
            jax.ShapeDtypeStruct((nb, 1, rb), I32),
            jax.ShapeDtypeStruct((N_EXPERTS, LANES), F32),
        ],
        scratch_shapes=[pltpu.VMEM((N_EXPERTS, LANES), F32)],
        compiler_params=_cp(("arbitrary",)),
        name="moe_rank",
    )(eid_flat.reshape(nb, 1, rb))
    return rank.reshape(a), cnt[:, 0].astype(I32)


def _row_copy(src, dst, sem):
    return pltpu.make_async_copy(src, dst, sem)


def _dispatch_kernel(dest_ref, hn_ref, xs_in_ref, xs_ref, sem, *, tm):
    del xs_in_ref
    base = pl.program_id(0) * (tm * TOP_K_INNER)

    def body(r, carry):
        for slot in range(TOP_K_INNER):
            d = dest_ref[base + r * TOP_K_INNER + slot]
            _row_copy(hn_ref.at[pl.ds(r, 1), :], xs_ref.at[pl.ds(d, 1), :], sem).start()
        return carry

    lax.fori_loop(0, tm, body, 0, unroll=8)
    for _ in range(TOP_K_INNER):
        _row_copy(hn_ref, xs_ref.at[pl.ds(0, tm), :], sem).wait()


def moe_dispatch(dest, hn, xs0):
    m, d = hn.shape
    tm = min(ROW_T, m)
    grid_spec = pltpu.PrefetchScalarGridSpec(
        num_scalar_prefetch=1,
        grid=(m // tm,),
        in_specs=[
            pl.BlockSpec((tm, d), lambda i, dst: (i, 0)),
            pl.BlockSpec(memory_space=pl.ANY),
        ],
        out_specs=pl.BlockSpec(memory_space=pl.ANY),
        scratch_shapes=[pltpu.SemaphoreType.DMA(())],
    )
    return pl.pallas_call(
        functools.partial(_dispatch_kernel, tm=tm),
        grid_spec=grid_spec,
        out_shape=jax.ShapeDtypeStruct(xs0.shape, xs0.dtype),
        input_output_aliases={2: 0},
        compiler_params=_cp(("arbitrary",), has_side_effects=True),
        name="moe_dispatch",
    )(dest, hn, xs0)


def _expert_kernel(be_ref, nv_ref, xs_ref, wg_ref, wu_ref, wd_ref, y_ref, wg_bf, wu_bf, wd_bf):
    i = pl.program_id(0)
    nv = nv_ref[0]
    prev = be_ref[jnp.maximum(i - 1, 0)]
    fresh = jnp.logical_or(i == 0, be_ref[i] != prev)

    @pl.when(jnp.logical_and(i < nv, fresh))
    def _():
        wg_bf[...] = wg_ref[...].astype(BF16)
        wu_bf[...] = wu_ref[...].astype(BF16)
        wd_bf[...] = wd_ref[...].astype(BF16)

    @pl.when(i < nv)
    def _():
        xb = xs_ref[...].astype(BF16)
        hg = _dot(xb, wg_bf[...])
        hu = _dot(xb, wu_bf[...])
        act = (hg * _sigmoid(hg) * hu).astype(BF16)
        y_ref[...] = _dot(act, wd_bf[...])

    @pl.when(i >= nv)
    def _():
        y_ref[...] = jnp.zeros_like(y_ref)


def moe_experts(blk_expert, nvalid, xs, w_gate, w_up, w_down):
    p, d = xs.shape
    nblk = p // MOE_BLK
    ff = w_gate.shape[2]

    def blk(i, be, nv):
        return jnp.minimum(i, nv[0] - 1)

    grid_spec = pltpu.PrefetchScalarGridSpec(
        num_scalar_prefetch=2,
        grid=(nblk,),
        in_specs=[
            pl.BlockSpec((MOE_BLK, d), lambda i, be, nv: (blk(i, be, nv), 0)),
            pl.BlockSpec((None, d, ff), lambda i, be, nv: (be[blk(i, be, nv)], 0, 0)),
            pl.BlockSpec((None, d, ff), lambda i, be, nv: (be[blk(i, be, nv)], 0, 0)),
            pl.BlockSpec((None, ff, d), lambda i, be, nv: (be[blk(i, be, nv)], 0, 0)),
        ],
        out_specs=pl.BlockSpec((MOE_BLK, d), lambda i, be, nv: (i, 0)),
        scratch_shapes=[pltpu.VMEM((d, ff), BF16), pltpu.VMEM((d, ff), BF16), pltpu.VMEM((ff, d), BF16)],
    )
    return pl.pallas_call(
        _expert_kernel,
        grid_spec=grid_spec,
        out_shape=jax.ShapeDtypeStruct((p, d), F32),
        compiler_params=_cp(("arbitrary",)),
        name="moe_experts",
    )(blk_expert, nvalid, xs, w_gate, w_up, w_down)


def _combine_kernel(dest_ref, x_ref, rw_ref, y_ref, o_ref, ybuf, sem, *, tm):
    i = pl.program_id(0)
    nsteps = pl.num_programs(0)

    def gather(step, buf):
        base = step * (tm * TOP_K_INNER)

        def body(r, carry):
            for slot in range(TOP_K_INNER):
                d = dest_ref[base + r * TOP_K_INNER + slot]
                _row_copy(y_ref.at[pl.ds(d, 1), :], ybuf.at[buf, slot, pl.ds(r, 1), :], sem.at[buf]).start()
            return carry

        lax.fori_loop(0, tm, body, 0, unroll=8)

    @pl.when(i == 0)
    def _():
        gather(0, 0)

    @pl.when(i + 1 < nsteps)
    def _():
        gather(i + 1, (i + 1) % 2)

    cur = i % 2
    for slot in range(TOP_K_INNER):
        _row_copy(y_ref.at[pl.ds(0, tm), :], ybuf.at[cur, slot], sem.at[cur]).wait()
    rw = rw_ref[...]
    o_ref[...] = x_ref[...] + rw[:, 0:1] * ybuf[cur, 0] + rw[:, 1:2] * ybuf[cur, 1]


def moe_combine(dest, x1, rw, y):
    m, d = x1.shape
    tm = min(ROW_T, m)
    grid_spec = pltpu.PrefetchScalarGridSpec(
        num_scalar_prefetch=1,
        grid=(m // tm,),
        in_specs=[
            pl.BlockSpec((tm, d), lambda i, dst: (i, 0)),
            pl.BlockSpec((tm, LANES), lambda i, dst: (i, 0)),
            pl.BlockSpec(memory_space=pl.ANY),
        ],
        out_specs=pl.BlockSpec((tm, d), lambda i, dst: (i, 0)),
        scratch_shapes=[pltpu.VMEM((2, TOP_K_INNER, tm, d), F32), pltpu.SemaphoreType.DMA((2,))],
    )
    return pl.pallas_call(
        functools.partial(_combine_kernel, tm=tm),
        grid_spec=grid_spec,
        out_shape=jax.ShapeDtypeStruct((m, d), F32),
        compiler_params=_cp(("arbitrary",)),
        name="moe_combine",
    )(dest, x1, rw, y)


def _layer(x, p, norm_mix_g, w_in, gla_w_alpha, gla_b_alpha, gla_norm_g, w_out_gla, q_norm_g, k_norm_g, rel_bias,
           w_out_dsa, w_branch_gate, b_branch_gate, w_out, norm_ffn_g, w_group_router, b_group_router,
           w_expert_router, b_expert_router, w_exp_gate, w_exp_up, w_exp_down, norm_ple_g, w_ple_gate,
           b_ple_gate, w_ple_proj):
    b, s, d = x.shape
    t = b * s
    x2d = x.reshape(t, d)

    o_gq, o_gk, o_gv, o_gr = 0, GLA_QK, 2 * GLA_QK, 2 * GLA_QK + GLA_V
    o_glr = o_gr + GLA_V
    o_dq = o_glr + GLA_GATE_RANK
    o_dk, o_dv = o_dq + DSA_W, o_dq + 2 * DSA_W
    o_iq = o_dv + DSA_W
    o_ik = o_iq + IDX_Q
    o_iw = o_ik + IDX_DIM
    w_g = w_in[:, o_gq:o_glr].astype(BF16)
    w_qk = w_in[:, o_dq:o_dv].astype(BF16)
    w_vi = w_in[:, o_dv:o_ik].astype(BF16)
    pad = jnp.zeros((d, LANES - GLA_GATE_RANK - IDX_HEADS - IDX_DIM), F32)
    w_s = jnp.concatenate([w_in[:, o_glr:o_dq], w_in[:, o_iw:o_iw + IDX_HEADS], pad,
                           w_in[:, o_ik:o_iw]], axis=1).astype(BF16)

    h = rmsnorm(x2d, norm_mix_g, BF16)
    zg = matmul(h, w_g, _epi_plain, [], BF16, "proj_gla")
    qk_gain = jnp.concatenate([jnp.tile(q_norm_g * (DSA_HD ** -0.5 * LOG2E), DSA_HEADS),
                               jnp.tile(k_norm_g, DSA_HEADS)]).reshape(1, 2 * DSA_W).astype(F32)
    zqk = matmul(h, w_qk, _epi_headnorm, [(qk_gain, "row")], BF16, "proj_dsa_qk")
    zvi = matmul(h, w_vi, _epi_plain, [], BF16, "proj_dsa_vi")
    zs = matmul(h, w_s, _epi_plain, [], F32, "proj_small")
    gates = matmul(h, w_branch_gate.astype(BF16), _epi_sigmoid_bias,
                   [(b_branch_gate.reshape(1, -1).astype(F32), "row")], BF16, "branch_gates")

    q_hm = zg[:, o_gq:o_gk].reshape(b, s, GLA_HEADS, GLA_DK).transpose(0, 2, 1, 3)
    k_hm = zg[:, o_gk:o_gv].reshape(b, s, GLA_HEADS, GLA_DK).transpose(0, 2, 1, 3)
    wa = gla_w_alpha.reshape(GLA_GATE_RANK, GLA_HEADS, GLA_DK).transpose(1, 0, 2).astype(BF16)
    ba = gla_b_alpha.reshape(GLA_HEADS, 1, GLA_DK).astype(F32)
    o_a = gla(q_hm, k_hm, zg, zs, wa, ba, gla_norm_g.reshape(1, GLA_DV).astype(F32), b, s)

    iq_hm = zvi[:, DSA_W:].reshape(b, s, IDX_HEADS, IDX_DIM).transpose(0, 2, 1, 3)
    ik = zs[:, LANES - IDX_DIM:].astype(BF16).reshape(b, s, IDX_DIM)
    w_hm = zs[:, GLA_GATE_RANK:GLA_GATE_RANK + IDX_HEADS].reshape(b, s, IDX_HEADS).transpose(0, 2, 1)
    mask = dsa_index(iq_hm, w_hm, ik, b, s)
    nbd, nbp = _near_bias_tables(rel_bias, min(ATT_T, s))
    vt = zvi[:, :DSA_W].reshape(b, s, DSA_W).transpose(0, 2, 1)
    o_b = dsa_attention(zqk, vt, mask, nbd, nbp, b, s).transpose(0, 2, 1).reshape(t, DSA_W)

    m = mix(o_a, o_b, w_out_gla.astype(BF16), w_out_dsa.astype(BF16), gates)
    x1 = matmul(m, w_out.astype(BF16), _epi_residual, [(x2d, "tile")], F32, "out_proj")

    w_r = jnp.concatenate([w_group_router, w_expert_router,
                           jnp.zeros((d, LANES - N_GROUPS - N_EXPERTS), F32)], axis=1).astype(BF16)
    b_r = jnp.concatenate([b_group_router, b_expert_router,
                           jnp.zeros((LANES - N_GROUPS - N_EXPERTS,), F32)]).reshape(1, LANES)
    hn, ri, rw = router(x1, norm_ffn_g, w_r, b_r)
    a = t * TOP_K_INNER
    eid = ri[:, :TOP_K_INNER].reshape(a)
    rank, counts = moe_rank(eid)
    padded = ((counts + MOE_BLK - 1) // MOE_BLK) * MOE_BLK
    pend = jnp.cumsum(padded)
    pstart = pend - padded
    dest = (pstart[eid] + rank).astype(I32)
    prow = a + N_EXPERTS * MOE_BLK
    nblk = prow // MOE_BLK
    blk_row0 = jnp.arange(nblk, dtype=I32)[:, None] * MOE_BLK
    blk_expert = jnp.minimum(jnp.sum((pend[None, :] <= blk_row0).astype(I32), axis=1), N_EXPERTS - 1)
    nvalid = (pend[-1:] // MOE_BLK).astype(I32)
    xs = moe_dispatch(dest, hn, jnp.zeros((prow, d), F32))
    y = moe_experts(blk_expert, nvalid, xs, w_exp_gate, w_exp_up, w_exp_down)
    x2 = moe_combine(dest, x1, rw, y)

    hp = rmsnorm(x2, norm_ple_g, BF16)
    out = matmul(hp, w_ple_gate.astype(BF16), _epi_ple,
                 [(b_ple_gate.reshape(1, -1).astype(F32), "row"), (x2, "tile"),
                  (p.reshape(t, -1).astype(BF16), "mk"), (w_ple_proj.astype(BF16), "kn")],
                 F32, "ple")
    return out.reshape(b, s, d)


def kernel(x, p, norm_mix_g, w_in, gla_w_alpha, gla_b_alpha, gla_norm_g, w_out_gla, q_norm_g, k_norm_g, rel_bias, w_out_dsa, w_branch_gate, b_branch_gate, w_out, norm_ffn_g, w_group_router, b_group_router, w_expert_router, b_expert_router, w_exp_gate, w_exp_up, w_exp_down, norm_ple_g, w_ple_gate, b_ple_gate, w_ple_proj):
    depth = w_in.shape[0]
    for i in range(depth):
        x = _layer(x, p[i], norm_mix_g[i], w_in[i], gla_w_alpha[i], gla_b_alpha[i], gla_norm_g[i], w_out_gla[i],
                   q_norm_g[i], k_norm_g[i], rel_bias, w_out_dsa[i], w_branch_gate[i], b_branch_gate[i], w_out[i],
                   norm_ffn_g[i], w_group_router[i], b_group_router[i], w_expert_router[i], b_expert_router[i],
                   w_exp_gate[i], w_exp_up[i], w_exp_down[i], norm_ple_g[i], w_ple_gate[i], b_ple_gate[i],
                   w_ple_proj[i])
    return x
```

```python
import functools
import math

import jax
import jax.numpy as jnp
from jax import lax
from jax.experimental import pallas as pl
from jax.experimental.pallas import tpu as pltpu

F32 = jnp.float32
BF16 = jnp.bfloat16
I32 = jnp.int32
I16 = jnp.int16

EPS = 1e-6
CHUNK = 64
GLA_HEADS = 8
GLA_DK = 64
GLA_DV = 128
GLA_GATE_RANK = 16
GLA_GATE_TEMP = 16.0
GLA_QK = GLA_HEADS * GLA_DK
GLA_V = GLA_HEADS * GLA_DV
DSA_HEADS = 8
DSA_HD = 128
DSA_W = DSA_HEADS * DSA_HD
IDX_HEADS = 16
IDX_DIM = 64
IDX_Q = IDX_HEADS * IDX_DIM
INDEX_TOPK = 256
REL_BUCKETS = 32
REL_MAX_DIST = 128
N_GROUPS = 8
EXPERTS_PER_GROUP = 8
N_EXPERTS = N_GROUPS * EXPERTS_PER_GROUP
EXPERT_FF = 512
TOP_K_INNER = 2

LANES = 128
VMEM_LIMIT = 56 * 1024 * 1024
NEG_BIG = -1e30
LOG2E = math.log2(math.e)
INT_MIN = -(2 ** 31)
KEY_NEG_INF = -2139095041
I16_MIN, I16_MAX = -32768, 32767
PACK16 = 16

GLA_CT = 512
IDX_QG = 256
ATT_T = 512
MOE_BLK = 256
RANK_RB = 1024
ROW_T = 256


def _cp(sem, **kw):
    return pltpu.CompilerParams(dimension_semantics=sem, vmem_limit_bytes=VMEM_LIMIT, **kw)


def _dot(a, b):
    return jnp.dot(a, b, preferred_element_type=F32)


def _dot_nt(a, b):
    return lax.dot_general(a, b, (((1,), (1,)), ((), ())), preferred_element_type=F32)


def _dot_tn(a, b):
    return lax.dot_general(a, b, (((0,), (0,)), ((), ())), preferred_element_type=F32)


def _sigmoid(x):
    return 1.0 / (1.0 + jnp.exp(-x))


def _rmsnorm_kernel(x_ref, g_ref, o_ref):
    x = x_ref[...]
    ms = jnp.mean(x * x, axis=-1, keepdims=True)
    o_ref[...] = (x * lax.rsqrt(ms + EPS) * g_ref[...]).astype(o_ref.dtype)


def rmsnorm(x, g, out_dtype, tm=512):
    m, d = x.shape
    tm = min(tm, m)
    return pl.pallas_call(
        _rmsnorm_kernel,
        grid=(m // tm,),
        in_specs=[pl.BlockSpec((tm, d), lambda i: (i, 0)), pl.BlockSpec((1, d), lambda i: (0, 0))],
        out_specs=pl.BlockSpec((tm, d), lambda i: (i, 0)),
        out_shape=jax.ShapeDtypeStruct((m, d), out_dtype),
        compiler_params=_cp(("parallel",)),
        name="rmsnorm",
    )(x, g.reshape(1, d).astype(F32))


def _mm_kernel(*refs, n_extra, epilogue):
    a_ref, w_ref = refs[0], refs[1]
    extras = refs[2:2 + n_extra]
    o_ref = refs[2 + n_extra]
    acc = _dot(a_ref[...], w_ref[...])
    epilogue(acc, o_ref, *extras)


def matmul(a, w, epilogue, extras, out_dtype, name, tm=1024, tn=512):
    m, k = a.shape
    n = w.shape[1]
    tm, tn = min(tm, m), min(tn, n)
    specs = [pl.BlockSpec((tm, k), lambda i, j: (i, 0)), pl.BlockSpec((k, tn), lambda i, j: (0, j))]
    args = [a, w]
    for arr, kind in extras:
        if kind == "tile":
            specs.append(pl.BlockSpec((tm, tn), lambda i, j: (i, j)))
        elif kind == "row":
            specs.append(pl.BlockSpec((1, tn), lambda i, j: (0, j)))
        elif kind == "mk":
            specs.append(pl.BlockSpec((tm, arr.shape[1]), lambda i, j: (i, 0)))
        elif kind == "kn":
            specs.append(pl.BlockSpec((arr.shape[0], tn), lambda i, j: (0, j)))
        else:
            raise ValueError(kind)
        args.append(arr)
    return pl.pallas_call(
        functools.partial(_mm_kernel, n_extra=len(extras), epilogue=epilogue),
        grid=(m // tm, n // tn),
        in_specs=specs,
        out_specs=pl.BlockSpec((tm, tn), lambda i, j: (i, j)),
        out_shape=jax.ShapeDtypeStruct((m, n), out_dtype),
        compiler_params=_cp(("parallel", "arbitrary")),
        name=name,
    )(*args)


def _epi_plain(acc, o_ref):
    o_ref[...] = acc.astype(o_ref.dtype)


def _epi_headnorm(acc, o_ref, g_ref):
    for c in range(acc.shape[1] // DSA_HD):
        sl = slice(c * DSA_HD, (c + 1) * DSA_HD)
        blk = acc[:, sl]
        ms = jnp.mean(blk * blk, axis=-1, keepdims=True)
        o_ref[:, sl] = (blk * lax.rsqrt(ms + EPS) * g_ref[:, sl]).astype(o_ref.dtype)


def _epi_sigmoid_bias(acc, o_ref, b_ref):
    o_ref[...] = _sigmoid(acc + b_ref[...]).astype(o_ref.dtype)


def _epi_residual(acc, o_ref, x_ref):
    o_ref[...] = (x_ref[...] + acc).astype(o_ref.dtype)


def _epi_ple(acc, o_ref, b_ref, x_ref, p_ref, wp_ref):
    ple = _dot(p_ref[...], wp_ref[...])
    o_ref[...] = (x_ref[...] + _sigmoid(acc + b_ref[...]) * ple).astype(o_ref.dtype)


def _mix_kernel(oa_ref, ob_ref, wa_ref, wb_ref, ga_ref, gb_ref, o_ref):
    ya = _dot(oa_ref[...], wa_ref[...])
    yb = _dot(ob_ref[...], wb_ref[...])
    o_ref[...] = (ga_ref[...].astype(F32) * ya + gb_ref[...].astype(F32) * yb).astype(o_ref.dtype)


def mix(oa, ob, wa, wb, gates, tm=1024, tn=512):
    m, k = oa.shape
    n = wa.shape[1]
    tm, tn = min(tm, m), min(tn, n)
    nj = n // tn
    return pl.pallas_call(
        _mix_kernel,
        grid=(m // tm, nj),
        in_specs=[
            pl.BlockSpec((tm, k), lambda i, j: (i, 0)),
            pl.BlockSpec((tm, k), lambda i, j: (i, 0)),
            pl.BlockSpec((k, tn), lambda i, j: (0, j)),
            pl.BlockSpec((k, tn), lambda i, j: (0, j)),
            pl.BlockSpec((tm, tn), lambda i, j: (i, j)),
            pl.BlockSpec((tm, tn), lambda i, j: (i, j + nj)),
        ],
        out_specs=pl.BlockSpec((tm, tn), lambda i, j: (i, j)),
        out_shape=jax.ShapeDtypeStruct((m, n), BF16),
        compiler_params=_cp(("parallel", "arbitrary")),
        name="mix",
    )(oa, ob, wa, wb, gates, gates)


def _gla_kernel(q_ref, k_ref, v_ref, gr_ref, zs_ref, wa_ref, ba_ref, gn_ref, o_ref, st_ref, *, nchunk):
    @pl.when(pl.program_id(2) == 0)
    def _():
        st_ref[...] = jnp.zeros_like(st_ref)

    glr = zs_ref[:, 0:GLA_GATE_RANK].astype(BF16)
    logit = _dot(glr, wa_ref[...]) + ba_ref[...]
    g = (jnp.minimum(logit, 0.0) - jnp.log(1.0 + jnp.exp(-jnp.abs(logit)))) * (1.0 / GLA_GATE_TEMP)
    g_hi = g.astype(BF16)
    g_lo = (g - g_hi.astype(F32)).astype(BF16)
    row = lax.broadcasted_iota(I32, (CHUNK, CHUNK), 0)
    col = lax.broadcasted_iota(I32, (CHUNK, CHUNK), 1)
    causal = row >= col
    tril = jnp.where(causal, 1.0, 0.0).astype(BF16)
    gn = gn_ref[...]
    mid = CHUNK // 2 - 1
    sls = [slice(c * CHUNK, (c + 1) * CHUNK) for c in range(nchunk)]
    cums = [_dot(tril, g_hi[sl]) + _dot(tril, g_lo[sl]) for sl in sls]
    qts, kts, qgs, kds, decs = [], [], [], [], []
    for c, sl in enumerate(sls):
        cum = cums[c]
        c0 = cum[mid:mid + 1, :]
        gl = cum[CHUNK - 1:CHUNK, :]
        qt = q_ref[sl, :].astype(F32) * (GLA_DK ** -0.5) * jnp.exp(cum - c0)
        kt = k_ref[sl, :].astype(F32) * jnp.exp(c0 - cum)
        qts.append(qt.astype(BF16))
        kts.append(kt.astype(BF16))
        qgs.append((qt * jnp.exp(c0)).astype(BF16))
        kds.append((kt * jnp.exp(gl - c0)).astype(BF16))
        decs.append(jnp.exp(gl))
    scs = [_dot_nt(qts[c], kts[c]) for c in range(nchunk)]
    kvs = [_dot_tn(v_ref[sl, :], kds[c]) for c, sl in enumerate(sls)]
    intra = [_dot(jnp.where(causal, scs[c], 0.0).astype(BF16), v_ref[sl, :]) for c, sl in enumerate(sls)]
    st = st_ref[...]
    sts = []
    for c in range(nchunk):
        sts.append(st.astype(BF16))
        st = st * decs[c] + kvs[c]
    st_ref[...] = st
    inter = [_dot_nt(qgs[c], sts[c]) for c in range(nchunk)]
    for c, sl in enumerate(sls):
        o = intra[c] + inter[c]
        ms = jnp.mean(o * o, axis=-1, keepdims=True)
        gr = gr_ref[sl, :].astype(F32)
        o_ref[sl, :] = (o * lax.rsqrt(ms + EPS) * gn * (gr * _sigmoid(gr))).astype(o_ref.dtype)


def gla(q_hm, k_hm, zg, zs, wa, ba, gn, b, s):
    ct = min(GLA_CT, s)
    ns = s // ct
    v_blk0 = (2 * GLA_QK) // GLA_DV
    gr_blk0 = (2 * GLA_QK + GLA_V) // GLA_DV
    return pl.pallas_call(
        functools.partial(_gla_kernel, nchunk=ct // CHUNK),
        grid=(b, GLA_HEADS, ns),
        in_specs=[
            pl.BlockSpec((None, None, ct, GLA_DK), lambda bi, h, si: (bi, h, si, 0)),
            pl.BlockSpec((None, None, ct, GLA_DK), lambda bi, h, si: (bi, h, si, 0)),
            pl.BlockSpec((ct, GLA_DV), lambda bi, h, si: (bi * ns + si, v_blk0 + h)),
            pl.BlockSpec((ct, GLA_DV), lambda bi, h, si: (bi * ns + si, gr_blk0 + h)),
            pl.BlockSpec((ct, LANES), lambda bi, h, si: (bi * ns + si, 0)),
            pl.BlockSpec((None, GLA_GATE_RANK, GLA_DK), lambda bi, h, si: (h, 0, 0)),
            pl.BlockSpec((None, 1, GLA_DK), lambda bi, h, si: (h, 0, 0)),
            pl.BlockSpec((1, GLA_DV), lambda bi, h, si: (0, 0)),
        ],
        out_specs=pl.BlockSpec((ct, GLA_DV), lambda bi, h, si: (bi * ns + si, h)),
        out_shape=jax.ShapeDtypeStruct((b * s, GLA_V), BF16),
        scratch_shapes=[pltpu.VMEM((GLA_DV, GLA_DK), F32)],
        compiler_params=_cp(("parallel", "parallel", "arbitrary")),
        name="gla",
    )(q_hm, k_hm, zg, zg, zs, wa, ba, gn)


def _order_key(x):
    bits = lax.bitcast_convert_type(x, I32)
    return bits ^ ((bits >> 31) & 0x7FFFFFFF)


def _idx_kernel(iq_ref, w_ref, ik_ref, mask_ref, hi_ref, lo_ref, seg_ref, *, qg, s):
    g = pl.program_id(1)
    nt = g + 1
    w = w_ref[...] * ((IDX_HEADS ** -0.5) * (IDX_DIM ** -0.5))
    kchunk = lax.broadcasted_iota(I32, (qg, qg), 0) // CHUNK
    qchunk = lax.broadcasted_iota(I32, (qg, qg), 1) // CHUNK
    adm_diag = kchunk <= qchunk

    half = qg // 2

    def score_tile(t, carry):
        for hf in range(2):
            off = pl.multiple_of(t * qg + hf * half, half)
            ikt = ik_ref[pl.ds(off, half), :]
            acc = jnp.zeros((half, qg), F32)
            for j in range(IDX_HEADS):
                acc = acc + jnp.maximum(_dot_nt(ikt, iq_ref[j]), 0.0) * w[j:j + 1, :]
            acc = jnp.where(jnp.logical_or(t < g, adm_diag[hf * half:(hf + 1) * half, :]), acc, -jnp.inf)
            key = _order_key(acc)
            hi_ref[pl.ds(off, half), :] = (key >> 16).astype(I16)
            lo_ref[pl.ds(off, half), :] = ((key & 0xFFFF) + I16_MIN).astype(I16)
            rows = slice(hf * half, (hf + 1) * half)
            seg_ref[rows, :] = jnp.maximum(seg_ref[rows, :], acc)
        return carry

    seg_ref[...] = jnp.full((qg, qg), -jnp.inf, F32)
    lax.fori_loop(0, nt, score_tile, 0)
    segmax = seg_ref[...]

    topk = min(INDEX_TOPK, s // 4)
    kf = float(topk)
    lb = _order_key(jnp.min(segmax, axis=0, keepdims=True))
    mx = _order_key(jnp.max(segmax, axis=0, keepdims=True))
    qcol = lax.broadcasted_iota(I32, (1, qg), 1)
    n_adm = g * qg + (qcol // CHUNK + 1) * CHUNK
    small = n_adm <= topk

    def count_ge(ref, cand):
        c16 = cand.astype(I16)

        def body(t, acc):
            off = pl.multiple_of(t * qg, qg)
            m = jnp.where(ref[pl.ds(off, qg), :] >= c16, jnp.int16(1), jnp.int16(0))
            for r in range(qg // PACK16):
                acc = acc + m[r * PACK16:(r + 1) * PACK16, :]
            return acc

        acc = lax.fori_loop(0, nt, body, jnp.zeros((PACK16, qg), I16))
        return jnp.sum(acc.astype(F32), axis=0, keepdims=True)

    def bisect(ref, lo0, hi0, target, npass):
        def body(_, c):
            lo, hi, c_hi = c
            mid = jnp.right_shift(lo + hi, 1)
            cnt = count_ge(ref, mid)
            up = cnt >= target
            return jnp.where(up, mid, lo), jnp.where(up, hi, mid), jnp.where(up, c_hi, cnt)

        lo, _, c_hi = lax.fori_loop(0, npass, body, (lo0, hi0, jnp.zeros((1, qg), F32)))
        return lo, c_hi

    a_lo0 = jnp.right_shift(lb, 16)
    a_hi0 = jnp.right_shift(mx, 16) + 1
    width = jnp.where(small, 1, a_hi0 - a_lo0)
    nbits = jnp.zeros((1, qg), F32)
    for b in range(17):
        nbits = nbits + jnp.where(width > (1 << b), 1.0, 0.0)
    a_lo, ca_hi = bisect(hi_ref, a_lo0, a_hi0, kf, jnp.max(nbits).astype(I32))
    a16 = jnp.where(small, I16_MAX, a_lo).astype(I16)

    def keep_equal_hi(t, carry):
        off = pl.multiple_of(t * qg, qg)
        lo_ref[pl.ds(off, qg), :] = jnp.where(hi_ref[pl.ds(off, qg), :] == a16, lo_ref[pl.ds(off, qg), :],
                                              jnp.int16(I16_MIN))
        return carry

    lax.fori_loop(0, nt, keep_equal_hi, 0)

    b_lo, _ = bisect(lo_ref, jnp.full((1, qg), I16_MIN, I32), jnp.full((1, qg), I16_MAX + 1, I32),
                     kf - ca_hi, 16)
    whole = b_lo == I16_MIN
    thr_h = jnp.where(small, jnp.int32(KEY_NEG_INF >> 16), jnp.where(whole, a_lo - 1, a_lo)).astype(I16)
    thr_l = jnp.where(jnp.logical_or(small, whole), I16_MAX, b_lo).astype(I16)

    def write_tile(t, carry):
        off = pl.multiple_of(t * qg, qg)
        zero = jnp.zeros((qg, qg), BF16)
        low = jnp.where(lo_ref[pl.ds(off, qg), :] >= thr_l, zero, jnp.full((qg, qg), NEG_BIG, BF16))
        mask_ref[pl.ds(off, qg), :] = jnp.where(hi_ref[pl.ds(off, qg), :] > thr_h, zero, low)
        return carry

    lax.fori_loop(0, nt, write_tile, 0)

    def fill_tile(t, carry):
        off = pl.multiple_of(t * qg, qg)
        mask_ref[pl.ds(off, qg), :] = jnp.full((qg, qg), NEG_BIG, BF16)
        return carry

    lax.fori_loop(nt, s // qg, fill_tile, 0)


def dsa_index(iq_hm, w_hm, ik, b, s):
    qg = min(IDX_QG, s)
    ng = s // qg
    return pl.pallas_call(
        functools.partial(_idx_kernel, qg=qg, s=s),
        grid=(b, ng),
        in_specs=[
            pl.BlockSpec((None, IDX_HEADS, qg, IDX_DIM), lambda bi, gi: (bi, 0, gi, 0)),
            pl.BlockSpec((None, IDX_HEADS, qg), lambda bi, gi: (bi, 0, gi)),
            pl.BlockSpec((None, s, IDX_DIM), lambda bi, gi: (bi, 0, 0)),
        ],
        out_specs=pl.BlockSpec((None, s, qg), lambda bi, gi: (bi, 0, gi)),
        out_shape=jax.ShapeDtypeStruct((b, s, s), BF16),
        scratch_shapes=[pltpu.VMEM((s, qg), I16), pltpu.VMEM((s, qg), I16), pltpu.VMEM((qg, qg), F32)],
        compiler_params=_cp(("parallel", "arbitrary")),
        name="dsa_index",
    )(iq_hm, w_hm, ik)


def _attn_kernel(qi_ref, ki_ref, q_ref, k_ref, vt_ref, mask_ref, nbd_ref, nbp_ref, o_ref, m_ref, l_ref, acc_ref):
    p = pl.program_id(1)
    qi = qi_ref[p]
    ki = ki_ref[p]

    @pl.when(ki == 0)
    def _():
        m_ref[...] = jnp.full_like(m_ref, NEG_BIG)
        l_ref[...] = jnp.zeros_like(l_ref)
        acc_ref[...] = jnp.zeros_like(acc_ref)

    def hs(h):
        return slice(h * DSA_HD, (h + 1) * DSA_HD)

    def heads(bias_of_head):
        def scores(h):
            return _dot_nt(k_ref[:, hs(h)], q_ref[:, hs(h)])

        pending = {0: scores(0), 1: scores(1)}
        for h in range(DSA_HEADS):
            sc = pending.pop(h) + bias_of_head(h)
            m_old = m_ref[h]
            m_new = jnp.maximum(m_old, jnp.max(sc, axis=0, keepdims=True))
            alpha = jnp.exp2(m_old - m_new)
            pr = jnp.exp2(sc - m_new)
            l_ref[h] = alpha * l_ref[h] + jnp.sum(pr, axis=0, keepdims=True)
            m_ref[h] = m_new
            if h + 2 < DSA_HEADS:
                pending[h + 2] = scores(h + 2)
            acc_ref[hs(h), :] = alpha * acc_ref[hs(h), :] + _dot(vt_ref[hs(h), :], pr.astype(BF16))

    @pl.when(ki < qi - 1)
    def _():
        mb = mask_ref[...].astype(F32)
        heads(lambda h: mb)

    @pl.when(ki == qi - 1)
    def _():
        mb = mask_ref[...].astype(F32)
        heads(lambda h: mb + nbp_ref[h].astype(F32))

    @pl.when(ki == qi)
    def _():
        mb = mask_ref[...].astype(F32)
        heads(lambda h: mb + nbd_ref[h].astype(F32))
        for h in range(DSA_HEADS):
            o_ref[hs(h), :] = (acc_ref[hs(h), :] / l_ref[h]).astype(o_ref.dtype)


def dsa_attention(zqk, vt, mask, nbd, nbp, b, s):
    t = min(ATT_T, s)
    ng = s // t
    pairs = [(qi, ki) for qi in range(ng) for ki in range(qi + 1)]
    qi_tab = jnp.asarray([p[0] for p in pairs], I32)
    ki_tab = jnp.asarray([p[1] for p in pairs], I32)
    grid_spec = pltpu.PrefetchScalarGridSpec(
        num_scalar_prefetch=2,
        grid=(b, len(pairs)),
        in_specs=[
            pl.BlockSpec((t, DSA_W), lambda bi, p, qt, kt: (bi * ng + qt[p], 0)),
            pl.BlockSpec((t, DSA_W), lambda bi, p, qt, kt: (bi * ng + kt[p], 1)),
            pl.BlockSpec((None, DSA_W, t), lambda bi, p, qt, kt: (bi, 0, kt[p])),
            pl.BlockSpec((None, t, t), lambda bi, p, qt, kt: (bi, kt[p], qt[p])),
            pl.BlockSpec((DSA_HEADS, t, t), lambda bi, p, qt, kt: (0, 0, 0)),
            pl.BlockSpec((DSA_HEADS, t, t), lambda bi, p, qt, kt: (0, 0, 0)),
        ],
        out_specs=pl.BlockSpec((None, DSA_W, t), lambda bi, p, qt, kt: (bi, 0, qt[p])),
        scratch_shapes=[
            pltpu.VMEM((DSA_HEADS, 1, t), F32),
            pltpu.VMEM((DSA_HEADS, 1, t), F32),
            pltpu.VMEM((DSA_W, t), F32),
        ],
    )
    return pl.pallas_call(
        _attn_kernel,
        grid_spec=grid_spec,
        out_shape=jax.ShapeDtypeStruct((b, DSA_W, s), BF16),
        compiler_params=_cp(("parallel", "arbitrary")),
        name="dsa_attention",
    )(qi_tab, ki_tab, zqk, zqk, vt, mask, nbd, nbp)


def _t5_bucket(rel):
    half = REL_BUCKETS // 2
    exact = half // 2
    sign = jnp.where(rel > 0, half, 0)
    n = jnp.abs(rel)
    nf = jnp.maximum(n, 1).astype(F32)
    large = exact + (jnp.log(nf / exact) / math.log(REL_MAX_DIST / exact) * (half - exact)).astype(I32)
    large = jnp.minimum(large, half - 1)
    return sign + jnp.where(n < exact, n, large)


def _near_bias_tables(rel_bias, t):
    kpos = jnp.arange(t, dtype=I32)[:, None]
    qpos = jnp.arange(t, dtype=I32)[None, :]
    shifted = (rel_bias - rel_bias[REL_BUCKETS // 2 - 1]) * LOG2E

    def table(rel):
        onehot = (_t5_bucket(rel)[:, :, None] == jnp.arange(REL_BUCKETS, dtype=I32)).astype(F32)
        return jnp.einsum("kqb,bh->hkq", onehot, shifted, precision=lax.Precision.HIGHEST).astype(BF16)

    return table(kpos - qpos), table(kpos - t - qpos)


def _router_kernel(x_ref, g_ref, w_ref, b_ref, hn_ref, ri_ref, rw_ref):
    x = x_ref[...]
    ms = jnp.mean(x * x, axis=-1, keepdims=True)
    hn = x * lax.rsqrt(ms + EPS) * g_ref[...]
    hn_ref[...] = hn
    logits = _dot(hn.astype(BF16), w_ref[...]) + b_ref[...]
    lane = lax.broadcasted_iota(I32, logits.shape, 1)
    lanef = lane.astype(F32)
    big = 1e9
    is_grp = lane < N_GROUPS
    gl = jnp.where(is_grp, logits, -jnp.inf)
    gmax = jnp.max(gl, axis=-1, keepdims=True)
    grp = jnp.min(jnp.where(gl == gmax, lanef, big), axis=-1, keepdims=True)
    gsum = jnp.sum(jnp.where(is_grp, jnp.exp(logits - gmax), 0.0), axis=-1, keepdims=True)
    grp_w = 1.0 / gsum
    elo = N_GROUPS + grp * EXPERTS_PER_GROUP
    in_grp = jnp.logical_and(lanef >= elo, lanef < elo + EXPERTS_PER_GROUP)
    el = jnp.where(in_grp, logits, -jnp.inf)
    v1 = jnp.max(el, axis=-1, keepdims=True)
    i1 = jnp.min(jnp.where(el == v1, lanef, big), axis=-1, keepdims=True)
    el2 = jnp.where(lanef == i1, -jnp.inf, el)
    v2 = jnp.max(el2, axis=-1, keepdims=True)
    i2 = jnp.min(jnp.where(el2 == v2, lanef, big), axis=-1, keepdims=True)
    e2 = jnp.exp(v2 - v1)
    w1 = grp_w * (1.0 / (1.0 + e2))
    w2 = grp_w * (e2 / (1.0 + e2))
    ri_ref[...] = jnp.where(lane == 0, i1 - N_GROUPS, jnp.where(lane == 1, i2 - N_GROUPS, 0.0)).astype(I32)
    rw_ref[...] = jnp.where(lane == 0, w1, jnp.where(lane == 1, w2, 0.0))


def router(x1, g, w_r, b_r, tm=256):
    m, d = x1.shape
    tm = min(tm, m)
    return pl.pallas_call(
        _router_kernel,
        grid=(m // tm,),
        in_specs=[
            pl.BlockSpec((tm, d), lambda i: (i, 0)),
            pl.BlockSpec((1, d), lambda i: (0, 0)),
            pl.BlockSpec((d, LANES), lambda i: (0, 0)),
            pl.BlockSpec((1, LANES), lambda i: (0, 0)),
        ],
        out_specs=[
            pl.BlockSpec((tm, d), lambda i: (i, 0)),
            pl.BlockSpec((tm, LANES), lambda i: (i, 0)),
            pl.BlockSpec((tm, LANES), lambda i: (i, 0)),
        ],
        out_shape=[
            jax.ShapeDtypeStruct((m, d), F32),
            jax.ShapeDtypeStruct((m, LANES), I32),
            jax.ShapeDtypeStruct((m, LANES), F32),
        ],
        compiler_params=_cp(("parallel",)),
        name="router",
    )(x1, g.reshape(1, d).astype(F32), w_r, b_r)


def _rank_kernel(e_ref, rank_ref, cnt_ref, run_ref, *, rb):
    @pl.when(pl.program_id(0) == 0)
    def _():
        run_ref[...] = jnp.zeros_like(run_ref)

    e = e_ref[...]
    sub = lax.broadcasted_iota(I32, (N_EXPERTS, rb), 0)
    hit = sub == e
    oh = jnp.where(hit, 1.0, 0.0)
    a = lax.broadcasted_iota(I32, (rb, rb), 0)
    bcol = lax.broadcasted_iota(I32, (rb, rb), 1)
    upper = jnp.where(a < bcol, 1.0, 0.0).astype(BF16)
    prefix = _dot(oh.astype(BF16), upper)
    run = run_ref[...]
    rank = jnp.sum(jnp.where(hit, prefix + run[:, 0:1], 0.0), axis=0, keepdims=True)
    rank_ref[...] = rank.astype(I32)
    new = run + jnp.sum(oh, axis=1, keepdims=True)
    run_ref[...] = new
    cnt_ref[...] = new


def moe_rank(eid_flat):
    a = eid_flat.shape[0]
    rb = min(RANK_RB, a)
    nb = a // rb
    rank, cnt = pl.pallas_call(
        functools.partial(_rank_kernel, rb=rb),
        grid=(nb,),
        in_specs=[pl.BlockSpec((None, 1, rb), lambda i: (i, 0, 0))],
        out_specs=[
            pl.BlockSpec((None, 1, rb), lambda i: (i, 0, 0)),
            pl.BlockSpec((N_EXPERTS, LANES), lambda i: (0, 0)),
        ],
        out_shape=[
            jax.ShapeDtypeStruct((nb, 1, rb), I32),
            jax.ShapeDtypeStruct((N_EXPERTS, LANES), F32),
        ],
        scratch_shapes=[pltpu.VMEM((N_EXPERTS, LANES), F32)],
        compiler_params=_cp(("arbitrary",)),
        name="moe_rank",
    )(eid_flat.reshape(nb, 1, rb))
    return rank.reshape(a), cnt[:, 0].astype(I32)


def _row_copy(src, dst, sem):
    return pltpu.make_async_copy(src, dst, sem)


def _dispatch_kernel(dest_ref, hn_ref, xs_in_ref, xs_ref, sem, *, tm):
    del xs_in_ref
    base = pl.program_id(0) * (tm * TOP_K_INNER)

    def body(r, carry):
        for slot in range(TOP_K_INNER):
            d = dest_ref[base + r * TOP_K_INNER + slot]
            _row_copy(hn_ref.at[pl.ds(r, 1), :], xs_ref.at[pl.ds(d, 1), :], sem).start()
        return carry

    lax.fori_loop(0, tm, body, 0, unroll=8)
    for _ in range(TOP_K_INNER):
        _row_copy(hn_ref, xs_ref.at[pl.ds(0, tm), :], sem).wait()


def moe_dispatch(dest, hn, xs0):
    m, d = hn.shape
    tm = min(ROW_T, m)
    grid_spec = pltpu.PrefetchScalarGridSpec(
        num_scalar_prefetch=1,
        grid=(m // tm,),
        in_specs=[
            pl.BlockSpec((tm, d), lambda i, dst: (i, 0)),
            pl.BlockSpec(memory_space=pl.ANY),
        ],
        out_specs=pl.BlockSpec(memory_space=pl.ANY),
        scratch_shapes=[pltpu.SemaphoreType.DMA(())],
    )
    return pl.pallas_call(
        functools.partial(_dispatch_kernel, tm=tm),
        grid_spec=grid_spec,
        out_shape=jax.ShapeDtypeStruct(xs0.shape, xs0.dtype),
        input_output_aliases={2: 0},
        compiler_params=_cp(("arbitrary",), has_side_effects=True),
        name="moe_dispatch",
    )(dest, hn, xs0)


def _expert_kernel(be_ref, nv_ref, xs_ref, wg_ref, wu_ref, wd_ref, y_ref, wg_bf, wu_bf, wd_bf):
    i = pl.program_id(0)
    nv = nv_ref[0]
    prev = be_ref[jnp.maximum(i - 1, 0)]
    fresh = jnp.logical_or(i == 0, be_ref[i] != prev)

    @pl.when(jnp.logical_and(i < nv, fresh))
    def _():
        wg_bf[...] = wg_ref[...].astype(BF16)
        wu_bf[...] = wu_ref[...].astype(BF16)
        wd_bf[...] = wd_ref[...].astype(BF16)

    @pl.when(i < nv)
    def _():
        xb = xs_ref[...].astype(BF16)
        hg = _dot(xb, wg_bf[...])
        hu = _dot(xb, wu_bf[...])
        act = (hg * _sigmoid(hg) * hu).astype(BF16)
        y_ref[...] = _dot(act, wd_bf[...])

    @pl.when(i >= nv)
    def _():
        y_ref[...] = jnp.zeros_like(y_ref)


def moe_experts(blk_expert, nvalid, xs, w_gate, w_up, w_down):
    p, d = xs.shape
    nblk = p // MOE_BLK
    ff = w_gate.shape[2]

    def blk(i, be, nv):
        return jnp.minimum(i, nv[0] - 1)

    grid_spec = pltpu.PrefetchScalarGridSpec(
        num_scalar_prefetch=2,
        grid=(nblk,),
        in_specs=[
            pl.BlockSpec((MOE_BLK, d), lambda i, be, nv: (blk(i, be, nv), 0)),
            pl.BlockSpec((None, d, ff), lambda i, be, nv: (be[blk(i, be, nv)], 0, 0)),
            pl.BlockSpec((None, d, ff), lambda i, be, nv: (be[blk(i, be, nv)], 0, 0)),
            pl.BlockSpec((None, ff, d), lambda i, be, nv: (be[blk(i, be, nv)], 0, 0)),
        ],
        out_specs=pl.BlockSpec((MOE_BLK, d), lambda i, be, nv: (i, 0)),
        scratch_shapes=[pltpu.VMEM((d, ff), BF16), pltpu.VMEM((d, ff), BF16), pltpu.VMEM((ff, d), BF16)],
    )
    return pl.pallas_call(
        _expert_kernel,
        grid_spec=grid_spec,
        out_shape=jax.ShapeDtypeStruct((p, d), F32),
        compiler_params=_cp(("arbitrary",)),
        name="moe_experts",
    )(blk_expert, nvalid, xs, w_gate, w_up, w_down)


def _combine_kernel(dest_ref, x_ref, rw_ref, y_ref, o_ref, ybuf, sem, *, tm):
    i = pl.program_id(0)
    nsteps = pl.num_programs(0)

    def gather(step, buf):
        base = step * (tm * TOP_K_INNER)

        def body(r, carry):
            for slot in range(TOP_K_INNER):
                d = dest_ref[base + r * TOP_K_INNER + slot]
                _row_copy(y_ref.at[pl.ds(d, 1), :], ybuf.at[buf, slot, pl.ds(r, 1), :], sem.at[buf]).start()
            return carry

        lax.fori_loop(0, tm, body, 0, unroll=8)

    @pl.when(i == 0)
    def _():
        gather(0, 0)

    @pl.when(i + 1 < nsteps)
    def _():
        gather(i + 1, (i + 1) % 2)

    cur = i % 2
    for slot in range(TOP_K_INNER):
        _row_copy(y_ref.at[pl.ds(0, tm), :], ybuf.at[cur, slot], sem.at[cur]).wait()
    rw = rw_ref[...]
    o_ref[...] = x_ref[...] + rw[:, 0:1] * ybuf[cur, 0] + rw[:, 1:2] * ybuf[cur, 1]


def moe_combine(dest, x1, rw, y):
    m, d = x1.shape
    tm = min(ROW_T, m)
    grid_spec = pltpu.PrefetchScalarGridSpec(
        num_scalar_prefetch=1,
        grid=(m // tm,),
        in_specs=[
            pl.BlockSpec((tm, d), lambda i, dst: (i, 0)),
            pl.BlockSpec((tm, LANES), lambda i, dst: (i, 0)),
            pl.BlockSpec(memory_space=pl.ANY),
        ],
        out_specs=pl.BlockSpec((tm, d), lambda i, dst: (i, 0)),
        scratch_shapes=[pltpu.VMEM((2, TOP_K_INNER, tm, d), F32), pltpu.SemaphoreType.DMA((2,))],
    )
    return pl.pallas_call(
        functools.partial(_combine_kernel, tm=tm),
        grid_spec=grid_spec,
        out_shape=jax.ShapeDtypeStruct((m, d), F32),
        compiler_params=_cp(("arbitrary",)),
        name="moe_combine",
    )(dest, x1, rw, y)


def _layer(x, p, norm_mix_g, w_in, gla_w_alpha, gla_b_alpha, gla_norm_g, w_out_gla, q_norm_g, k_norm_g, rel_bias,
           w_out_dsa, w_branch_gate, b_branch_gate, w_out, norm_ffn_g, w_group_router, b_group_router,
           w_expert_router, b_expert_router, w_exp_gate, w_exp_up, w_exp_down, norm_ple_g, w_ple_gate,
           b_ple_gate, w_ple_proj):
    b, s, d = x.shape
    t = b * s
    x2d = x.reshape(t, d)

    o_gq, o_gk, o_gv, o_gr = 0, GLA_QK, 2 * GLA_QK, 2 * GLA_QK + GLA_V
    o_glr = o_gr + GLA_V
    o_dq = o_glr + GLA_GATE_RANK
    o_dk, o_dv = o_dq + DSA_W, o_dq + 2 * DSA_W
    o_iq = o_dv + DSA_W
    o_ik = o_iq + IDX_Q
    o_iw = o_ik + IDX_DIM
    w_g = w_in[:, o_gq:o_glr].astype(BF16)
    w_qk = w_in[:, o_dq:o_dv].astype(BF16)
    w_vi = w_in[:, o_dv:o_ik].astype(BF16)
    pad = jnp.zeros((d, LANES - GLA_GATE_RANK - IDX_HEADS - IDX_DIM), F32)
    w_s = jnp.concatenate([w_in[:, o_glr:o_dq], w_in[:, o_iw:o_iw + IDX_HEADS], pad,
                           w_in[:, o_ik:o_iw]], axis=1).astype(BF16)

    h = rmsnorm(x2d, norm_mix_g, BF16)
    zg = matmul(h, w_g, _epi_plain, [], BF16, "proj_gla")
    qk_gain = jnp.concatenate([jnp.tile(q_norm_g * (DSA_HD ** -0.5 * LOG2E), DSA_HEADS),
                               jnp.tile(k_norm_g, DSA_HEADS)]).reshape(1, 2 * DSA_W).astype(F32)
    zqk = matmul(h, w_qk, _epi_headnorm, [(qk_gain, "row")], BF16, "proj_dsa_qk")
    zvi = matmul(h, w_vi, _epi_plain, [], BF16, "proj_dsa_vi")
    zs = matmul(h, w_s, _epi_plain, [], F32, "proj_small")
    gates = matmul(h, w_branch_gate.astype(BF16), _epi_sigmoid_bias,
                   [(b_branch_gate.reshape(1, -1).astype(F32), "row")], BF16, "branch_gates")

    q_hm = zg[:, o_gq:o_gk].reshape(b, s, GLA_HEADS, GLA_DK).transpose(0, 2, 1, 3)
    k_hm = zg[:, o_gk:o_gv].reshape(b, s, GLA_HEADS, GLA_DK).transpose(0, 2, 1, 3)
    wa = gla_w_alpha.reshape(GLA_GATE_RANK, GLA_HEADS, GLA_DK).transpose(1, 0, 2).astype(BF16)
    ba = gla_b_alpha.reshape(GLA_HEADS, 1, GLA_DK).astype(F32)
    o_a = gla(q_hm, k_hm, zg, zs, wa, ba, gla_norm_g.reshape(1, GLA_DV).astype(F32), b, s)

    iq_hm = zvi[:, DSA_W:].reshape(b, s, IDX_HEADS, IDX_DIM).transpose(0, 2, 1, 3)
    ik = zs[:, LANES - IDX_DIM:].astype(BF16).reshape(b, s, IDX_DIM)
    w_hm = zs[:, GLA_GATE_RANK:GLA_GATE_RANK + IDX_HEADS].reshape(b, s, IDX_HEADS).transpose(0, 2, 1)
    mask = dsa_index(iq_hm, w_hm, ik, b, s)
    nbd, nbp = _near_bias_tables(rel_bias, min(ATT_T, s))
    vt = zvi[:, :DSA_W].reshape(b, s, DSA_W).transpose(0, 2, 1)
    o_b = dsa_attention(zqk, vt, mask, nbd, nbp, b, s).transpose(0, 2, 1).reshape(t, DSA_W)

    m = mix(o_a, o_b, w_out_gla.astype(BF16), w_out_dsa.astype(BF16), gates)
    x1 = matmul(m, w_out.astype(BF16), _epi_residual, [(x2d, "tile")], F32, "out_proj")

    w_r = jnp.concatenate([w_group_router, w_expert_router,
                           jnp.zeros((d, LANES - N_GROUPS - N_EXPERTS), F32)], axis=1).astype(BF16)
    b_r = jnp.concatenate([b_group_router, b_expert_router,
                           jnp.zeros((LANES - N_GROUPS - N_EXPERTS,), F32)]).reshape(1, LANES)
    hn, ri, rw = router(x1, norm_ffn_g, w_r, b_r)
    a = t * TOP_K_INNER
    eid = ri[:, :TOP_K_INNER].reshape(a)
    rank, counts = moe_rank(eid)
    padded = ((counts + MOE_BLK - 1) // MOE_BLK) * MOE_BLK
    pend = jnp.cumsum(padded)
    pstart = pend - padded
    dest = (pstart[eid] + rank).astype(I32)
    prow = a + N_EXPERTS * MOE_BLK
    nblk = prow // MOE_BLK
    blk_row0 = jnp.arange(nblk, dtype=I32)[:, None] * MOE_BLK
    blk_expert = jnp.minimum(jnp.sum((pend[None, :] <= blk_row0).astype(I32), axis=1), N_EXPERTS - 1)
    nvalid = (pend[-1:] // MOE_BLK).astype(I32)
    xs = moe_dispatch(dest, hn, jnp.zeros((prow, d), F32))
    y = moe_experts(blk_expert, nvalid, xs, w_exp_gate, w_exp_up, w_exp_down)
    x2 = moe_combine(dest, x1, rw, y)

    hp = rmsnorm(x2, norm_ple_g, BF16)
    out = matmul(hp, w_ple_gate.astype(BF16), _epi_ple,
                 [(b_ple_gate.reshape(1, -1).astype(F32), "row"), (x2, "tile"),
                  (p.reshape(t, -1).astype(BF16), "mk"), (w_ple_proj.astype(BF16), "kn")],
                 F32, "ple")
    return out.reshape(b, s, d)


def kernel(x, p, norm_mix_g, w_in, gla_w_alpha, gla_b_alpha, gla_norm_g, w_out_gla, q_norm_g, k_norm_g, rel_bias, w_out_dsa, w_branch_gate, b_branch_gate, w_out, norm_ffn_g, w_group_router, b_group_router, w_expert_router, b_expert_router, w_exp_gate, w_exp_up, w_exp_down, norm_ple_g, w_ple_gate, b_ple_gate, w_ple_proj):
    depth = w_in.shape[0]
    for i in range(depth):
        x = _layer(x, p[i], norm_mix_g[i], w_in[i], gla_w_alpha[i], gla_b_alpha[i], gla_norm_g[i], w_out_gla[i],
                   q_norm_g[i], k_norm_g[i], rel_bias, w_out_dsa[i], w_branch_gate[i], b_branch_gate[i], w_out[i],
                   norm_ffn_g[i], w_group_router[i], b_group_router[i], w_expert_router[i], b_expert_router[i],
                   w_exp_gate[i], w_exp_up[i], w_exp_down[i], norm_ple_g[i], w_ple_gate[i], b_ple_gate[i],
                   w_ple_proj[i])
    return x
```

```python
import functools
import math

import jax
import jax.numpy as jnp
from jax import lax
from jax.experimental import pallas as pl
from jax.experimental.pallas import tpu as pltpu

F32 = jnp.float32
BF16 = jnp.bfloat16
I32 = jnp.int32
I16 = jnp.int16

EPS = 1e-6
CHUNK = 64
GLA_HEADS = 8
GLA_DK = 64
GLA_DV = 128
GLA_GATE_RANK = 16
GLA_GATE_TEMP = 16.0
GLA_QK = GLA_HEADS * GLA_DK
GLA_V = GLA_HEADS * GLA_DV
DSA_HEADS = 8
DSA_HD = 128
DSA_W = DSA_HEADS * DSA_HD
IDX_HEADS = 16
IDX_DIM = 64
IDX_Q = IDX_HEADS * IDX_DIM
INDEX_TOPK = 256
REL_BUCKETS = 32
REL_MAX_DIST = 128
N_GROUPS = 8
EXPERTS_PER_GROUP = 8
N_EXPERTS = N_GROUPS * EXPERTS_PER_GROUP
EXPERT_FF = 512
TOP_K_INNER = 2

LANES = 128
VMEM_LIMIT = 56 * 1024 * 1024
NEG_BIG = -1e30
LOG2E = math.log2(math.e)
KEY_NEG_INF = -2139095041
I16_MIN, I16_MAX = -32768, 32767
PACK16 = 16

GLA_CT = 512
IDX_QG = 256
ATT_T = 512
ATT_VROWS = DSA_HD + PACK16
MOE_BLK = 256
RANK_RB = 1024
ROW_T = 256


def _cp(sem, **kw):
    return pltpu.CompilerParams(dimension_semantics=sem, vmem_limit_bytes=VMEM_LIMIT, **kw)


def _dot(a, b):
    return jnp.dot(a, b, preferred_element_type=F32)


def _dot_nt(a, b):
    return lax.dot_general(a, b, (((1,), (1,)), ((), ())), preferred_element_type=F32)


def _dot_tn(a, b):
    return lax.dot_general(a, b, (((0,), (0,)), ((), ())), preferred_element_type=F32)


def _sigmoid(x):
    return 1.0 / (1.0 + jnp.exp(-x))


def _rmsnorm_kernel(x_ref, g_ref, o_ref):
    x = x_ref[...]
    ms = jnp.mean(x * x, axis=-1, keepdims=True)
    o_ref[...] = (x * lax.rsqrt(ms + EPS) * g_ref[...]).astype(o_ref.dtype)


def rmsnorm(x, g, out_dtype, tm=512):
    m, d = x.shape
    tm = min(tm, m)
    return pl.pallas_call(
        _rmsnorm_kernel,
        grid=(m // tm,),
        in_specs=[pl.BlockSpec((tm, d), lambda i: (i, 0)), pl.BlockSpec((1, d), lambda i: (0, 0))],
        out_specs=pl.BlockSpec((tm, d), lambda i: (i, 0)),
        out_shape=jax.ShapeDtypeStruct((m, d), out_dtype),
        compiler_params=_cp(("parallel",)),
        name="rmsnorm",
    )(x, g.reshape(1, d).astype(F32))


def _mm_kernel(*refs, n_extra, epilogue):
    a_ref, w_ref = refs[0], refs[1]
    extras = refs[2:2 + n_extra]
    o_ref = refs[2 + n_extra]
    acc = _dot(a_ref[...], w_ref[...])
    epilogue(acc, o_ref, *extras)


def matmul(a, w, epilogue, extras, out_dtype, name, tm=1024, tn=512):
    m, k = a.shape
    n = w.shape[1]
    tm, tn = min(tm, m), min(tn, n)
    specs = [pl.BlockSpec((tm, k), lambda i, j: (i, 0)), pl.BlockSpec((k, tn), lambda i, j: (0, j))]
    args = [a, w]
    for arr, kind in extras:
        if kind == "tile":
            specs.append(pl.BlockSpec((tm, tn), lambda i, j: (i, j)))
        elif kind == "row":
            specs.append(pl.BlockSpec((1, tn), lambda i, j: (0, j)))
        elif kind == "mk":
            specs.append(pl.BlockSpec((tm, arr.shape[1]), lambda i, j: (i, 0)))
        elif kind == "kn":
            specs.append(pl.BlockSpec((arr.shape[0], tn), lambda i, j: (0, j)))
        else:
            raise ValueError(kind)
        args.append(arr)
    return pl.pallas_call(
        functools.partial(_mm_kernel, n_extra=len(extras), epilogue=epilogue),
        grid=(m // tm, n // tn),
        in_specs=specs,
        out_specs=pl.BlockSpec((tm, tn), lambda i, j: (i, j)),
        out_shape=jax.ShapeDtypeStruct((m, n), out_dtype),
        compiler_params=_cp(("parallel", "arbitrary")),
        name=name,
    )(*args)


def _epi_plain(acc, o_ref):
    o_ref[...] = acc.astype(o_ref.dtype)


def _epi_headnorm(acc, o_ref, g_ref):
    for c in range(acc.shape[1] // DSA_HD):
        sl = slice(c * DSA_HD, (c + 1) * DSA_HD)
        blk = acc[:, sl]
        ms = jnp.mean(blk * blk, axis=-1, keepdims=True)
        o_ref[:, sl] = (blk * lax.rsqrt(ms + EPS) * g_ref[:, sl]).astype(o_ref.dtype)


def _epi_sigmoid_bias(acc, o_ref, b_ref):
    o_ref[...] = _sigmoid(acc + b_ref[...]).astype(o_ref.dtype)


def _epi_residual(acc, o_ref, x_ref):
    o_ref[...] = (x_ref[...] + acc).astype(o_ref.dtype)


def _epi_ple(acc, o_ref, b_ref, x_ref, p_ref, wp_ref):
    ple = _dot(p_ref[...], wp_ref[...])
    o_ref[...] = (x_ref[...] + _sigmoid(acc + b_ref[...]) * ple).astype(o_ref.dtype)


def _mix_kernel(oa_ref, ob_ref, wa_ref, wb_ref, ga_ref, gb_ref, o_ref):
    ya = _dot(oa_ref[...], wa_ref[...])
    yb = _dot(ob_ref[...], wb_ref[...])
    o_ref[...] = (ga_ref[...].astype(F32) * ya + gb_ref[...].astype(F32) * yb).astype(o_ref.dtype)


def mix(oa, ob, wa, wb, gates, tm=1024, tn=512):
    m, k = oa.shape
    n = wa.shape[1]
    tm, tn = min(tm, m), min(tn, n)
    nj = n // tn
    return pl.pallas_call(
        _mix_kernel,
        grid=(m // tm, nj),
        in_specs=[
            pl.BlockSpec((tm, k), lambda i, j: (i, 0)),
            pl.BlockSpec((tm, k), lambda i, j: (i, 0)),
            pl.BlockSpec((k, tn), lambda i, j: (0, j)),
            pl.BlockSpec((k, tn), lambda i, j: (0, j)),
            pl.BlockSpec((tm, tn), lambda i, j: (i, j)),
            pl.BlockSpec((tm, tn), lambda i, j: (i, j + nj)),
        ],
        out_specs=pl.BlockSpec((tm, tn), lambda i, j: (i, j)),
        out_shape=jax.ShapeDtypeStruct((m, n), BF16),
        compiler_params=_cp(("parallel", "arbitrary")),
        name="mix",
    )(oa, ob, wa, wb, gates, gates)


def _gla_kernel(q_ref, k_ref, v_ref, gr_ref, zs_ref, wa_ref, ba_ref, gn_ref, o_ref, st_ref, *, nchunk):
    @pl.when(pl.program_id(2) == 0)
    def _():
        st_ref[...] = jnp.zeros_like(st_ref)

    glr = zs_ref[:, 0:GLA_GATE_RANK].astype(BF16)
    logit = _dot(glr, wa_ref[...]) + ba_ref[...]
    g = (jnp.minimum(logit, 0.0) - jnp.log(1.0 + jnp.exp(-jnp.abs(logit)))) * (1.0 / GLA_GATE_TEMP)
    g_hi = g.astype(BF16)
    g_lo = (g - g_hi.astype(F32)).astype(BF16)
    row = lax.broadcasted_iota(I32, (CHUNK, CHUNK), 0)
    col = lax.broadcasted_iota(I32, (CHUNK, CHUNK), 1)
    causal = row >= col
    tril = jnp.where(causal, 1.0, 0.0).astype(BF16)
    gn = gn_ref[...]
    mid = CHUNK // 2 - 1
    sls = [slice(c * CHUNK, (c + 1) * CHUNK) for c in range(nchunk)]
    cums = [_dot(tril, g_hi[sl]) + _dot(tril, g_lo[sl]) for sl in sls]
    qts, kts, qgs, kds, decs = [], [], [], [], []
    for c, sl in enumerate(sls):
        cum = cums[c]
        c0 = cum[mid:mid + 1, :]
        gl = cum[CHUNK - 1:CHUNK, :]
        qt = q_ref[sl, :].astype(F32) * (GLA_DK ** -0.5) * jnp.exp(cum - c0)
        kt = k_ref[sl, :].astype(F32) * jnp.exp(c0 - cum)
        qts.append(qt.astype(BF16))
        kts.append(kt.astype(BF16))
        qgs.append((qt * jnp.exp(c0)).astype(BF16))
        kds.append((kt * jnp.exp(gl - c0)).astype(BF16))
        decs.append(jnp.exp(gl))
    scs = [_dot_nt(qts[c], kts[c]) for c in range(nchunk)]
    kvs = [_dot_tn(v_ref[sl, :], kds[c]) for c, sl in enumerate(sls)]
    intra = [_dot(jnp.where(causal, scs[c], 0.0).astype(BF16), v_ref[sl, :]) for c, sl in enumerate(sls)]
    st = st_ref[...]
    sts = []
    for c in range(nchunk):
        sts.append(st.astype(BF16))
        st = st * decs[c] + kvs[c]
    st_ref[...] = st
    inter = [_dot_nt(qgs[c], sts[c]) for c in range(nchunk)]
    for c, sl in enumerate(sls):
        o = intra[c] + inter[c]
        ms = jnp.mean(o * o, axis=-1, keepdims=True)
        gr = gr_ref[sl, :].astype(F32)
        o_ref[sl, :] = (o * lax.rsqrt(ms + EPS) * gn * (gr * _sigmoid(gr))).astype(o_ref.dtype)


def gla(q_hm, k_hm, zg, zs, wa, ba, gn, b, s):
    ct = min(GLA_CT, s)
    ns = s // ct
    v_blk0 = (2 * GLA_QK) // GLA_DV
    gr_blk0 = (2 * GLA_QK + GLA_V) // GLA_DV
    return pl.pallas_call(
        functools.partial(_gla_kernel, nchunk=ct // CHUNK),
        grid=(b, GLA_HEADS, ns),
        in_specs=[
            pl.BlockSpec((None, None, ct, GLA_DK), lambda bi, h, si: (bi, h, si, 0)),
            pl.BlockSpec((None, None, ct, GLA_DK), lambda bi, h, si: (bi, h, si, 0)),
            pl.BlockSpec((ct, GLA_DV), lambda bi, h, si: (bi * ns + si, v_blk0 + h)),
            pl.BlockSpec((ct, GLA_DV), lambda bi, h, si: (bi * ns + si, gr_blk0 + h)),
            pl.BlockSpec((ct, LANES), lambda bi, h, si: (bi * ns + si, 0)),
            pl.BlockSpec((None, GLA_GATE_RANK, GLA_DK), lambda bi, h, si: (h, 0, 0)),
            pl.BlockSpec((None, 1, GLA_DK), lambda bi, h, si: (h, 0, 0)),
            pl.BlockSpec((1, GLA_DV), lambda bi, h, si: (0, 0)),
        ],
        out_specs=pl.BlockSpec((ct, GLA_DV), lambda bi, h, si: (bi * ns + si, h)),
        out_shape=jax.ShapeDtypeStruct((b * s, GLA_V), BF16),
        scratch_shapes=[pltpu.VMEM((GLA_DV, GLA_DK), F32)],
        compiler_params=_cp(("parallel", "parallel", "arbitrary")),
        name="gla",
    )(q_hm, k_hm, zg, zg, zs, wa, ba, gn)


def _order_key(x):
    bits = lax.bitcast_convert_type(x, I32)
    return bits ^ ((bits >> 31) & 0x7FFFFFFF)


def _idx_kernel(iq_ref, w_ref, ik_ref, mask_ref, hi_ref, lo_ref, seg_ref, *, qg, s):
    g = pl.program_id(1)
    nt = g + 1
    w = w_ref[...] * ((IDX_HEADS ** -0.5) * (IDX_DIM ** -0.5))
    kchunk = lax.broadcasted_iota(I32, (qg, qg), 0) // CHUNK
    qchunk = lax.broadcasted_iota(I32, (qg, qg), 1) // CHUNK
    adm_diag = kchunk <= qchunk

    half = qg // 2

    def score_tile(t, carry):
        for hf in range(2):
            off = pl.multiple_of(t * qg + hf * half, half)
            ikt = ik_ref[pl.ds(off, half), :]
            acc = jnp.zeros((half, qg), F32)
            for j in range(IDX_HEADS):
                acc = acc + jnp.maximum(_dot_nt(ikt, iq_ref[j]), 0.0) * w[j:j + 1, :]
            acc = jnp.where(jnp.logical_or(t < g, adm_diag[hf * half:(hf + 1) * half, :]), acc, -jnp.inf)
            key = _order_key(acc)
            hi_ref[pl.ds(off, half), :] = (key >> 16).astype(I16)
            lo_ref[pl.ds(off, half), :] = ((key & 0xFFFF) + I16_MIN).astype(I16)
            rows = slice(hf * half, (hf + 1) * half)
            seg_ref[rows, :] = jnp.maximum(seg_ref[rows, :], acc)
        return carry

    seg_ref[...] = jnp.full((qg, qg), -jnp.inf, F32)
    lax.fori_loop(0, nt, score_tile, 0)
    segmax = seg_ref[...]

    topk = min(INDEX_TOPK, s // 4)
    kf = float(topk)
    lb = _order_key(jnp.min(segmax, axis=0, keepdims=True))
    mx = _order_key(jnp.max(segmax, axis=0, keepdims=True))
    qcol = lax.broadcasted_iota(I32, (1, qg), 1)
    n_adm = g * qg + (qcol // CHUNK + 1) * CHUNK
    small = n_adm <= topk

    def count_ge(ref, cand):
        c16 = cand.astype(I16)

        def body(t, acc):
            off = pl.multiple_of(t * qg, qg)
            m = jnp.where(ref[pl.ds(off, qg), :] >= c16, jnp.int16(1), jnp.int16(0))
            for r in range(qg // PACK16):
                acc = acc + m[r * PACK16:(r + 1) * PACK16, :]
            return acc

        acc = lax.fori_loop(0, nt, body, jnp.zeros((PACK16, qg), I16))
        return jnp.sum(acc.astype(F32), axis=0, keepdims=True)

    def bisect(ref, lo0, hi0, target, npass):
        def body(_, c):
            lo, hi, c_hi = c
            mid = jnp.right_shift(lo + hi, 1)
            cnt = count_ge(ref, mid)
            up = cnt >= target
            return jnp.where(up, mid, lo), jnp.where(up, hi, mid), jnp.where(up, c_hi, cnt)

        lo, _, c_hi = lax.fori_loop(0, npass, body, (lo0, hi0, jnp.zeros((1, qg), F32)))
        return lo, c_hi

    a_lo0 = jnp.right_shift(lb, 16)
    a_hi0 = jnp.right_shift(mx, 16) + 1
    width = jnp.where(small, 1, a_hi0 - a_lo0)
    nbits = jnp.zeros((1, qg), F32)
    for b in range(17):
        nbits = nbits + jnp.where(width > (1 << b), 1.0, 0.0)
    a_lo, ca_hi = bisect(hi_ref, a_lo0, a_hi0, kf, jnp.max(nbits).astype(I32))
    a16 = jnp.where(small, I16_MAX, a_lo).astype(I16)

    def keep_equal_hi(t, carry):
        off = pl.multiple_of(t * qg, qg)
        lo_ref[pl.ds(off, qg), :] = jnp.where(hi_ref[pl.ds(off, qg), :] == a16, lo_ref[pl.ds(off, qg), :],
                                              jnp.int16(I16_MIN))
        return carry

    lax.fori_loop(0, nt, keep_equal_hi, 0)

    b_lo, _ = bisect(lo_ref, jnp.full((1, qg), I16_MIN, I32), jnp.full((1, qg), I16_MAX + 1, I32),
                     kf - ca_hi, 16)
    whole = b_lo == I16_MIN
    thr_h = jnp.where(small, jnp.int32(KEY_NEG_INF >> 16), jnp.where(whole, a_lo - 1, a_lo)).astype(I16)
    thr_l = jnp.where(jnp.logical_or(small, whole), I16_MAX, b_lo).astype(I16)

    def write_tile(t, carry):
        off = pl.multiple_of(t * qg, qg)
        zero = jnp.zeros((qg, qg), BF16)
        low = jnp.where(lo_ref[pl.ds(off, qg), :] >= thr_l, zero, jnp.full((qg, qg), NEG_BIG, BF16))
        mask_ref[pl.ds(off, qg), :] = jnp.where(hi_ref[pl.ds(off, qg), :] > thr_h, zero, low)
        return carry

    lax.fori_loop(0, nt, write_tile, 0)

    def fill_tile(t, carry):
        off = pl.multiple_of(t * qg, qg)
        mask_ref[pl.ds(off, qg), :] = jnp.full((qg, qg), NEG_BIG, BF16)
        return carry

    lax.fori_loop(nt, s // qg, fill_tile, 0)


def dsa_index(iq_hm, w_hm, ik, b, s):
    qg = min(IDX_QG, s)
    ng = s // qg
    return pl.pallas_call(
        functools.partial(_idx_kernel, qg=qg, s=s),
        grid=(b, ng),
        in_specs=[
            pl.BlockSpec((None, IDX_HEADS, qg, IDX_DIM), lambda bi, gi: (bi, 0, gi, 0)),
            pl.BlockSpec((None, IDX_HEADS, qg), lambda bi, gi: (bi, 0, gi)),
            pl.BlockSpec((None, s, IDX_DIM), lambda bi, gi: (bi, 0, 0)),
        ],
        out_specs=pl.BlockSpec((None, s, qg), lambda bi, gi: (bi, 0, gi)),
        out_shape=jax.ShapeDtypeStruct((b, s, s), BF16),
        scratch_shapes=[pltpu.VMEM((s, qg), I16), pltpu.VMEM((s, qg), I16), pltpu.VMEM((qg, qg), F32)],
        compiler_params=_cp(("parallel", "arbitrary")),
        name="dsa_index",
    )(iq_hm, w_hm, ik)


def _attn_kernel(qi_ref, ki_ref, q_ref, k_ref, vt_ref, mask_ref, nbd_ref, nbp_ref, o_ref, m_ref, acc_ref):
    p = pl.program_id(1)
    qi = qi_ref[p]
    ki = ki_ref[p]

    @pl.when(ki == 0)
    def _():
        m_ref[...] = jnp.full_like(m_ref, NEG_BIG)
        acc_ref[...] = jnp.zeros_like(acc_ref)

    def hs(h):
        return slice(h * DSA_HD, (h + 1) * DSA_HD)

    def vs(h):
        return slice(h * ATT_VROWS, (h + 1) * ATT_VROWS)

    nrow = 2
    rch = q_ref.shape[0] // nrow
    kh = k_ref.shape[0] // 2

    def heads(bias_of_head):
        def score_chunk(h, r):
            return _dot_nt(k_ref[r * rch:(r + 1) * rch, hs(h)], q_ref[:, hs(h)])

        def value_half(h, pr, half):
            return _dot(vt_ref[vs(h), half * kh:(half + 1) * kh], pr)

        sc = {0: [score_chunk(0, r) for r in range(nrow)], 1: [score_chunk(1, r) for r in range(nrow)]}
        prev = None
        for h in range(DSA_HEADS + 1):
            if h < DSA_HEADS:
                bias = bias_of_head(h)
                cur = sc.pop(h)
                sb, cm = [], []
                nxt = []
                for r in range(nrow):
                    sb.append(cur[r] + bias[r * rch:(r + 1) * rch, :])
                    cm.append(jnp.max(sb[r], axis=0, keepdims=True))
                    if h + 2 < DSA_HEADS:
                        nxt.append(score_chunk(h + 2, r))
                if nxt:
                    sc[h + 2] = nxt
                m_old = m_ref[h]
                m_new = m_old
                for c in cm:
                    m_new = jnp.maximum(m_new, c)
                alpha = jnp.exp2(m_old - m_new)
                m_ref[h] = m_new
            pv = []
            pr = []
            for r in range(nrow):
                if h < DSA_HEADS:
                    pr.append(jnp.exp2(sb[r] - m_new).astype(BF16))
                if prev is not None and r % (nrow // 2) == 0:
                    pv.append(value_half(prev[0], prev[2][r // (nrow // 2)], r // (nrow // 2)))
            if prev is not None:
                ph = prev[0]
                acc_ref[vs(ph), :] = prev[1] * acc_ref[vs(ph), :] + pv[0] + pv[1]
            if h < DSA_HEADS:
                prev = (h, alpha, [jnp.concatenate(pr[:nrow // 2], axis=0), jnp.concatenate(pr[nrow // 2:], axis=0)])

    @pl.when(ki < qi - 1)
    def _():
        mb = mask_ref[...].astype(F32)
        heads(lambda h: mb)

    @pl.when(ki == qi - 1)
    def _():
        mb = mask_ref[...].astype(F32)
        heads(lambda h: mb + nbp_ref[h].astype(F32))

    @pl.when(ki == qi)
    def _():
        mb = mask_ref[...].astype(F32)
        heads(lambda h: mb + nbd_ref[h].astype(F32))
        for h in range(DSA_HEADS):
            num = acc_ref[h * ATT_VROWS:h * ATT_VROWS + DSA_HD, :]
            den = acc_ref[h * ATT_VROWS + DSA_HD:h * ATT_VROWS + DSA_HD + 1, :]
            o_ref[hs(h), :] = (num / den).astype(o_ref.dtype)


def dsa_attention(zqk, vt, mask, nbd, nbp, b, s):
    t = min(ATT_T, s)
    ng = s // t
    pairs = [(qi, ki) for qi in range(ng) for ki in range(qi + 1)]
    qi_tab = jnp.asarray([p[0] for p in pairs], I32)
    ki_tab = jnp.asarray([p[1] for p in pairs], I32)
    grid_spec = pltpu.PrefetchScalarGridSpec(
        num_scalar_prefetch=2,
        grid=(b, len(pairs)),
        in_specs=[
            pl.BlockSpec((t, DSA_W), lambda bi, p, qt, kt: (bi * ng + qt[p], 0)),
            pl.BlockSpec((t, DSA_W), lambda bi, p, qt, kt: (bi * ng + kt[p], 1)),
            pl.BlockSpec((None, DSA_HEADS * ATT_VROWS, t), lambda bi, p, qt, kt: (bi, 0, kt[p])),
            pl.BlockSpec((None, t, t), lambda bi, p, qt, kt: (bi, kt[p], qt[p])),
            pl.BlockSpec((DSA_HEADS, t, t), lambda bi, p, qt, kt: (0, 0, 0)),
            pl.BlockSpec((DSA_HEADS, t, t), lambda bi, p, qt, kt: (0, 0, 0)),
        ],
        out_specs=pl.BlockSpec((None, DSA_W, t), lambda bi, p, qt, kt: (bi, 0, qt[p])),
        scratch_shapes=[
            pltpu.VMEM((DSA_HEADS, 1, t), F32),
            pltpu.VMEM((DSA_HEADS * ATT_VROWS, t), F32),
        ],
    )
    return pl.pallas_call(
        _attn_kernel,
        grid_spec=grid_spec,
        out_shape=jax.ShapeDtypeStruct((b, DSA_W, s), BF16),
        compiler_params=_cp(("parallel", "arbitrary")),
        name="dsa_attention",
    )(qi_tab, ki_tab, zqk, zqk, vt, mask, nbd, nbp)


def _t5_bucket(rel):
    half = REL_BUCKETS // 2
    exact = half // 2
    sign = jnp.where(rel > 0, half, 0)
    n = jnp.abs(rel)
    nf = jnp.maximum(n, 1).astype(F32)
    large = exact + (jnp.log(nf / exact) / math.log(REL_MAX_DIST / exact) * (half - exact)).astype(I32)
    large = jnp.minimum(large, half - 1)
    return sign + jnp.where(n < exact, n, large)


def _near_bias_tables(rel_bias, t):
    kpos = jnp.arange(t, dtype=I32)[:, None]
    qpos = jnp.arange(t, dtype=I32)[None, :]
    shifted = (rel_bias - rel_bias[REL_BUCKETS // 2 - 1]) * LOG2E

    def table(rel):
        onehot = (_t5_bucket(rel)[:, :, None] == jnp.arange(REL_BUCKETS, dtype=I32)).astype(F32)
        return jnp.einsum("kqb,bh->hkq", onehot, shifted, precision=lax.Precision.HIGHEST).astype(BF16)

    return table(kpos - qpos), table(kpos - t - qpos)


def _router_kernel(x_ref, g_ref, w_ref, b_ref, hn_ref, ri_ref, rw_ref):
    x = x_ref[...]
    ms = jnp.mean(x * x, axis=-1, keepdims=True)
    hn = x * lax.rsqrt(ms + EPS) * g_ref[...]
    hn_ref[...] = hn
    logits = _dot(hn.astype(BF16), w_ref[...]) + b_ref[...]
    lane = lax.broadcasted_iota(I32, logits.shape, 1)
    lanef = lane.astype(F32)
    big = 1e9
    is_grp = lane < N_GROUPS
    gl = jnp.where(is_grp, logits, -jnp.inf)
    gmax = jnp.max(gl, axis=-1, keepdims=True)
    grp = jnp.min(jnp.where(gl == gmax, lanef, big), axis=-1, keepdims=True)
    gsum = jnp.sum(jnp.where(is_grp, jnp.exp(logits - gmax), 0.0), axis=-1, keepdims=True)
    grp_w = 1.0 / gsum
    elo = N_GROUPS + grp * EXPERTS_PER_GROUP
    in_grp = jnp.logical_and(lanef >= elo, lanef < elo + EXPERTS_PER_GROUP)
    el = jnp.where(in_grp, logits, -jnp.inf)
    v1 = jnp.max(el, axis=-1, keepdims=True)
    i1 = jnp.min(jnp.where(el == v1, lanef, big), axis=-1, keepdims=True)
    el2 = jnp.where(lanef == i1, -jnp.inf, el)
    v2 = jnp.max(el2, axis=-1, keepdims=True)
    i2 = jnp.min(jnp.where(el2 == v2, lanef, big), axis=-1, keepdims=True)
    e2 = jnp.exp(v2 - v1)
    w1 = grp_w * (1.0 / (1.0 + e2))
    w2 = grp_w * (e2 / (1.0 + e2))
    ri_ref[...] = jnp.where(lane == 0, i1 - N_GROUPS, jnp.where(lane == 1, i2 - N_GROUPS, 0.0)).astype(I32)
    rw_ref[...] = jnp.where(lane == 0, w1, jnp.where(lane == 1, w2, 0.0))


def router(x1, g, w_r, b_r, tm=256):
    m, d = x1.shape
    tm = min(tm, m)
    return pl.pallas_call(
        _router_kernel,
        grid=(m // tm,),
        in_specs=[
            pl.BlockSpec((tm, d), lambda i: (i, 0)),
            pl.BlockSpec((1, d), lambda i: (0, 0)),
            pl.BlockSpec((d, LANES), lambda i: (0, 0)),
            pl.BlockSpec((1, LANES), lambda i: (0, 0)),
        ],
        out_specs=[
            pl.BlockSpec((tm, d), lambda i: (i, 0)),
            pl.BlockSpec((tm, LANES), lambda i: (i, 0)),
            pl.BlockSpec((tm, LANES), lambda i: (i, 0)),
        ],
        out_shape=[
            jax.ShapeDtypeStruct((m, d), F32),
            jax.ShapeDtypeStruct((m, LANES), I32),
            jax.ShapeDtypeStruct((m, LANES), F32),
        ],
        compiler_params=_cp(("parallel",)),
        name="router",
    )(x1, g.reshape(1, d).astype(F32), w_r, b_r)


def _rank_kernel(e_ref, rank_ref, cnt_ref, run_ref, *, rb):
    @pl.when(pl.program_id(0) == 0)
    def _():
        run_ref[...] = jnp.zeros_like(run_ref)

    e = e_ref[...]
    sub = lax.broadcasted_iota(I32, (N_EXPERTS, rb), 0)
    hit = sub == e
    oh = jnp.where(hit, 1.0, 0.0)
    a = lax.broadcasted_iota(I32, (rb, rb), 0)
    bcol = lax.broadcasted_iota(I32, (rb, rb), 1)
    upper = jnp.where(a < bcol, 1.0, 0.0).astype(BF16)
    prefix = _dot(oh.astype(BF16), upper)
    run = run_ref[...]
    rank = jnp.sum(jnp.where(hit, prefix + run[:, 0:1], 0.0), axis=0, keepdims=True)
    rank_ref[...] = rank.astype(I32)
    new = run + jnp.sum(oh, axis=1, keepdims=True)
    run_ref[...] = new
    cnt_ref[...] = new


def moe_rank(eid_flat):
    a = eid_flat.shape[0]
    rb = min(RANK_RB, a)
    nb = a // rb
    rank, cnt = pl.pallas_call(
        functools.partial(_rank_kernel, rb=rb),
        grid=(nb,),
        in_specs=[pl.BlockSpec((None, 1, rb), lambda i: (i, 0, 0))],
        out_specs=[
            pl.BlockSpec((None, 1, rb), lambda i: (i, 0, 0)),
            pl.BlockSpec((N_EXPERTS, LANES), lambda i: (0, 0)),
        ],
        out_shape=[
            jax.ShapeDtypeStruct((nb, 1, rb), I32),
            jax.ShapeDtypeStruct((N_EXPERTS, LANES), F32),
        ],
        scratch_shapes=[pltpu.VMEM((N_EXPERTS, LANES), F32)],
        compiler_params=_cp(("arbitrary",)),
        name="moe_rank",
    )(eid_flat.reshape(nb, 1, rb))
    return rank.reshape(a), cnt[:, 0].astype(I32)


def _row_copy(src, dst, sem):
    return pltpu.make_async_copy(src, dst, sem)


def _dispatch_kernel(dest_ref, hn_ref, xs_in_ref, xs_ref, sem, *, tm):
    del xs_in_ref
    base = pl.program_id(0) * (tm * TOP_K_INNER)

    def body(r, carry):
        for slot in range(TOP_K_INNER):
            d = dest_ref[base + r * TOP_K_INNER + slot]
            _row_copy(hn_ref.at[pl.ds(r, 1), :], xs_ref.at[pl.ds(d, 1), :], sem).start()
        return carry

    lax.fori_loop(0, tm, body, 0, unroll=8)
    for _ in range(TOP_K_INNER):
        _row_copy(hn_ref, xs_ref.at[pl.ds(0, tm), :], sem).wait()


def moe_dispatch(dest, hn, xs0):
    m, d = hn.shape
    tm = min(ROW_T, m)
    grid_spec = pltpu.PrefetchScalarGridSpec(
        num_scalar_prefetch=1,
        grid=(m // tm,),
        in_specs=[
            pl.BlockSpec((tm, d), lambda i, dst: (i, 0)),
            pl.BlockSpec(memory_space=pl.ANY),
        ],
        out_specs=pl.BlockSpec(memory_space=pl.ANY),
        scratch_shapes=[pltpu.SemaphoreType.DMA(())],
    )
    return pl.pallas_call(
        functools.partial(_dispatch_kernel, tm=tm),
        grid_spec=grid_spec,
        out_shape=jax.ShapeDtypeStruct(xs0.shape, xs0.dtype),
        input_output_aliases={2: 0},
        compiler_params=_cp(("arbitrary",), has_side_effects=True),
        name="moe_dispatch",
    )(dest, hn, xs0)


def _expert_kernel(be_ref, nv_ref, xs_ref, wg_ref, wu_ref, wd_ref, y_ref, wg_bf, wu_bf, wd_bf):
    i = pl.program_id(0)
    nv = nv_ref[0]
    prev = be_ref[jnp.maximum(i - 1, 0)]
    fresh = jnp.logical_or(i == 0, be_ref[i] != prev)

    @pl.when(jnp.logical_and(i < nv, fresh))
    def _():
        wg_bf[...] = wg_ref[...].astype(BF16)
        wu_bf[...] = wu_ref[...].astype(BF16)
        wd_bf[...] = wd_ref[...].astype(BF16)

    @pl.when(i < nv)
    def _():
        xb = xs_ref[...].astype(BF16)
        hg = _dot(xb, wg_bf[...])
        hu = _dot(xb, wu_bf[...])
        act = (hg * _sigmoid(hg) * hu).astype(BF16)
        y_ref[...] = _dot(act, wd_bf[...])

    @pl.when(i >= nv)
    def _():
        y_ref[...] = jnp.zeros_like(y_ref)


def moe_experts(blk_expert, nvalid, xs, w_gate, w_up, w_down):
    p, d = xs.shape
    nblk = p // MOE_BLK
    ff = w_gate.shape[2]

    def blk(i, be, nv):
        return jnp.minimum(i, nv[0] - 1)

    grid_spec = pltpu.PrefetchScalarGridSpec(
        num_scalar_prefetch=2,
        grid=(nblk,),
        in_specs=[
            pl.BlockSpec((MOE_BLK, d), lambda i, be, nv: (blk(i, be, nv), 0)),
            pl.BlockSpec((None, d, ff), lambda i, be, nv: (be[blk(i, be, nv)], 0, 0)),
            pl.BlockSpec((None, d, ff), lambda i, be, nv: (be[blk(i, be, nv)], 0, 0)),
            pl.BlockSpec((None, ff, d), lambda i, be, nv: (be[blk(i, be, nv)], 0, 0)),
        ],
        out_specs=pl.BlockSpec((MOE_BLK, d), lambda i, be, nv: (i, 0)),
        scratch_shapes=[pltpu.VMEM((d, ff), BF16), pltpu.VMEM((d, ff), BF16), pltpu.VMEM((ff, d), BF16)],
    )
    return pl.pallas_call(
        _expert_kernel,
        grid_spec=grid_spec,
        out_shape=jax.ShapeDtypeStruct((p, d), F32),
        compiler_params=_cp(("arbitrary",)),
        name="moe_experts",
    )(blk_expert, nvalid, xs, w_gate, w_up, w_down)


def _combine_kernel(dest_ref, x_ref, rw_ref, y_ref, o_ref, ybuf, sem, *, tm):
    i = pl.program_id(0)
    nsteps = pl.num_programs(0)

    def gather(step, buf):
        base = step * (tm * TOP_K_INNER)

        def body(r, carry):
            for slot in range(TOP_K_INNER):
                d = dest_ref[base + r * TOP_K_INNER + slot]
                _row_copy(y_ref.at[pl.ds(d, 1), :], ybuf.at[buf, slot, pl.ds(r, 1), :], sem.at[buf]).start()
            return carry

        lax.fori_loop(0, tm, body, 0, unroll=8)

    @pl.when(i == 0)
    def _():
        gather(0, 0)

    @pl.when(i + 1 < nsteps)
    def _():
        gather(i + 1, (i + 1) % 2)

    cur = i % 2
    for slot in range(TOP_K_INNER):
        _row_copy(y_ref.at[pl.ds(0, tm), :], ybuf.at[cur, slot], sem.at[cur]).wait()
    rw = rw_ref[...]
    o_ref[...] = x_ref[...] + rw[:, 0:1] * ybuf[cur, 0] + rw[:, 1:2] * ybuf[cur, 1]


def moe_combine(dest, x1, rw, y):
    m, d = x1.shape
    tm = min(ROW_T, m)
    grid_spec = pltpu.PrefetchScalarGridSpec(
        num_scalar_prefetch=1,
        grid=(m // tm,),
        in_specs=[
            pl.BlockSpec((tm, d), lambda i, dst: (i, 0)),
            pl.BlockSpec((tm, LANES), lambda i, dst: (i, 0)),
            pl.BlockSpec(memory_space=pl.ANY),
        ],
        out_specs=pl.BlockSpec((tm, d), lambda i, dst: (i, 0)),
        scratch_shapes=[pltpu.VMEM((2, TOP_K_INNER, tm, d), F32), pltpu.SemaphoreType.DMA((2,))],
    )
    return pl.pallas_call(
        functools.partial(_combine_kernel, tm=tm),
        grid_spec=grid_spec,
        out_shape=jax.ShapeDtypeStruct((m, d), F32),
        compiler_params=_cp(("arbitrary",)),
        name="moe_combine",
    )(dest, x1, rw, y)


def _layer(x, p, norm_mix_g, w_in, gla_w_alpha, gla_b_alpha, gla_norm_g, w_out_gla, q_norm_g, k_norm_g, rel_bias,
           w_out_dsa, w_branch_gate, b_branch_gate, w_out, norm_ffn_g, w_group_router, b_group_router,
           w_expert_router, b_expert_router, w_exp_gate, w_exp_up, w_exp_down, norm_ple_g, w_ple_gate,
           b_ple_gate, w_ple_proj):
    b, s, d = x.shape
    t = b * s
    x2d = x.reshape(t, d)

    o_gq, o_gk, o_gv, o_gr = 0, GLA_QK, 2 * GLA_QK, 2 * GLA_QK + GLA_V
    o_glr = o_gr + GLA_V
    o_dq = o_glr + GLA_GATE_RANK
    o_dk, o_dv = o_dq + DSA_W, o_dq + 2 * DSA_W
    o_iq = o_dv + DSA_W
    o_ik = o_iq + IDX_Q
    o_iw = o_ik + IDX_DIM
    w_g = w_in[:, o_gq:o_glr].astype(BF16)
    w_qk = w_in[:, o_dq:o_dv].astype(BF16)
    w_vi = w_in[:, o_dv:o_ik].astype(BF16)
    pad = jnp.zeros((d, LANES - GLA_GATE_RANK - IDX_HEADS - IDX_DIM), F32)
    w_s = jnp.concatenate([w_in[:, o_glr:o_dq], w_in[:, o_iw:o_iw + IDX_HEADS], pad,
                           w_in[:, o_ik:o_iw]], axis=1).astype(BF16)

    h = rmsnorm(x2d, norm_mix_g, BF16)
    zg = matmul(h, w_g, _epi_plain, [], BF16, "proj_gla")
    qk_gain = jnp.concatenate([jnp.tile(q_norm_g * (DSA_HD ** -0.5 * LOG2E), DSA_HEADS),
                               jnp.tile(k_norm_g, DSA_HEADS)]).reshape(1, 2 * DSA_W).astype(F32)
    zqk = matmul(h, w_qk, _epi_headnorm, [(qk_gain, "row")], BF16, "proj_dsa_qk")
    zvi = matmul(h, w_vi, _epi_plain, [], BF16, "proj_dsa_vi")
    zs = matmul(h, w_s, _epi_plain, [], F32, "proj_small")
    gates = matmul(h, w_branch_gate.astype(BF16), _epi_sigmoid_bias,
                   [(b_branch_gate.reshape(1, -1).astype(F32), "row")], BF16, "branch_gates")

    q_hm = zg[:, o_gq:o_gk].reshape(b, s, GLA_HEADS, GLA_DK).transpose(0, 2, 1, 3)
    k_hm = zg[:, o_gk:o_gv].reshape(b, s, GLA_HEADS, GLA_DK).transpose(0, 2, 1, 3)
    wa = gla_w_alpha.reshape(GLA_GATE_RANK, GLA_HEADS, GLA_DK).transpose(1, 0, 2).astype(BF16)
    ba = gla_b_alpha.reshape(GLA_HEADS, 1, GLA_DK).astype(F32)
    o_a = gla(q_hm, k_hm, zg, zs, wa, ba, gla_norm_g.reshape(1, GLA_DV).astype(F32), b, s)

    iq_hm = zvi[:, DSA_W:].reshape(b, s, IDX_HEADS, IDX_DIM).transpose(0, 2, 1, 3)
    ik = zs[:, LANES - IDX_DIM:].astype(BF16).reshape(b, s, IDX_DIM)
    w_hm = zs[:, GLA_GATE_RANK:GLA_GATE_RANK + IDX_HEADS].reshape(b, s, IDX_HEADS).transpose(0, 2, 1)
    mask = dsa_index(iq_hm, w_hm, ik, b, s)
    nbd, nbp = _near_bias_tables(rel_bias, min(ATT_T, s))
    vt = zvi[:, :DSA_W].reshape(b, s, DSA_HEADS, DSA_HD).transpose(0, 2, 3, 1)
    vt = jnp.concatenate([vt, jnp.ones((b, DSA_HEADS, ATT_VROWS - DSA_HD, s), BF16)], axis=2)
    vt = vt.reshape(b, DSA_HEADS * ATT_VROWS, s)
    o_b = dsa_attention(zqk, vt, mask, nbd, nbp, b, s).transpose(0, 2, 1).reshape(t, DSA_W)

    m = mix(o_a, o_b, w_out_gla.astype(BF16), w_out_dsa.astype(BF16), gates)
    x1 = matmul(m, w_out.astype(BF16), _epi_residual, [(x2d, "tile")], F32, "out_proj")

    w_r = jnp.concatenate([w_group_router, w_expert_router,
                           jnp.zeros((d, LANES - N_GROUPS - N_EXPERTS), F32)], axis=1).astype(BF16)
    b_r = jnp.concatenate([b_group_router, b_expert_router,
                           jnp.zeros((LANES - N_GROUPS - N_EXPERTS,), F32)]).reshape(1, LANES)
    hn, ri, rw = router(x1, norm_ffn_g, w_r, b_r)
    a = t * TOP_K_INNER
    eid = ri[:, :TOP_K_INNER].reshape(a)
    rank, counts = moe_rank(eid)
    padded = ((counts + MOE_BLK - 1) // MOE_BLK) * MOE_BLK
    pend = jnp.cumsum(padded)
    pstart = pend - padded
    dest = (pstart[eid] + rank).astype(I32)
    prow = a + N_EXPERTS * MOE_BLK
    nblk = prow // MOE_BLK
    blk_row0 = jnp.arange(nblk, dtype=I32)[:, None] * MOE_BLK
    blk_expert = jnp.minimum(jnp.sum((pend[None, :] <= blk_row0).astype(I32), axis=1), N_EXPERTS - 1)
    nvalid = (pend[-1:] // MOE_BLK).astype(I32)
    xs = moe_dispatch(dest, hn, jnp.zeros((prow, d), F32))
    y = moe_experts(blk_expert, nvalid, xs, w_exp_gate, w_exp_up, w_exp_down)
    x2 = moe_combine(dest, x1, rw, y)

    hp = rmsnorm(x2, norm_ple_g, BF16)
    out = matmul(hp, w_ple_gate.astype(BF16), _epi_ple,
                 [(b_ple_gate.reshape(1, -1).astype(F32), "row"), (x2, "tile"),
                  (p.reshape(t, -1).astype(BF16), "mk"), (w_ple_proj.astype(BF16), "kn")],
                 F32, "ple")
    return out.reshape(b, s, d)


def kernel(x, p, norm_mix_g, w_in, gla_w_alpha, gla_b_alpha, gla_norm_g, w_out_gla, q_norm_g, k_norm_g, rel_bias, w_out_dsa, w_branch_gate, b_branch_gate, w_out, norm_ffn_g, w_group_router, b_group_router, w_expert_router, b_expert_router, w_exp_gate, w_exp_up, w_exp_down, norm_ple_g, w_ple_gate, b_ple_gate, w_ple_proj):
    depth = w_in.shape[0]
    for i in range(depth):
        x = _layer(x, p[i], norm_mix_g[i], w_in[i], gla_w_alpha[i], gla_b_alpha[i], gla_norm_g[i], w_out_gla[i],
                   q_norm_g[i], k_norm_g[i], rel_bias, w_out_dsa[i], w_branch_gate[i], b_branch_gate[i], w_out[i],
                   norm_ffn_g[i], w_group_router[i], b_group_router[i], w_expert_router[i], b_expert_router[i],
                   w_exp_gate[i], w_exp_up[i], w_exp_down[i], norm_ple_g[i], w_ple_gate[i], b_ple_gate[i],
                   w_ple_proj[i])
    return x
```

```python
import functools
import math

import jax
import jax.numpy as jnp
from jax import lax
from jax.experimental import pallas as pl
from jax.experimental.pallas import tpu as pltpu

F32 = jnp.float32
BF16 = jnp.bfloat16
I32 = jnp.int32
I16 = jnp.int16

EPS = 1e-6
CHUNK = 64
GLA_HEADS = 8
GLA_DK = 64
GLA_DV = 128
GLA_GATE_RANK = 16
GLA_GATE_TEMP = 16.0
GLA_QK = GLA_HEADS * GLA_DK
GLA_V = GLA_HEADS * GLA_DV
DSA_HEADS = 8
DSA_HD = 128
DSA_W = DSA_HEADS * DSA_HD
IDX_HEADS = 16
IDX_DIM = 64
IDX_Q = IDX_HEADS * IDX_DIM
INDEX_TOPK = 256
REL_BUCKETS = 32
REL_MAX_DIST = 128
N_GROUPS = 8
EXPERTS_PER_GROUP = 8
N_EXPERTS = N_GROUPS * EXPERTS_PER_GROUP
EXPERT_FF = 512
TOP_K_INNER = 2

LANES = 128
VMEM_LIMIT = 56 * 1024 * 1024
NEG_BIG = -1e30
LOG2E = math.log2(math.e)
KEY_NEG_INF = -2139095041
I16_MIN, I16_MAX = -32768, 32767
PACK16 = 16

GLA_CT = 512
GLA_HEADS_PER_STEP = 2
IDX_QG = 256
ATT_T = 512
ATT_VROWS = DSA_HD + PACK16
MOE_BLK = 256
RANK_RB = 1024
ROW_T = 256
MM_SPLIT = 4


def _cp(sem, **kw):
    return pltpu.CompilerParams(dimension_semantics=sem, vmem_limit_bytes=VMEM_LIMIT, **kw)


def _dot(a, b):
    return jnp.dot(a, b, preferred_element_type=F32)


def _dot_nt(a, b):
    return lax.dot_general(a, b, (((1,), (1,)), ((), ())), preferred_element_type=F32)


def _dot_tn(a, b):
    return lax.dot_general(a, b, (((0,), (0,)), ((), ())), preferred_element_type=F32)


def _sigmoid(x):
    return 1.0 / (1.0 + jnp.exp(-x))


def _rmsnorm_kernel(x_ref, g_ref, o_ref):
    x = x_ref[...]
    ms = jnp.mean(x * x, axis=-1, keepdims=True)
    o_ref[...] = (x * lax.rsqrt(ms + EPS) * g_ref[...]).astype(o_ref.dtype)


def rmsnorm(x, g, out_dtype, tm=512):
    m, d = x.shape
    tm = min(tm, m)
    return pl.pallas_call(
        _rmsnorm_kernel,
        grid=(m // tm,),
        in_specs=[pl.BlockSpec((tm, d), lambda i: (i, 0)), pl.BlockSpec((1, d), lambda i: (0, 0))],
        out_specs=pl.BlockSpec((tm, d), lambda i: (i, 0)),
        out_shape=jax.ShapeDtypeStruct((m, d), out_dtype),
        compiler_params=_cp(("parallel",)),
        name="rmsnorm",
    )(x, g.reshape(1, d).astype(F32))


def _row_chunks(tm):
    rc = tm // MM_SPLIT if tm % (MM_SPLIT * PACK16) == 0 else tm
    return [slice(r, r + rc) for r in range(0, tm, rc)]


def _mm_kernel(*refs, n_extra, epilogue):
    a_ref, w_ref = refs[0], refs[1]
    extras = refs[2:2 + n_extra]
    o_ref = refs[2 + n_extra]
    chunks = _row_chunks(a_ref.shape[0])
    acc = _dot(a_ref[chunks[0], :], w_ref[...])
    for c, rows in enumerate(chunks):
        nxt = _dot(a_ref[chunks[c + 1], :], w_ref[...]) if c + 1 < len(chunks) else None
        epilogue(acc, rows, o_ref, *extras)
        acc = nxt


def matmul(a, w, epilogue, extras, out_dtype, name, tm=1024, tn=512):
    m, k = a.shape
    n = w.shape[1]
    tm, tn = min(tm, m), min(tn, n)
    specs = [pl.BlockSpec((tm, k), lambda i, j: (i, 0)), pl.BlockSpec((k, tn), lambda i, j: (0, j))]
    args = [a, w]
    for arr, kind in extras:
        if kind == "tile":
            specs.append(pl.BlockSpec((tm, tn), lambda i, j: (i, j)))
        elif kind == "row":
            specs.append(pl.BlockSpec((1, tn), lambda i, j: (0, j)))
        elif kind == "mk":
            specs.append(pl.BlockSpec((tm, arr.shape[1]), lambda i, j: (i, 0)))
        elif kind == "kn":
            specs.append(pl.BlockSpec((arr.shape[0], tn), lambda i, j: (0, j)))
        else:
            raise ValueError(kind)
        args.append(arr)
    return pl.pallas_call(
        functools.partial(_mm_kernel, n_extra=len(extras), epilogue=epilogue),
        grid=(m // tm, n // tn),
        in_specs=specs,
        out_specs=pl.BlockSpec((tm, tn), lambda i, j: (i, j)),
        out_shape=jax.ShapeDtypeStruct((m, n), out_dtype),
        compiler_params=_cp(("parallel", "arbitrary")),
        name=name,
    )(*args)


def _epi_plain(acc, rows, o_ref):
    o_ref[rows, :] = acc.astype(o_ref.dtype)


def _epi_headnorm(acc, rows, o_ref, g_ref):
    for c in range(acc.shape[1] // DSA_HD):
        sl = slice(c * DSA_HD, (c + 1) * DSA_HD)
        blk = acc[:, sl]
        ms = jnp.mean(blk * blk, axis=-1, keepdims=True)
        o_ref[rows, sl] = (blk * lax.rsqrt(ms + EPS) * g_ref[:, sl]).astype(o_ref.dtype)


def _epi_sigmoid_bias(acc, rows, o_ref, b_ref):
    o_ref[rows, :] = _sigmoid(acc + b_ref[...]).astype(o_ref.dtype)


def _epi_residual(acc, rows, o_ref, x_ref):
    o_ref[rows, :] = (x_ref[rows, :] + acc).astype(o_ref.dtype)


def _epi_ple(acc, rows, o_ref, b_ref, x_ref, p_ref, wp_ref):
    ple = _dot(p_ref[rows, :], wp_ref[...])
    o_ref[rows, :] = (x_ref[rows, :] + _sigmoid(acc + b_ref[...]) * ple).astype(o_ref.dtype)


def _mix_kernel(oa_ref, ob_ref, wa_ref, wb_ref, ga_ref, gb_ref, o_ref):
    chunks = _row_chunks(oa_ref.shape[0])

    def both(rows):
        return _dot(oa_ref[rows, :], wa_ref[...]), _dot(ob_ref[rows, :], wb_ref[...])

    ya, yb = both(chunks[0])
    for c, rows in enumerate(chunks):
        nxt = both(chunks[c + 1]) if c + 1 < len(chunks) else None
        o_ref[rows, :] = (ga_ref[rows, :].astype(F32) * ya + gb_ref[rows, :].astype(F32) * yb).astype(o_ref.dtype)
        if nxt is not None:
            ya, yb = nxt


def mix(oa, ob, wa, wb, gates, tm=1024, tn=512):
    m, k = oa.shape
    n = wa.shape[1]
    tm, tn = min(tm, m), min(tn, n)
    nj = n // tn
    return pl.pallas_call(
        _mix_kernel,
        grid=(m // tm, nj),
        in_specs=[
            pl.BlockSpec((tm, k), lambda i, j: (i, 0)),
            pl.BlockSpec((tm, k), lambda i, j: (i, 0)),
            pl.BlockSpec((k, tn), lambda i, j: (0, j)),
            pl.BlockSpec((k, tn), lambda i, j: (0, j)),
            pl.BlockSpec((tm, tn), lambda i, j: (i, j)),
            pl.BlockSpec((tm, tn), lambda i, j: (i, j + nj)),
        ],
        out_specs=pl.BlockSpec((tm, tn), lambda i, j: (i, j)),
        out_shape=jax.ShapeDtypeStruct((m, n), BF16),
        compiler_params=_cp(("parallel", "arbitrary")),
        name="mix",
    )(oa, ob, wa, wb, gates, gates)


def _gla_kernel(q_ref, k_ref, v_ref, gr_ref, zs_ref, wa_ref, ba_ref, gn_ref, o_ref, st_ref, *, nchunk):
    @pl.when(pl.program_id(2) == 0)
    def _():
        st_ref[...] = jnp.zeros_like(st_ref)

    glr = zs_ref[:, 0:GLA_GATE_RANK].astype(BF16)
    row = lax.broadcasted_iota(I32, (CHUNK, CHUNK), 0)
    col = lax.broadcasted_iota(I32, (CHUNK, CHUNK), 1)
    causal = row >= col
    tril = jnp.where(causal, 1.0, 0.0).astype(BF16)
    gn = gn_ref[...]
    mid = CHUNK // 2 - 1
    nh = q_ref.shape[1] // GLA_DK
    sls = [slice(c * CHUNK, (c + 1) * CHUNK) for c in range(nchunk)]
    items = [(hh, c) for hh in range(nh) for c in range(nchunk)]

    def kcols(hh):
        return slice(hh * GLA_DK, (hh + 1) * GLA_DK)

    def vcols(hh):
        return slice(hh * GLA_DV, (hh + 1) * GLA_DV)

    g_hi, g_lo = [], []
    for hh in range(nh):
        logit = _dot(glr, wa_ref[hh]) + ba_ref[hh]
        g = (jnp.minimum(logit, 0.0) - jnp.log(1.0 + jnp.exp(-jnp.abs(logit)))) * (1.0 / GLA_GATE_TEMP)
        hi = g.astype(BF16)
        g_hi.append(hi)
        g_lo.append((g - hi.astype(F32)).astype(BF16))
    cums = {(hh, c): _dot(tril, g_hi[hh][sls[c]]) + _dot(tril, g_lo[hh][sls[c]]) for hh, c in items}
    qts, kts, qgs, kds, decs = {}, {}, {}, {}, {}
    for hh, c in items:
        cum = cums[hh, c]
        c0 = cum[mid:mid + 1, :]
        gl = cum[CHUNK - 1:CHUNK, :]
        qt = q_ref[sls[c], kcols(hh)].astype(F32) * (GLA_DK ** -0.5) * jnp.exp(cum - c0)
        kt = k_ref[sls[c], kcols(hh)].astype(F32) * jnp.exp(c0 - cum)
        qts[hh, c] = qt.astype(BF16)
        kts[hh, c] = kt.astype(BF16)
        qgs[hh, c] = (qt * jnp.exp(c0)).astype(BF16)
        kds[hh, c] = (kt * jnp.exp(gl - c0)).astype(BF16)
        decs[hh, c] = jnp.exp(gl)
    scs = {it: _dot_nt(qts[it], kts[it]) for it in items}
    kvs = {(hh, c): _dot_tn(v_ref[sls[c], vcols(hh)], kds[hh, c]) for hh, c in items}
    intra = {(hh, c): _dot(jnp.where(causal, scs[hh, c], 0.0).astype(BF16), v_ref[sls[c], vcols(hh)])
             for hh, c in items}
    sts = {}
    for hh in range(nh):
        st = st_ref[hh]
        for c in range(nchunk):
            sts[hh, c] = st.astype(BF16)
            st = st * decs[hh, c] + kvs[hh, c]
        st_ref[hh] = st
    inter = {it: _dot_nt(qgs[it], sts[it]) for it in items}
    for hh, c in items:
        o = intra[hh, c] + inter[hh, c]
        ms = jnp.mean(o * o, axis=-1, keepdims=True)
        gr = gr_ref[sls[c], vcols(hh)].astype(F32)
        o_ref[sls[c], vcols(hh)] = (o * lax.rsqrt(ms + EPS) * gn * (gr * _sigmoid(gr))).astype(o_ref.dtype)


def gla(zg, zs, wa, ba, gn, b, s):
    ct = min(GLA_CT, s)
    ns = s // ct
    hps = GLA_HEADS_PER_STEP
    kw, vw = hps * GLA_DK, hps * GLA_DV
    k_blk0 = GLA_QK // kw
    v_blk0 = (2 * GLA_QK) // vw
    gr_blk0 = (2 * GLA_QK + GLA_V) // vw
    return pl.pallas_call(
        functools.partial(_gla_kernel, nchunk=ct // CHUNK),
        grid=(b, GLA_HEADS // hps, ns),
        in_specs=[
            pl.BlockSpec((ct, kw), lambda bi, hp, si: (bi * ns + si, hp)),
            pl.BlockSpec((ct, kw), lambda bi, hp, si: (bi * ns + si, k_blk0 + hp)),
            pl.BlockSpec((ct, vw), lambda bi, hp, si: (bi * ns + si, v_blk0 + hp)),
            pl.BlockSpec((ct, vw), lambda bi, hp, si: (bi * ns + si, gr_blk0 + hp)),
            pl.BlockSpec((ct, LANES), lambda bi, hp, si: (bi * ns + si, 0)),
            pl.BlockSpec((hps, GLA_GATE_RANK, GLA_DK), lambda bi, hp, si: (hp, 0, 0)),
            pl.BlockSpec((hps, 1, GLA_DK), lambda bi, hp, si: (hp, 0, 0)),
            pl.BlockSpec((1, GLA_DV), lambda bi, hp, si: (0, 0)),
        ],
        out_specs=pl.BlockSpec((ct, vw), lambda bi, hp, si: (bi * ns + si, hp)),
        out_shape=jax.ShapeDtypeStruct((b * s, GLA_V), BF16),
        scratch_shapes=[pltpu.VMEM((hps, GLA_DV, GLA_DK), F32)],
        compiler_params=_cp(("parallel", "parallel", "arbitrary")),
        name="gla",
    )(zg, zg, zg, zg, zs, wa, ba, gn)


def _order_key(x):
    bits = lax.bitcast_convert_type(x, I32)
    return bits ^ ((bits >> 31) & 0x7FFFFFFF)


def _idx_kernel(iq_ref, w_ref, ik_ref, mask_ref, hi_ref, lo_ref, seg_ref, iqh_ref, *, qg, s):
    g = pl.program_id(1)
    nt = g + 1
    w = w_ref[...] * ((IDX_HEADS ** -0.5) * (IDX_DIM ** -0.5))
    kchunk = lax.broadcasted_iota(I32, (qg, qg), 0) // CHUNK
    qchunk = lax.broadcasted_iota(I32, (qg, qg), 1) // CHUNK
    adm_diag = kchunk <= qchunk

    half = qg // 2
    for j in range(IDX_HEADS):
        iqh_ref[j] = iq_ref[:, j * IDX_DIM:(j + 1) * IDX_DIM]

    def score_tile(t, carry):
        for hf in range(2):
            off = pl.multiple_of(t * qg + hf * half, half)
            ikt = ik_ref[pl.ds(off, half), :]
            acc = jnp.zeros((half, qg), F32)
            for j in range(IDX_HEADS):
                acc = acc + jnp.maximum(_dot_nt(ikt, iqh_ref[j]), 0.0) * w[j:j + 1, :]
            acc = jnp.where(jnp.logical_or(t < g, adm_diag[hf * half:(hf + 1) * half, :]), acc, -jnp.inf)
            key = _order_key(acc)
            hi_ref[pl.ds(off, half), :] = (key >> 16).astype(I16)
            lo_ref[pl.ds(off, half), :] = ((key & 0xFFFF) + I16_MIN).astype(I16)
            rows = slice(hf * half, (hf + 1) * half)
            seg_ref[rows, :] = jnp.maximum(seg_ref[rows, :], acc)
        return carry

    seg_ref[...] = jnp.full((qg, qg), -jnp.inf, F32)
    lax.fori_loop(0, nt, score_tile, 0)
    segmax = seg_ref[...]

    topk = min(INDEX_TOPK, s // 4)
    kf = float(topk)
    lb = _order_key(jnp.min(segmax, axis=0, keepdims=True))
    mx = _order_key(jnp.max(segmax, axis=0, keepdims=True))
    qcol = lax.broadcasted_iota(I32, (1, qg), 1)
    n_adm = g * qg + (qcol // CHUNK + 1) * CHUNK
    small = n_adm <= topk

    def count_ge(ref, cand):
        c16 = cand.astype(I16)

        def body(t, acc):
            off = pl.multiple_of(t * qg, qg)
            m = jnp.where(ref[pl.ds(off, qg), :] >= c16, jnp.int16(1), jnp.int16(0))
            for r in range(qg // PACK16):
                acc = acc + m[r * PACK16:(r + 1) * PACK16, :]
            return acc

        acc = lax.fori_loop(0, nt, body, jnp.zeros((PACK16, qg), I16))
        return jnp.sum(acc.astype(F32), axis=0, keepdims=True)

    def bisect(ref, lo0, hi0, target, npass):
        def body(_, c):
            lo, hi, c_hi = c
            mid = jnp.right_shift(lo + hi, 1)
            cnt = count_ge(ref, mid)
            up = cnt >= target
            return jnp.where(up, mid, lo), jnp.where(up, hi, mid), jnp.where(up, c_hi, cnt)

        lo, _, c_hi = lax.fori_loop(0, npass, body, (lo0, hi0, jnp.zeros((1, qg), F32)))
        return lo, c_hi

    a_lo0 = jnp.right_shift(lb, 16)
    a_hi0 = jnp.right_shift(mx, 16) + 1
    width = jnp.where(small, 1, a_hi0 - a_lo0)
    nbits = jnp.zeros((1, qg), F32)
    for b in range(17):
        nbits = nbits + jnp.where(width > (1 << b), 1.0, 0.0)
    a_lo, ca_hi = bisect(hi_ref, a_lo0, a_hi0, kf, jnp.max(nbits).astype(I32))
    a16 = jnp.where(small, I16_MAX, a_lo).astype(I16)

    def keep_equal_hi(t, carry):
        off = pl.multiple_of(t * qg, qg)
        lo_ref[pl.ds(off, qg), :] = jnp.where(hi_ref[pl.ds(off, qg), :] == a16, lo_ref[pl.ds(off, qg), :],
                                              jnp.int16(I16_MIN))
        return carry

    lax.fori_loop(0, nt, keep_equal_hi, 0)

    b_lo, _ = bisect(lo_ref, jnp.full((1, qg), I16_MIN, I32), jnp.full((1, qg), I16_MAX + 1, I32),
                     kf - ca_hi, 16)
    whole = b_lo == I16_MIN
    thr_h = jnp.where(small, jnp.int32(KEY_NEG_INF >> 16), jnp.where(whole, a_lo - 1, a_lo)).astype(I16)
    thr_l = jnp.where(jnp.logical_or(small, whole), I16_MAX, b_lo).astype(I16)

    def write_tile(t, carry):
        off = pl.multiple_of(t * qg, qg)
        zero = jnp.zeros((qg, qg), BF16)
        low = jnp.where(lo_ref[pl.ds(off, qg), :] >= thr_l, zero, jnp.full((qg, qg), NEG_BIG, BF16))
        mask_ref[pl.ds(off, qg), :] = jnp.where(hi_ref[pl.ds(off, qg), :] > thr_h, zero, low)
        return carry

    lax.fori_loop(0, nt, write_tile, 0)

    def fill_tile(t, carry):
        off = pl.multiple_of(t * qg, qg)
        mask_ref[pl.ds(off, qg), :] = jnp.full((qg, qg), NEG_BIG, BF16)
        return carry

    lax.fori_loop(nt, s // qg, fill_tile, 0)


def dsa_index(zvi, w_hm, ik, b, s):
    qg = min(IDX_QG, s)
    ng = s // qg
    return pl.pallas_call(
        functools.partial(_idx_kernel, qg=qg, s=s),
        grid=(b, ng),
        in_specs=[
            pl.BlockSpec((qg, IDX_Q), lambda bi, gi: (bi * ng + gi, DSA_W // IDX_Q)),
            pl.BlockSpec((None, IDX_HEADS, qg), lambda bi, gi: (bi, 0, gi)),
            pl.BlockSpec((None, s, IDX_DIM), lambda bi, gi: (bi, 0, 0)),
        ],
        out_specs=pl.BlockSpec((None, s, qg), lambda bi, gi: (bi, 0, gi)),
        out_shape=jax.ShapeDtypeStruct((b, s, s), BF16),
        scratch_shapes=[pltpu.VMEM((s, qg), I16), pltpu.VMEM((s, qg), I16), pltpu.VMEM((qg, qg), F32),
                        pltpu.VMEM((IDX_HEADS, qg, IDX_DIM), BF16)],
        compiler_params=_cp(("parallel", "arbitrary")),
        name="dsa_index",
    )(zvi, w_hm, ik)


def _attn_kernel(qi_ref, ki_ref, q_ref, k_ref, vt_ref, mask_ref, nbd_ref, nbp_ref, o_ref, m_ref, acc_ref):
    p = pl.program_id(1)
    qi = qi_ref[p]
    ki = ki_ref[p]

    @pl.when(ki == 0)
    def _():
        m_ref[...] = jnp.full_like(m_ref, NEG_BIG)
        acc_ref[...] = jnp.zeros_like(acc_ref)

    def hs(h):
        return slice(h * DSA_HD, (h + 1) * DSA_HD)

    def vs(h):
        return slice(h * ATT_VROWS, (h + 1) * ATT_VROWS)

    nrow = 2
    rch = q_ref.shape[0] // nrow
    kh = k_ref.shape[0] // 2

    def heads(bias_of_head):
        def score_chunk(h, r):
            return _dot_nt(k_ref[r * rch:(r + 1) * rch, hs(h)], q_ref[:, hs(h)])

        def value_half(h, pr, half):
            return _dot(vt_ref[vs(h), half * kh:(half + 1) * kh], pr)

        sc = {0: [score_chunk(0, r) for r in range(nrow)], 1: [score_chunk(1, r) for r in range(nrow)]}
        prev = None
        for h in range(DSA_HEADS + 1):
            if h < DSA_HEADS:
                bias = bias_of_head(h)
                cur = sc.pop(h)
                sb, cm = [], []
                nxt = []
                for r in range(nrow):
                    sb.append(cur[r] + bias[r * rch:(r + 1) * rch, :])
                    cm.append(jnp.max(sb[r], axis=0, keepdims=True))
                    if h + 2 < DSA_HEADS:
                        nxt.append(score_chunk(h + 2, r))
                if nxt:
                    sc[h + 2] = nxt
                m_old = m_ref[h]
                m_new = m_old
                for c in cm:
                    m_new = jnp.maximum(m_new, c)
                alpha = jnp.exp2(m_old - m_new)
                m_ref[h] = m_new
            pv = []
            pr = []
            for r in range(nrow):
                if h < DSA_HEADS:
                    pr.append(jnp.exp2(sb[r] - m_new).astype(BF16))
                if prev is not None and r % (nrow // 2) == 0:
                    pv.append(value_half(prev[0], prev[2][r // (nrow // 2)], r // (nrow // 2)))
            if prev is not None:
                ph = prev[0]
                acc_ref[vs(ph), :] = prev[1] * acc_ref[vs(ph), :] + pv[0] + pv[1]
            if h < DSA_HEADS:
                prev = (h, alpha, [jnp.concatenate(pr[:nrow // 2], axis=0), jnp.concatenate(pr[nrow // 2:], axis=0)])

    @pl.when(ki < qi - 1)
    def _():
        mb = mask_ref[...].astype(F32)
        heads(lambda h: mb)

    @pl.when(ki == qi - 1)
    def _():
        mb = mask_ref[...].astype(F32)
        heads(lambda h: mb + nbp_ref[h].astype(F32))

    @pl.when(ki == qi)
    def _():
        mb = mask_ref[...].astype(F32)
        heads(lambda h: mb + nbd_ref[h].astype(F32))
        for h in range(DSA_HEADS):
            num = acc_ref[h * ATT_VROWS:h * ATT_VROWS + DSA_HD, :]
            den = acc_ref[h * ATT_VROWS + DSA_HD:h * ATT_VROWS + DSA_HD + 1, :]
            o_ref[hs(h), :] = (num / den).astype(o_ref.dtype)


def dsa_attention(zqk, vt, mask, nbd, nbp, b, s):
    t = min(ATT_T, s)
    ng = s // t
    pairs = [(qi, ki) for qi in range(ng) for ki in range(qi + 1)]
    qi_tab = jnp.asarray([p[0] for p in pairs], I32)
    ki_tab = jnp.asarray([p[1] for p in pairs], I32)
    grid_spec = pltpu.PrefetchScalarGridSpec(
        num_scalar_prefetch=2,
        grid=(b, len(pairs)),
        in_specs=[
            pl.BlockSpec((t, DSA_W), lambda bi, p, qt, kt: (bi * ng + qt[p], 0)),
            pl.BlockSpec((t, DSA_W), lambda bi, p, qt, kt: (bi * ng + kt[p], 1)),
            pl.BlockSpec((None, DSA_HEADS * ATT_VROWS, t), lambda bi, p, qt, kt: (bi, 0, kt[p])),
            pl.BlockSpec((None, t, t), lambda bi, p, qt, kt: (bi, kt[p], qt[p])),
            pl.BlockSpec((DSA_HEADS, t, t), lambda bi, p, qt, kt: (0, 0, 0)),
            pl.BlockSpec((DSA_HEADS, t, t), lambda bi, p, qt, kt: (0, 0, 0)),
        ],
        out_specs=pl.BlockSpec((None, DSA_W, t), lambda bi, p, qt, kt: (bi, 0, qt[p])),
        scratch_shapes=[
            pltpu.VMEM((DSA_HEADS, 1, t), F32),
            pltpu.VMEM((DSA_HEADS * ATT_VROWS, t), F32),
        ],
    )
    return pl.pallas_call(
        _attn_kernel,
        grid_spec=grid_spec,
        out_shape=jax.ShapeDtypeStruct((b, DSA_W, s), BF16),
        compiler_params=_cp(("parallel", "arbitrary")),
        name="dsa_attention",
    )(qi_tab, ki_tab, zqk, zqk, vt, mask, nbd, nbp)


def _t5_bucket(rel):
    half = REL_BUCKETS // 2
    exact = half // 2
    sign = jnp.where(rel > 0, half, 0)
    n = jnp.abs(rel)
    nf = jnp.maximum(n, 1).astype(F32)
    large = exact + (jnp.log(nf / exact) / math.log(REL_MAX_DIST / exact) * (half - exact)).astype(I32)
    large = jnp.minimum(large, half - 1)
    return sign + jnp.where(n < exact, n, large)


def _near_bias_tables(rel_bias, t):
    kpos = jnp.arange(t, dtype=I32)[:, None]
    qpos = jnp.arange(t, dtype=I32)[None, :]
    shifted = (rel_bias - rel_bias[REL_BUCKETS // 2 - 1]) * LOG2E

    def table(rel):
        onehot = (_t5_bucket(rel)[:, :, None] == jnp.arange(REL_BUCKETS, dtype=I32)).astype(F32)
        return jnp.einsum("kqb,bh->hkq", onehot, shifted, precision=lax.Precision.HIGHEST).astype(BF16)

    return table(kpos - qpos), table(kpos - t - qpos)


def _router_kernel(x_ref, g_ref, w_ref, b_ref, hn_ref, ri_ref, rw_ref):
    x = x_ref[...]
    ms = jnp.mean(x * x, axis=-1, keepdims=True)
    hn = x * lax.rsqrt(ms + EPS) * g_ref[...]
    hn_ref[...] = hn
    logits = _dot(hn.astype(BF16), w_ref[...]) + b_ref[...]
    lane = lax.broadcasted_iota(I32, logits.shape, 1)
    lanef = lane.astype(F32)
    big = 1e9
    is_grp = lane < N_GROUPS
    gl = jnp.where(is_grp, logits, -jnp.inf)
    gmax = jnp.max(gl, axis=-1, keepdims=True)
    grp = jnp.min(jnp.where(gl == gmax, lanef, big), axis=-1, keepdims=True)
    gsum = jnp.sum(jnp.where(is_grp, jnp.exp(logits - gmax), 0.0), axis=-1, keepdims=True)
    grp_w = 1.0 / gsum
    elo = N_GROUPS + grp * EXPERTS_PER_GROUP
    in_grp = jnp.logical_and(lanef >= elo, lanef < elo + EXPERTS_PER_GROUP)
    el = jnp.where(in_grp, logits, -jnp.inf)
    v1 = jnp.max(el, axis=-1, keepdims=True)
    i1 = jnp.min(jnp.where(el == v1, lanef, big), axis=-1, keepdims=True)
    el2 = jnp.where(lanef == i1, -jnp.inf, el)
    v2 = jnp.max(el2, axis=-1, keepdims=True)
    i2 = jnp.min(jnp.where(el2 == v2, lanef, big), axis=-1, keepdims=True)
    e2 = jnp.exp(v2 - v1)
    w1 = grp_w * (1.0 / (1.0 + e2))
    w2 = grp_w * (e2 / (1.0 + e2))
    ri_ref[...] = jnp.where(lane == 0, i1 - N_GROUPS, jnp.where(lane == 1, i2 - N_GROUPS, 0.0)).astype(I32)
    rw_ref[...] = jnp.where(lane == 0, w1, jnp.where(lane == 1, w2, 0.0))


def router(x1, g, w_r, b_r, tm=256):
    m, d = x1.shape
    tm = min(tm, m)
    return pl.pallas_call(
        _router_kernel,
        grid=(m // tm,),
        in_specs=[
            pl.BlockSpec((tm, d), lambda i: (i, 0)),
            pl.BlockSpec((1, d), lambda i: (0, 0)),
            pl.BlockSpec((d, LANES), lambda i: (0, 0)),
            pl.BlockSpec((1, LANES), lambda i: (0, 0)),
        ],
        out_specs=[
            pl.BlockSpec((tm, d), lambda i: (i, 0)),
            pl.BlockSpec((tm, LANES), lambda i: (i, 0)),
            pl.BlockSpec((tm, LANES), lambda i: (i, 0)),
        ],
        out_shape=[
            jax.ShapeDtypeStruct((m, d), F32),
            jax.ShapeDtypeStruct((m, LANES), I32),
            jax.ShapeDtypeStruct((m, LANES), F32),
        ],
        compiler_params=_cp(("parallel",)),
        name="router",
    )(x1, g.reshape(1, d).astype(F32), w_r, b_r)


def _rank_kernel(e_ref, rank_ref, cnt_ref, run_ref, *, rb):
    @pl.when(pl.program_id(0) == 0)
    def _():
        run_ref[...] = jnp.zeros_like(run_ref)

    e = e_ref[...]
    sub = lax.broadcasted_iota(I32, (N_EXPERTS, rb), 0)
    hit = sub == e
    oh = jnp.where(hit, 1.0, 0.0)
    a = lax.broadcasted_iota(I32, (rb, rb), 0)
    bcol = lax.broadcasted_iota(I32, (rb, rb), 1)
    upper = jnp.where(a < bcol, 1.0, 0.0).astype(BF16)
    prefix = _dot(oh.astype(BF16), upper)
    run = run_ref[...]
    rank = jnp.sum(jnp.where(hit, prefix + run[:, 0:1], 0.0), axis=0, keepdims=True)
    rank_ref[...] = rank.astype(I32)
    new = run + jnp.sum(oh, axis=1, keepdims=True)
    run_ref[...] = new
    cnt_ref[...] = new


def moe_rank(eid_flat):
    a = eid_flat.shape[0]
    rb = min(RANK_RB, a)
    nb = a // rb
    rank, cnt = pl.pallas_call(
        functools.partial(_rank_kernel, rb=rb),
        grid=(nb,),
        in_specs=[pl.BlockSpec((None, 1, rb), lambda i: (i, 0, 0))],
        out_specs=[
            pl.BlockSpec((None, 1, rb), lambda i: (i, 0, 0)),
            pl.BlockSpec((N_EXPERTS, LANES), lambda i: (0, 0)),
        ],
        out_shape=[
            jax.ShapeDtypeStruct((nb, 1, rb), I32),
            jax.ShapeDtypeStruct((N_EXPERTS, LANES), F32),
        ],
        scratch_shapes=[pltpu.VMEM((N_EXPERTS, LANES), F32)],
        compiler_params=_cp(("arbitrary",)),
        name="moe_rank",
    )(eid_flat.reshape(nb, 1, rb))
    return rank.reshape(a), cnt[:, 0].astype(I32)


def _row_copy(src, dst, sem):
    return pltpu.make_async_copy(src, dst, sem)


def _dispatch_kernel(dest_ref, hn_ref, xs_in_ref, xs_ref, sem, *, tm):
    del xs_in_ref
    base = pl.program_id(0) * (tm * TOP_K_INNER)

    def body(r, carry):
        for slot in range(TOP_K_INNER):
            d = dest_ref[base + r * TOP_K_INNER + slot]
            _row_copy(hn_ref.at[pl.ds(r, 1), :], xs_ref.at[pl.ds(d, 1), :], sem).start()
        return carry

    lax.fori_loop(0, tm, body, 0, unroll=8)
    for _ in range(TOP_K_INNER):
        _row_copy(hn_ref, xs_ref.at[pl.ds(0, tm), :], sem).wait()


def moe_dispatch(dest, hn, xs0):
    m, d = hn.shape
    tm = min(ROW_T, m)
    grid_spec = pltpu.PrefetchScalarGridSpec(
        num_scalar_prefetch=1,
        grid=(m // tm,),
        in_specs=[
            pl.BlockSpec((tm, d), lambda i, dst: (i, 0)),
            pl.BlockSpec(memory_space=pl.ANY),
        ],
        out_specs=pl.BlockSpec(memory_space=pl.ANY),
        scratch_shapes=[pltpu.SemaphoreType.DMA(())],
    )
    return pl.pallas_call(
        functools.partial(_dispatch_kernel, tm=tm),
        grid_spec=grid_spec,
        out_shape=jax.ShapeDtypeStruct(xs0.shape, xs0.dtype),
        input_output_aliases={2: 0},
        compiler_params=_cp(("arbitrary",), has_side_effects=True),
        name="moe_dispatch",
    )(dest, hn, xs0)


def _expert_kernel(be_ref, nv_ref, run_ref, nxt_ref, xs_ref, wg_hbm, wu_hbm, wd_hbm, y_ref,
                   wg_f, wu_f, wd_f, wg_bf, wu_bf, wd_bf, sem):
    i = pl.program_id(0)
    valid = i < nv_ref[0]
    e = be_ref[i]
    fresh = jnp.logical_and(valid, jnp.logical_or(i == 0, e != be_ref[jnp.maximum(i - 1, 0)]))
    slot = run_ref[i] % 2

    def weight_copies(expert, s_):
        return [pltpu.make_async_copy(wg_hbm.at[expert], wg_f.at[s_], sem.at[s_, 0]),
                pltpu.make_async_copy(wu_hbm.at[expert], wu_f.at[s_], sem.at[s_, 1]),
                pltpu.make_async_copy(wd_hbm.at[expert], wd_f.at[s_], sem.at[s_, 2])]

    @pl.when(i == 0)
    def _():
        for c in weight_copies(e, 0):
            c.start()

    @pl.when(fresh)
    def _():
        nxt = nxt_ref[i]

        @pl.when(nxt >= 0)
        def _():
            for c in weight_copies(nxt, 1 - slot):
                c.start()

        for c in weight_copies(e, slot):
            c.wait()
        wg_bf[...] = wg_f[slot].astype(BF16)
        wu_bf[...] = wu_f[slot].astype(BF16)
        wd_bf[...] = wd_f[slot].astype(BF16)

    @pl.when(valid)
    def _():
        xb = xs_ref[...].astype(BF16)
        hg = _dot(xb, wg_bf[...])
        hu = _dot(xb, wu_bf[...])
        act = (hg * _sigmoid(hg) * hu).astype(BF16)
        y_ref[...] = _dot(act, wd_bf[...])

    @pl.when(jnp.logical_not(valid))
    def _():
        y_ref[...] = jnp.zeros_like(y_ref)


def moe_experts(blk_expert, nvalid, has_rows, xs, w_gate, w_up, w_down):
    p, d = xs.shape
    nblk = p // MOE_BLK
    ff = w_gate.shape[2]
    first = jnp.concatenate([jnp.ones((1,), bool), blk_expert[1:] != blk_expert[:-1]])
    run = (jnp.cumsum(first.astype(I32)) - 1).astype(I32)
    eids = jnp.arange(N_EXPERTS, dtype=I32)
    later = jnp.logical_and(eids[None, :] > eids[:, None], has_rows[None, :])
    nxt_tab = jnp.min(jnp.where(later, eids[None, :], N_EXPERTS), axis=1)
    nxt_tab = jnp.where(nxt_tab >= N_EXPERTS, -1, nxt_tab)
    nxt = jnp.sum(jnp.where(blk_expert[:, None] == eids[None, :], nxt_tab[None, :], 0), axis=1).astype(I32)

    def blk(i, nv):
        return jnp.minimum(i, nv[0] - 1)

    grid_spec = pltpu.PrefetchScalarGridSpec(
        num_scalar_prefetch=4,
        grid=(nblk,),
        in_specs=[
            pl.BlockSpec((MOE_BLK, d), lambda i, be, nv, rn, nx: (blk(i, nv), 0)),
            pl.BlockSpec(memory_space=pl.ANY),
            pl.BlockSpec(memory_space=pl.ANY),
            pl.BlockSpec(memory_space=pl.ANY),
        ],
        out_specs=pl.BlockSpec((MOE_BLK, d), lambda i, be, nv, rn, nx: (i, 0)),
        scratch_shapes=[
            pltpu.VMEM((2, d, ff), F32), pltpu.VMEM((2, d, ff), F32), pltpu.VMEM((2, ff, d), F32),
            pltpu.VMEM((d, ff), BF16), pltpu.VMEM((d, ff), BF16), pltpu.VMEM((ff, d), BF16),
            pltpu.SemaphoreType.DMA((2, 3)),
        ],
    )
    return pl.pallas_call(
        _expert_kernel,
        grid_spec=grid_spec,
        out_shape=jax.ShapeDtypeStruct((p, d), F32),
        compiler_params=_cp(("arbitrary",)),
        name="moe_experts",
    )(blk_expert, nvalid, run, nxt, xs, w_gate, w_up, w_down)


def _combine_kernel(dest_ref, x_ref, rw_ref, y_ref, o_ref, ybuf, sem, *, tm):
    i = pl.program_id(0)
    nsteps = pl.num_programs(0)

    def gather(step, buf):
        base = step * (tm * TOP_K_INNER)

        def body(r, carry):
            for slot in range(TOP_K_INNER):
                d = dest_ref[base + r * TOP_K_INNER + slot]
                _row_copy(y_ref.at[pl.ds(d, 1), :], ybuf.at[buf, slot, pl.ds(r, 1), :], sem.at[buf]).start()
            return carry

        lax.fori_loop(0, tm, body, 0, unroll=8)

    @pl.when(i == 0)
    def _():
        gather(0, 0)

    @pl.when(i + 1 < nsteps)
    def _():
        gather(i + 1, (i + 1) % 2)

    cur = i % 2
    for slot in range(TOP_K_INNER):
        _row_copy(y_ref.at[pl.ds(0, tm), :], ybuf.at[cur, slot], sem.at[cur]).wait()
    rw = rw_ref[...]
    o_ref[...] = x_ref[...] + rw[:, 0:1] * ybuf[cur, 0] + rw[:, 1:2] * ybuf[cur, 1]


def moe_combine(dest, x1, rw, y):
    m, d = x1.shape
    tm = min(ROW_T, m)
    grid_spec = pltpu.PrefetchScalarGridSpec(
        num_scalar_prefetch=1,
        grid=(m // tm,),
        in_specs=[
            pl.BlockSpec((tm, d), lambda i, dst: (i, 0)),
            pl.BlockSpec((tm, LANES), lambda i, dst: (i, 0)),
            pl.BlockSpec(memory_space=pl.ANY),
        ],
        out_specs=pl.BlockSpec((tm, d), lambda i, dst: (i, 0)),
        scratch_shapes=[pltpu.VMEM((2, TOP_K_INNER, tm, d), F32), pltpu.SemaphoreType.DMA((2,))],
    )
    return pl.pallas_call(
        functools.partial(_combine_kernel, tm=tm),
        grid_spec=grid_spec,
        out_shape=jax.ShapeDtypeStruct((m, d), F32),
        compiler_params=_cp(("arbitrary",)),
        name="moe_combine",
    )(dest, x1, rw, y)


def _layer(x, p, norm_mix_g, w_in, gla_w_alpha, gla_b_alpha, gla_norm_g, w_out_gla, q_norm_g, k_norm_g, rel_bias,
           w_out_dsa, w_branch_gate, b_branch_gate, w_out, norm_ffn_g, w_group_router, b_group_router,
           w_expert_router, b_expert_router, w_exp_gate, w_exp_up, w_exp_down, norm_ple_g, w_ple_gate,
           b_ple_gate, w_ple_proj):
    b, s, d = x.shape
    t = b * s
    x2d = x.reshape(t, d)

    o_gq, o_gk, o_gv, o_gr = 0, GLA_QK, 2 * GLA_QK, 2 * GLA_QK + GLA_V
    o_glr = o_gr + GLA_V
    o_dq = o_glr + GLA_GATE_RANK
    o_dk, o_dv = o_dq + DSA_W, o_dq + 2 * DSA_W
    o_iq = o_dv + DSA_W
    o_ik = o_iq + IDX_Q
    o_iw = o_ik + IDX_DIM
    w_g = w_in[:, o_gq:o_glr].astype(BF16)
    w_qk = w_in[:, o_dq:o_dv].astype(BF16)
    w_vi = w_in[:, o_dv:o_ik].astype(BF16)
    pad = jnp.zeros((d, LANES - GLA_GATE_RANK - IDX_HEADS - IDX_DIM), F32)
    w_s = jnp.concatenate([w_in[:, o_glr:o_dq], w_in[:, o_iw:o_iw + IDX_HEADS], pad,
                           w_in[:, o_ik:o_iw]], axis=1).astype(BF16)

    h = rmsnorm(x2d, norm_mix_g, BF16)
    zg = matmul(h, w_g, _epi_plain, [], BF16, "proj_gla")
    qk_gain = jnp.concatenate([jnp.tile(q_norm_g * (DSA_HD ** -0.5 * LOG2E), DSA_HEADS),
                               jnp.tile(k_norm_g, DSA_HEADS)]).reshape(1, 2 * DSA_W).astype(F32)
    zqk = matmul(h, w_qk, _epi_headnorm, [(qk_gain, "row")], BF16, "proj_dsa_qk")
    zvi = matmul(h, w_vi, _epi_plain, [], BF16, "proj_dsa_vi")
    zs = matmul(h, w_s, _epi_plain, [], F32, "proj_small")
    gates = matmul(h, w_branch_gate.astype(BF16), _epi_sigmoid_bias,
                   [(b_branch_gate.reshape(1, -1).astype(F32), "row")], BF16, "branch_gates")

    wa = gla_w_alpha.reshape(GLA_GATE_RANK, GLA_HEADS, GLA_DK).transpose(1, 0, 2).astype(BF16)
    ba = gla_b_alpha.reshape(GLA_HEADS, 1, GLA_DK).astype(F32)
    o_a = gla(zg, zs, wa, ba, gla_norm_g.reshape(1, GLA_DV).astype(F32), b, s)

    ik = zs[:, LANES - IDX_DIM:].astype(BF16).reshape(b, s, IDX_DIM)
    w_hm = zs[:, GLA_GATE_RANK:GLA_GATE_RANK + IDX_HEADS].reshape(b, s, IDX_HEADS).transpose(0, 2, 1)
    mask = dsa_index(zvi, w_hm, ik, b, s)
    nbd, nbp = _near_bias_tables(rel_bias, min(ATT_T, s))
    vt = zvi[:, :DSA_W].reshape(b, s, DSA_HEADS, DSA_HD).transpose(0, 2, 3, 1)
    vt = jnp.concatenate([vt, jnp.ones((b, DSA_HEADS, ATT_VROWS - DSA_HD, s), BF16)], axis=2)
    vt = vt.reshape(b, DSA_HEADS * ATT_VROWS, s)
    o_b = dsa_attention(zqk, vt, mask, nbd, nbp, b, s).transpose(0, 2, 1).reshape(t, DSA_W)

    m = mix(o_a, o_b, w_out_gla.astype(BF16), w_out_dsa.astype(BF16), gates)
    x1 = matmul(m, w_out.astype(BF16), _epi_residual, [(x2d, "tile")], F32, "out_proj")

    w_r = jnp.concatenate([w_group_router, w_expert_router,
                           jnp.zeros((d, LANES - N_GROUPS - N_EXPERTS), F32)], axis=1).astype(BF16)
    b_r = jnp.concatenate([b_group_router, b_expert_router,
                           jnp.zeros((LANES - N_GROUPS - N_EXPERTS,), F32)]).reshape(1, LANES)
    hn, ri, rw = router(x1, norm_ffn_g, w_r, b_r)
    a = t * TOP_K_INNER
    eid = ri[:, :TOP_K_INNER].reshape(a)
    rank, counts = moe_rank(eid)
    padded = ((counts + MOE_BLK - 1) // MOE_BLK) * MOE_BLK
    pend = jnp.cumsum(padded)
    pstart = pend - padded
    dest = (pstart[eid] + rank).astype(I32)
    prow = a + N_EXPERTS * MOE_BLK
    nblk = prow // MOE_BLK
    blk_row0 = jnp.arange(nblk, dtype=I32)[:, None] * MOE_BLK
    blk_expert = jnp.minimum(jnp.sum((pend[None, :] <= blk_row0).astype(I32), axis=1), N_EXPERTS - 1)
    nvalid = (pend[-1:] // MOE_BLK).astype(I32)
    xs = moe_dispatch(dest, hn, jnp.zeros((prow, d), F32))
    y = moe_experts(blk_expert, nvalid, padded > 0, xs, w_exp_gate, w_exp_up, w_exp_down)
    x2 = moe_combine(dest, x1, rw, y)

    hp = rmsnorm(x2, norm_ple_g, BF16)
    out = matmul(hp, w_ple_gate.astype(BF16), _epi_ple,
                 [(b_ple_gate.reshape(1, -1).astype(F32), "row"), (x2, "tile"),
                  (p.reshape(t, -1).astype(BF16), "mk"), (w_ple_proj.astype(BF16), "kn")],
                 F32, "ple")
    return out.reshape(b, s, d)


def kernel(x, p, norm_mix_g, w_in, gla_w_alpha, gla_b_alpha, gla_norm_g, w_out_gla, q_norm_g, k_norm_g, rel_bias, w_out_dsa, w_branch_gate, b_branch_gate, w_out, norm_ffn_g, w_group_router, b_group_router, w_expert_router, b_expert_router, w_exp_gate, w_exp_up, w_exp_down, norm_ple_g, w_ple_gate, b_ple_gate, w_ple_proj):
    depth = w_in.shape[0]
    for i in range(depth):
        x = _layer(x, p[i], norm_mix_g[i], w_in[i], gla_w_alpha[i], gla_b_alpha[i], gla_norm_g[i], w_out_gla[i],
                   q_norm_g[i], k_norm_g[i], rel_bias, w_out_dsa[i], w_branch_gate[i], b_branch_gate[i], w_out[i],
                   norm_ffn_g[i], w_group_router[i], b_group_router[i], w_expert_router[i], b_expert_router[i],
                   w_exp_gate[i], w_exp_up[i], w_exp_down[i], norm_ple_g[i], w_ple_gate[i], b_ple_gate[i],
                   w_ple_proj[i])
    return x
```

```python
import functools
import math

import jax
import jax.numpy as jnp
from jax import lax
from jax.experimental import pallas as pl
from jax.experimental.pallas import tpu as pltpu

F32 = jnp.float32
BF16 = jnp.bfloat16
I32 = jnp.int32
I16 = jnp.int16

EPS = 1e-6
CHUNK = 64
GLA_HEADS = 8
GLA_DK = 64
GLA_DV = 128
GLA_GATE_RANK = 16
GLA_GATE_TEMP = 16.0
GLA_QK = GLA_HEADS * GLA_DK
GLA_V = GLA_HEADS * GLA_DV
DSA_HEADS = 8
DSA_HD = 128
DSA_W = DSA_HEADS * DSA_HD
IDX_HEADS = 16
IDX_DIM = 64
IDX_Q = IDX_HEADS * IDX_DIM
INDEX_TOPK = 256
REL_BUCKETS = 32
REL_MAX_DIST = 128
N_GROUPS = 8
EXPERTS_PER_GROUP = 8
N_EXPERTS = N_GROUPS * EXPERTS_PER_GROUP
EXPERT_FF = 512
TOP_K_INNER = 2

LANES = 128
VMEM_LIMIT = 56 * 1024 * 1024
NEG_BIG = -1e30
LOG2E = math.log2(math.e)
KEY_NEG_INF = -2139095041
I16_MIN, I16_MAX = -32768, 32767
PACK16 = 16

GLA_CT = 512
GLA_HEADS_PER_STEP = 2
IDX_QG = 256
ATT_T = 512
ATT_VROWS = DSA_HD + PACK16
MOE_BLK = 256
RANK_RB = 1024
ROW_T = 256
MM_SPLIT = 4


def _cp(sem, **kw):
    return pltpu.CompilerParams(dimension_semantics=sem, vmem_limit_bytes=VMEM_LIMIT, **kw)


def _dot(a, b):
    return jnp.dot(a, b, preferred_element_type=F32)


def _dot_nt(a, b):
    return lax.dot_general(a, b, (((1,), (1,)), ((), ())), preferred_element_type=F32)


def _dot_tn(a, b):
    return lax.dot_general(a, b, (((0,), (0,)), ((), ())), preferred_element_type=F32)


def _sigmoid(x):
    return 1.0 / (1.0 + jnp.exp(-x))


def _rmsnorm_kernel(x_ref, g_ref, o_ref):
    x = x_ref[...]
    ms = jnp.mean(x * x, axis=-1, keepdims=True)
    o_ref[...] = (x * lax.rsqrt(ms + EPS) * g_ref[...]).astype(o_ref.dtype)


def rmsnorm(x, g, out_dtype, tm=512):
    m, d = x.shape
    tm = min(tm, m)
    return pl.pallas_call(
        _rmsnorm_kernel,
        grid=(m // tm,),
        in_specs=[pl.BlockSpec((tm, d), lambda i: (i, 0)), pl.BlockSpec((1, d), lambda i: (0, 0))],
        out_specs=pl.BlockSpec((tm, d), lambda i: (i, 0)),
        out_shape=jax.ShapeDtypeStruct((m, d), out_dtype),
        compiler_params=_cp(("parallel",)),
        name="rmsnorm",
    )(x, g.reshape(1, d).astype(F32))


def _row_chunks(tm):
    rc = tm // MM_SPLIT if tm % (MM_SPLIT * PACK16) == 0 else tm
    return [slice(r, r + rc) for r in range(0, tm, rc)]


def _mm_kernel(*refs, n_extra, epilogue):
    a_ref, w_ref = refs[0], refs[1]
    extras = refs[2:2 + n_extra]
    o_ref = refs[2 + n_extra]
    chunks = _row_chunks(a_ref.shape[0])
    acc = _dot(a_ref[chunks[0], :], w_ref[...])
    for c, rows in enumerate(chunks):
        nxt = _dot(a_ref[chunks[c + 1], :], w_ref[...]) if c + 1 < len(chunks) else None
        epilogue(acc, rows, o_ref, *extras)
        acc = nxt


def matmul(a, w, epilogue, extras, out_dtype, name, tm=1024, tn=512):
    m, k = a.shape
    n = w.shape[1]
    tm, tn = min(tm, m), min(tn, n)
    specs = [pl.BlockSpec((tm, k), lambda i, j: (i, 0)), pl.BlockSpec((k, tn), lambda i, j: (0, j))]
    args = [a, w]
    for arr, kind in extras:
        if kind == "tile":
            specs.append(pl.BlockSpec((tm, tn), lambda i, j: (i, j)))
        elif kind == "row":
            specs.append(pl.BlockSpec((1, tn), lambda i, j: (0, j)))
        elif kind == "mk":
            specs.append(pl.BlockSpec((tm, arr.shape[1]), lambda i, j: (i, 0)))
        elif kind == "kn":
            specs.append(pl.BlockSpec((arr.shape[0], tn), lambda i, j: (0, j)))
        else:
            raise ValueError(kind)
        args.append(arr)
    return pl.pallas_call(
        functools.partial(_mm_kernel, n_extra=len(extras), epilogue=epilogue),
        grid=(m // tm, n // tn),
        in_specs=specs,
        out_specs=pl.BlockSpec((tm, tn), lambda i, j: (i, j)),
        out_shape=jax.ShapeDtypeStruct((m, n), out_dtype),
        compiler_params=_cp(("parallel", "arbitrary")),
        name=name,
    )(*args)


def _epi_plain(acc, rows, o_ref):
    o_ref[rows, :] = acc.astype(o_ref.dtype)


def _epi_headnorm(acc, rows, o_ref, g_ref):
    for c in range(acc.shape[1] // DSA_HD):
        sl = slice(c * DSA_HD, (c + 1) * DSA_HD)
        blk = acc[:, sl]
        ms = jnp.mean(blk * blk, axis=-1, keepdims=True)
        o_ref[rows, sl] = (blk * lax.rsqrt(ms + EPS) * g_ref[:, sl]).astype(o_ref.dtype)


def _epi_sigmoid_bias(acc, rows, o_ref, b_ref):
    o_ref[rows, :] = _sigmoid(acc + b_ref[...]).astype(o_ref.dtype)


def _epi_residual(acc, rows, o_ref, x_ref):
    o_ref[rows, :] = (x_ref[rows, :] + acc).astype(o_ref.dtype)


def _epi_ple(acc, rows, o_ref, b_ref, x_ref, p_ref, wp_ref):
    ple = _dot(p_ref[rows, :], wp_ref[...])
    o_ref[rows, :] = (x_ref[rows, :] + _sigmoid(acc + b_ref[...]) * ple).astype(o_ref.dtype)


def _mix_kernel(oa_ref, ob_ref, wa_ref, wb_ref, ga_ref, gb_ref, o_ref):
    chunks = _row_chunks(oa_ref.shape[0])

    def both(rows):
        return _dot(oa_ref[rows, :], wa_ref[...]), _dot(ob_ref[rows, :], wb_ref[...])

    ya, yb = both(chunks[0])
    for c, rows in enumerate(chunks):
        nxt = both(chunks[c + 1]) if c + 1 < len(chunks) else None
        o_ref[rows, :] = (ga_ref[rows, :].astype(F32) * ya + gb_ref[rows, :].astype(F32) * yb).astype(o_ref.dtype)
        if nxt is not None:
            ya, yb = nxt


def mix(oa, ob, wa, wb, gates, tm=1024, tn=512):
    m, k = oa.shape
    n = wa.shape[1]
    tm, tn = min(tm, m), min(tn, n)
    nj = n // tn
    return pl.pallas_call(
        _mix_kernel,
        grid=(m // tm, nj),
        in_specs=[
            pl.BlockSpec((tm, k), lambda i, j: (i, 0)),
            pl.BlockSpec((tm, k), lambda i, j: (i, 0)),
            pl.BlockSpec((k, tn), lambda i, j: (0, j)),
            pl.BlockSpec((k, tn), lambda i, j: (0, j)),
            pl.BlockSpec((tm, tn), lambda i, j: (i, j)),
            pl.BlockSpec((tm, tn), lambda i, j: (i, j + nj)),
        ],
        out_specs=pl.BlockSpec((tm, tn), lambda i, j: (i, j)),
        out_shape=jax.ShapeDtypeStruct((m, n), BF16),
        compiler_params=_cp(("parallel", "arbitrary")),
        name="mix",
    )(oa, ob, wa, wb, gates, gates)


def _gla_kernel(q_ref, k_ref, v_ref, gr_ref, zs_ref, wa_ref, ba_ref, gn_ref, o_ref, st_ref, *, nchunk):
    @pl.when(pl.program_id(2) == 0)
    def _():
        st_ref[...] = jnp.zeros_like(st_ref)

    glr = zs_ref[:, 0:GLA_GATE_RANK].astype(BF16)
    row = lax.broadcasted_iota(I32, (CHUNK, CHUNK), 0)
    col = lax.broadcasted_iota(I32, (CHUNK, CHUNK), 1)
    causal = row >= col
    tril = jnp.where(causal, 1.0, 0.0).astype(BF16)
    gn = gn_ref[...]
    mid = CHUNK // 2 - 1
    nh = q_ref.shape[1] // GLA_DK
    sls = [slice(c * CHUNK, (c + 1) * CHUNK) for c in range(nchunk)]
    items = [(hh, c) for hh in range(nh) for c in range(nchunk)]

    def kcols(hh):
        return slice(hh * GLA_DK, (hh + 1) * GLA_DK)

    def vcols(hh):
        return slice(hh * GLA_DV, (hh + 1) * GLA_DV)

    g_hi, g_lo = [], []
    for hh in range(nh):
        logit = _dot(glr, wa_ref[hh]) + ba_ref[hh]
        g = (jnp.minimum(logit, 0.0) - jnp.log(1.0 + jnp.exp(-jnp.abs(logit)))) * (1.0 / GLA_GATE_TEMP)
        hi = g.astype(BF16)
        g_hi.append(hi)
        g_lo.append((g - hi.astype(F32)).astype(BF16))
    cums = {(hh, c): _dot(tril, g_hi[hh][sls[c]]) + _dot(tril, g_lo[hh][sls[c]]) for hh, c in items}
    qts, kts, qgs, kds, decs = {}, {}, {}, {}, {}
    for hh, c in items:
        cum = cums[hh, c]
        c0 = cum[mid:mid + 1, :]
        gl = cum[CHUNK - 1:CHUNK, :]
        qt = q_ref[sls[c], kcols(hh)].astype(F32) * (GLA_DK ** -0.5) * jnp.exp(cum - c0)
        kt = k_ref[sls[c], kcols(hh)].astype(F32) * jnp.exp(c0 - cum)
        qts[hh, c] = qt.astype(BF16)
        kts[hh, c] = kt.astype(BF16)
        qgs[hh, c] = (qt * jnp.exp(c0)).astype(BF16)
        kds[hh, c] = (kt * jnp.exp(gl - c0)).astype(BF16)
        decs[hh, c] = jnp.exp(gl)
    scs = {it: _dot_nt(qts[it], kts[it]) for it in items}
    kvs = {(hh, c): _dot_tn(v_ref[sls[c], vcols(hh)], kds[hh, c]) for hh, c in items}
    intra = {(hh, c): _dot(jnp.where(causal, scs[hh, c], 0.0).astype(BF16), v_ref[sls[c], vcols(hh)])
             for hh, c in items}
    sts = {}
    for hh in range(nh):
        st = st_ref[hh]
        for c in range(nchunk):
            sts[hh, c] = st.astype(BF16)
            st = st * decs[hh, c] + kvs[hh, c]
        st_ref[hh] = st
    inter = {it: _dot_nt(qgs[it], sts[it]) for it in items}
    for hh, c in items:
        o = intra[hh, c] + inter[hh, c]
        ms = jnp.mean(o * o, axis=-1, keepdims=True)
        gr = gr_ref[sls[c], vcols(hh)].astype(F32)
        o_ref[sls[c], vcols(hh)] = (o * lax.rsqrt(ms + EPS) * gn * (gr * _sigmoid(gr))).astype(o_ref.dtype)


def gla(zg, zs, wa, ba, gn, b, s):
    ct = min(GLA_CT, s)
    ns = s // ct
    hps = GLA_HEADS_PER_STEP
    kw, vw = hps * GLA_DK, hps * GLA_DV
    k_blk0 = GLA_QK // kw
    v_blk0 = (2 * GLA_QK) // vw
    gr_blk0 = (2 * GLA_QK + GLA_V) // vw
    return pl.pallas_call(
        functools.partial(_gla_kernel, nchunk=ct // CHUNK),
        grid=(b, GLA_HEADS // hps, ns),
        in_specs=[
            pl.BlockSpec((ct, kw), lambda bi, hp, si: (bi * ns + si, hp)),
            pl.BlockSpec((ct, kw), lambda bi, hp, si: (bi * ns + si, k_blk0 + hp)),
            pl.BlockSpec((ct, vw), lambda bi, hp, si: (bi * ns + si, v_blk0 + hp)),
            pl.BlockSpec((ct, vw), lambda bi, hp, si: (bi * ns + si, gr_blk0 + hp)),
            pl.BlockSpec((ct, LANES), lambda bi, hp, si: (bi * ns + si, 0)),
            pl.BlockSpec((hps, GLA_GATE_RANK, GLA_DK), lambda bi, hp, si: (hp, 0, 0)),
            pl.BlockSpec((hps, 1, GLA_DK), lambda bi, hp, si: (hp, 0, 0)),
            pl.BlockSpec((1, GLA_DV), lambda bi, hp, si: (0, 0)),
        ],
        out_specs=pl.BlockSpec((ct, vw), lambda bi, hp, si: (bi * ns + si, hp)),
        out_shape=jax.ShapeDtypeStruct((b * s, GLA_V), BF16),
        scratch_shapes=[pltpu.VMEM((hps, GLA_DV, GLA_DK), F32)],
        compiler_params=_cp(("parallel", "parallel", "arbitrary")),
        name="gla",
    )(zg, zg, zg, zg, zs, wa, ba, gn)


def _order_key(x):
    bits = lax.bitcast_convert_type(x, I32)
    return bits ^ ((bits >> 31) & 0x7FFFFFFF)


def _idx_kernel(iq_ref, w_ref, ik_ref, mask_ref, hi_ref, lo_ref, seg_ref, iqh_ref, *, qg, s):
    g = pl.program_id(1)
    nt = g + 1
    w = w_ref[...] * ((IDX_HEADS ** -0.5) * (IDX_DIM ** -0.5))
    kchunk = lax.broadcasted_iota(I32, (qg, qg), 0) // CHUNK
    qchunk = lax.broadcasted_iota(I32, (qg, qg), 1) // CHUNK
    adm_diag = kchunk <= qchunk

    half = qg // 2
    for j in range(IDX_HEADS):
        iqh_ref[j] = iq_ref[:, j * IDX_DIM:(j + 1) * IDX_DIM]

    def score_tile(t, carry):
        for hf in range(2):
            off = pl.multiple_of(t * qg + hf * half, half)
            ikt = ik_ref[pl.ds(off, half), :]
            acc = jnp.zeros((half, qg), F32)
            for j in range(IDX_HEADS):
                acc = acc + jnp.maximum(_dot_nt(ikt, iqh_ref[j]), 0.0) * w[j:j + 1, :]
            acc = jnp.where(jnp.logical_or(t < g, adm_diag[hf * half:(hf + 1) * half, :]), acc, -jnp.inf)
            key = _order_key(acc)
            hi_ref[pl.ds(off, half), :] = (key >> 16).astype(I16)
            lo_ref[pl.ds(off, half), :] = ((key & 0xFFFF) + I16_MIN).astype(I16)
            rows = slice(hf * half, (hf + 1) * half)
            seg_ref[rows, :] = jnp.maximum(seg_ref[rows, :], acc)
        return carry

    seg_ref[...] = jnp.full((qg, qg), -jnp.inf, F32)
    lax.fori_loop(0, nt, score_tile, 0)
    segmax = seg_ref[...]

    topk = min(INDEX_TOPK, s // 4)
    kf = float(topk)
    lb = _order_key(jnp.min(segmax, axis=0, keepdims=True))
    mx = _order_key(jnp.max(segmax, axis=0, keepdims=True))
    qcol = lax.broadcasted_iota(I32, (1, qg), 1)
    n_adm = g * qg + (qcol // CHUNK + 1) * CHUNK
    small = n_adm <= topk

    def count_ge(ref, cand):
        c16 = cand.astype(I16)

        def body(t, acc):
            off = pl.multiple_of(t * qg, qg)
            m = jnp.where(ref[pl.ds(off, qg), :] >= c16, jnp.int16(1), jnp.int16(0))
            for r in range(qg // PACK16):
                acc = acc + m[r * PACK16:(r + 1) * PACK16, :]
            return acc

        acc = lax.fori_loop(0, nt, body, jnp.zeros((PACK16, qg), I16))
        return jnp.sum(acc.astype(F32), axis=0, keepdims=True)

    def bisect(ref, lo0, hi0, target, npass):
        def body(_, c):
            lo, hi, c_hi = c
            mid = jnp.right_shift(lo + hi, 1)
            cnt = count_ge(ref, mid)
            up = cnt >= target
            return jnp.where(up, mid, lo), jnp.where(up, hi, mid), jnp.where(up, c_hi, cnt)

        lo, _, c_hi = lax.fori_loop(0, npass, body, (lo0, hi0, jnp.zeros((1, qg), F32)))
        return lo, c_hi

    a_lo0 = jnp.right_shift(lb, 16)
    a_hi0 = jnp.right_shift(mx, 16) + 1
    width = jnp.where(small, 1, a_hi0 - a_lo0)
    nbits = jnp.zeros((1, qg), F32)
    for b in range(17):
        nbits = nbits + jnp.where(width > (1 << b), 1.0, 0.0)
    a_lo, ca_hi = bisect(hi_ref, a_lo0, a_hi0, kf, jnp.max(nbits).astype(I32))
    a16 = jnp.where(small, I16_MAX, a_lo).astype(I16)

    def keep_equal_hi(t, carry):
        off = pl.multiple_of(t * qg, qg)
        lo_ref[pl.ds(off, qg), :] = jnp.where(hi_ref[pl.ds(off, qg), :] == a16, lo_ref[pl.ds(off, qg), :],
                                              jnp.int16(I16_MIN))
        return carry

    lax.fori_loop(0, nt, keep_equal_hi, 0)

    b_lo, _ = bisect(lo_ref, jnp.full((1, qg), I16_MIN, I32), jnp.full((1, qg), I16_MAX + 1, I32),
                     kf - ca_hi, 16)
    whole = b_lo == I16_MIN
    thr_h = jnp.where(small, jnp.int32(KEY_NEG_INF >> 16), jnp.where(whole, a_lo - 1, a_lo)).astype(I16)
    thr_l = jnp.where(jnp.logical_or(small, whole), I16_MAX, b_lo).astype(I16)

    def write_tile(t, acc):
        off = pl.multiple_of(t * qg, qg)
        one, nil = jnp.int16(1), jnp.int16(0)
        sel = jnp.where(hi_ref[pl.ds(off, qg), :] > thr_h, one, jnp.where(lo_ref[pl.ds(off, qg), :] >= thr_l, one, nil))
        mask_ref[pl.ds(off, qg), :] = jnp.where(sel > nil, jnp.zeros((qg, qg), BF16), jnp.full((qg, qg), NEG_BIG, BF16))
        for r in range(qg // PACK16):
            acc = acc + sel[r * PACK16:(r + 1) * PACK16, :]
        return acc

    n_sel = lax.fori_loop(0, nt, write_tile, jnp.zeros((PACK16, qg), I16))
    n_sel = jnp.sum(n_sel.astype(F32), axis=0, keepdims=True)
    excess = jnp.where(small, 0.0, n_sel - kf)

    @pl.when(jnp.max(excess) > 0.5)
    def _():
        h16 = a_lo.astype(I16)
        l16 = b_lo.astype(I16)

        def tied(off):
            same_lo = jnp.where(lo_ref[pl.ds(off, qg), :] == l16, jnp.ones((qg, qg), BF16), jnp.zeros((qg, qg), BF16))
            return jnp.where(hi_ref[pl.ds(off, qg), :] == h16, same_lo, jnp.zeros((qg, qg), BF16))

        def count_tied(t, acc):
            eq = tied(pl.multiple_of(t * qg, qg)).astype(F32)
            return acc + jnp.sum(eq.reshape(qg // 8, 8, qg), axis=0)

        n_eq = jnp.sum(lax.fori_loop(0, nt, count_tied, jnp.zeros((8, qg), F32)), axis=0, keepdims=True)
        need_eq = jnp.where(small, 1e9, kf - (n_sel - n_eq))
        krow = lax.broadcasted_iota(I32, (qg, qg), 0)
        kcol = lax.broadcasted_iota(I32, (qg, qg), 1)
        tril_incl = jnp.where(krow >= kcol, 1.0, 0.0).astype(BF16)

        def drop_late_ties(t, run):
            off = pl.multiple_of(t * qg, qg)
            eq = tied(off)
            rank = _dot(tril_incl, eq) + run
            late = jnp.where(rank > need_eq, 1.0, 0.0).astype(BF16) * eq
            old = mask_ref[pl.ds(off, qg), :]
            mask_ref[pl.ds(off, qg), :] = jnp.where(late > 0, jnp.full((qg, qg), NEG_BIG, BF16), old)
            return rank[qg - 1:qg, :]

        lax.fori_loop(0, nt, drop_late_ties, jnp.zeros((1, qg), F32))

    def fill_tile(t, carry):
        off = pl.multiple_of(t * qg, qg)
        mask_ref[pl.ds(off, qg), :] = jnp.full((qg, qg), NEG_BIG, BF16)
        return carry

    lax.fori_loop(nt, s // qg, fill_tile, 0)


def dsa_index(zvi, w_hm, ik, b, s):
    qg = min(IDX_QG, s)
    ng = s // qg
    return pl.pallas_call(
        functools.partial(_idx_kernel, qg=qg, s=s),
        grid=(b, ng),
        in_specs=[
            pl.BlockSpec((qg, IDX_Q), lambda bi, gi: (bi * ng + gi, DSA_W // IDX_Q)),
            pl.BlockSpec((None, IDX_HEADS, qg), lambda bi, gi: (bi, 0, gi)),
            pl.BlockSpec((None, s, IDX_DIM), lambda bi, gi: (bi, 0, 0)),
        ],
        out_specs=pl.BlockSpec((None, s, qg), lambda bi, gi: (bi, 0, gi)),
        out_shape=jax.ShapeDtypeStruct((b, s, s), BF16),
        scratch_shapes=[pltpu.VMEM((s, qg), I16), pltpu.VMEM((s, qg), I16), pltpu.VMEM((qg, qg), F32),
                        pltpu.VMEM((IDX_HEADS, qg, IDX_DIM), BF16)],
        compiler_params=_cp(("parallel", "arbitrary")),
        name="dsa_index",
    )(zvi, w_hm, ik)


def _attn_kernel(qi_ref, ki_ref, q_ref, k_ref, vt_ref, mask_ref, nbd_ref, nbp_ref, o_ref, m_ref, acc_ref):
    p = pl.program_id(1)
    qi = qi_ref[p]
    ki = ki_ref[p]

    @pl.when(ki == 0)
    def _():
        m_ref[...] = jnp.full_like(m_ref, NEG_BIG)
        acc_ref[...] = jnp.zeros_like(acc_ref)

    def hs(h):
        return slice(h * DSA_HD, (h + 1) * DSA_HD)

    def vs(h):
        return slice(h * ATT_VROWS, (h + 1) * ATT_VROWS)

    nrow = 2
    rch = q_ref.shape[0] // nrow
    kh = k_ref.shape[0] // 2

    def heads(bias_of_head):
        def score_chunk(h, r):
            return _dot_nt(k_ref[r * rch:(r + 1) * rch, hs(h)], q_ref[:, hs(h)])

        def value_half(h, pr, half):
            return _dot(vt_ref[vs(h), half * kh:(half + 1) * kh], pr)

        sc = {0: [score_chunk(0, r) for r in range(nrow)], 1: [score_chunk(1, r) for r in range(nrow)]}
        prev = None
        for h in range(DSA_HEADS + 1):
            if h < DSA_HEADS:
                bias = bias_of_head(h)
                cur = sc.pop(h)
                sb, cm = [], []
                nxt = []
                for r in range(nrow):
                    sb.append(cur[r] + bias[r * rch:(r + 1) * rch, :])
                    cm.append(jnp.max(sb[r], axis=0, keepdims=True))
                    if h + 2 < DSA_HEADS:
                        nxt.append(score_chunk(h + 2, r))
                if nxt:
                    sc[h + 2] = nxt
                m_old = m_ref[h]
                m_new = m_old
                for c in cm:
                    m_new = jnp.maximum(m_new, c)
                alpha = jnp.exp2(m_old - m_new)
                m_ref[h] = m_new
            pv = []
            pr = []
            for r in range(nrow):
                if h < DSA_HEADS:
                    pr.append(jnp.exp2(sb[r] - m_new).astype(BF16))
                if prev is not None and r % (nrow // 2) == 0:
                    pv.append(value_half(prev[0], prev[2][r // (nrow // 2)], r // (nrow // 2)))
            if prev is not None:
                ph = prev[0]
                acc_ref[vs(ph), :] = prev[1] * acc_ref[vs(ph), :] + pv[0] + pv[1]
            if h < DSA_HEADS:
                prev = (h, alpha, [jnp.concatenate(pr[:nrow // 2], axis=0), jnp.concatenate(pr[nrow // 2:], axis=0)])

    @pl.when(ki < qi - 1)
    def _():
        mb = mask_ref[...].astype(F32)
        heads(lambda h: mb)

    @pl.when(ki == qi - 1)
    def _():
        mb = mask_ref[...].astype(F32)
        heads(lambda h: mb + nbp_ref[h].astype(F32))

    @pl.when(ki == qi)
    def _():
        mb = mask_ref[...].astype(F32)
        heads(lambda h: mb + nbd_ref[h].astype(F32))
        for h in range(DSA_HEADS):
            num = acc_ref[h * ATT_VROWS:h * ATT_VROWS + DSA_HD, :]
            den = acc_ref[h * ATT_VROWS + DSA_HD:h * ATT_VROWS + DSA_HD + 1, :]
            o_ref[hs(h), :] = (num / den).astype(o_ref.dtype)


def dsa_attention(zqk, vt, mask, nbd, nbp, b, s):
    t = min(ATT_T, s)
    ng = s // t
    pairs = [(qi, ki) for qi in range(ng) for ki in range(qi + 1)]
    qi_tab = jnp.asarray([p[0] for p in pairs], I32)
    ki_tab = jnp.asarray([p[1] for p in pairs], I32)
    grid_spec = pltpu.PrefetchScalarGridSpec(
        num_scalar_prefetch=2,
        grid=(b, len(pairs)),
        in_specs=[
            pl.BlockSpec((t, DSA_W), lambda bi, p, qt, kt: (bi * ng + qt[p], 0)),
            pl.BlockSpec((t, DSA_W), lambda bi, p, qt, kt: (bi * ng + kt[p], 1)),
            pl.BlockSpec((None, DSA_HEADS * ATT_VROWS, t), lambda bi, p, qt, kt: (bi, 0, kt[p])),
            pl.BlockSpec((None, t, t), lambda bi, p, qt, kt: (bi, kt[p], qt[p])),
            pl.BlockSpec((DSA_HEADS, t, t), lambda bi, p, qt, kt: (0, 0, 0)),
            pl.BlockSpec((DSA_HEADS, t, t), lambda bi, p, qt, kt: (0, 0, 0)),
        ],
        out_specs=pl.BlockSpec((None, DSA_W, t), lambda bi, p, qt, kt: (bi, 0, qt[p])),
        scratch_shapes=[
            pltpu.VMEM((DSA_HEADS, 1, t), F32),
            pltpu.VMEM((DSA_HEADS * ATT_VROWS, t), F32),
        ],
    )
    return pl.pallas_call(
        _attn_kernel,
        grid_spec=grid_spec,
        out_shape=jax.ShapeDtypeStruct((b, DSA_W, s), BF16),
        compiler_params=_cp(("parallel", "arbitrary")),
        name="dsa_attention",
    )(qi_tab, ki_tab, zqk, zqk, vt, mask, nbd, nbp)


def _t5_bucket(rel):
    half = REL_BUCKETS // 2
    exact = half // 2
    sign = jnp.where(rel > 0, half, 0)
    n = jnp.abs(rel)
    nf = jnp.maximum(n, 1).astype(F32)
    large = exact + (jnp.log(nf / exact) / math.log(REL_MAX_DIST / exact) * (half - exact)).astype(I32)
    large = jnp.minimum(large, half - 1)
    return sign + jnp.where(n < exact, n, large)


def _near_bias_tables(rel_bias, t):
    kpos = jnp.arange(t, dtype=I32)[:, None]
    qpos = jnp.arange(t, dtype=I32)[None, :]
    shifted = (rel_bias - rel_bias[REL_BUCKETS // 2 - 1]) * LOG2E

    def table(rel):
        onehot = (_t5_bucket(rel)[:, :, None] == jnp.arange(REL_BUCKETS, dtype=I32)).astype(F32)
        return jnp.einsum("kqb,bh->hkq", onehot, shifted, precision=lax.Precision.HIGHEST).astype(BF16)

    return table(kpos - qpos), table(kpos - t - qpos)


def _router_kernel(x_ref, g_ref, w_ref, b_ref, hn_ref, ri_ref, rw_ref):
    x = x_ref[...]
    ms = jnp.mean(x * x, axis=-1, keepdims=True)
    hn = x * lax.rsqrt(ms + EPS) * g_ref[...]
    hn_ref[...] = hn
    logits = _dot(hn.astype(BF16), w_ref[...]) + b_ref[...]
    lane = lax.broadcasted_iota(I32, logits.shape, 1)
    lanef = lane.astype(F32)
    big = 1e9
    is_grp = lane < N_GROUPS
    gl = jnp.where(is_grp, logits, -jnp.inf)
    gmax = jnp.max(gl, axis=-1, keepdims=True)
    grp = jnp.min(jnp.where(gl == gmax, lanef, big), axis=-1, keepdims=True)
    gsum = jnp.sum(jnp.where(is_grp, jnp.exp(logits - gmax), 0.0), axis=-1, keepdims=True)
    grp_w = 1.0 / gsum
    elo = N_GROUPS + grp * EXPERTS_PER_GROUP
    in_grp = jnp.logical_and(lanef >= elo, lanef < elo + EXPERTS_PER_GROUP)
    el = jnp.where(in_grp, logits, -jnp.inf)
    v1 = jnp.max(el, axis=-1, keepdims=True)
    i1 = jnp.min(jnp.where(el == v1, lanef, big), axis=-1, keepdims=True)
    el2 = jnp.where(lanef == i1, -jnp.inf, el)
    v2 = jnp.max(el2, axis=-1, keepdims=True)
    i2 = jnp.min(jnp.where(el2 == v2, lanef, big), axis=-1, keepdims=True)
    e2 = jnp.exp(v2 - v1)
    w1 = grp_w * (1.0 / (1.0 + e2))
    w2 = grp_w * (e2 / (1.0 + e2))
    ri_ref[...] = jnp.where(lane == 0, i1 - N_GROUPS, jnp.where(lane == 1, i2 - N_GROUPS, 0.0)).astype(I32)
    rw_ref[...] = jnp.where(lane == 0, w1, jnp.where(lane == 1, w2, 0.0))


def router(x1, g, w_r, b_r, tm=256):
    m, d = x1.shape
    tm = min(tm, m)
    return pl.pallas_call(
        _router_kernel,
        grid=(m // tm,),
        in_specs=[
            pl.BlockSpec((tm, d), lambda i: (i, 0)),
            pl.BlockSpec((1, d), lambda i: (0, 0)),
            pl.BlockSpec((d, LANES), lambda i: (0, 0)),
            pl.BlockSpec((1, LANES), lambda i: (0, 0)),
        ],
        out_specs=[
            pl.BlockSpec((tm, d), lambda i: (i, 0)),
            pl.BlockSpec((tm, LANES), lambda i: (i, 0)),
            pl.BlockSpec((tm, LANES), lambda i: (i, 0)),
        ],
        out_shape=[
            jax.ShapeDtypeStruct((m, d), F32),
            jax.ShapeDtypeStruct((m, LANES), I32),
            jax.ShapeDtypeStruct((m, LANES), F32),
        ],
        compiler_params=_cp(("parallel",)),
        name="router",
    )(x1, g.reshape(1, d).astype(F32), w_r, b_r)


def _rank_kernel(e_ref, rank_ref, cnt_ref, run_ref, *, rb):
    @pl.when(pl.program_id(0) == 0)
    def _():
        run_ref[...] = jnp.zeros_like(run_ref)

    e = e_ref[...]
    sub = lax.broadcasted_iota(I32, (N_EXPERTS, rb), 0)
    hit = sub == e
    oh = jnp.where(hit, 1.0, 0.0)
    a = lax.broadcasted_iota(I32, (rb, rb), 0)
    bcol = lax.broadcasted_iota(I32, (rb, rb), 1)
    upper = jnp.where(a < bcol, 1.0, 0.0).astype(BF16)
    prefix = _dot(oh.astype(BF16), upper)
    run = run_ref[...]
    rank = jnp.sum(jnp.where(hit, prefix + run[:, 0:1], 0.0), axis=0, keepdims=True)
    rank_ref[...] = rank.astype(I32)
    new = run + jnp.sum(oh, axis=1, keepdims=True)
    run_ref[...] = new
    cnt_ref[...] = new


def moe_rank(eid_flat):
    a = eid_flat.shape[0]
    rb = min(RANK_RB, a)
    nb = a // rb
    rank, cnt = pl.pallas_call(
        functools.partial(_rank_kernel, rb=rb),
        grid=(nb,),
        in_specs=[pl.BlockSpec((None, 1, rb), lambda i: (i, 0, 0))],
        out_specs=[
            pl.BlockSpec((None, 1, rb), lambda i: (i, 0, 0)),
            pl.BlockSpec((N_EXPERTS, LANES), lambda i: (0, 0)),
        ],
        out_shape=[
            jax.ShapeDtypeStruct((nb, 1, rb), I32),
            jax.ShapeDtypeStruct((N_EXPERTS, LANES), F32),
        ],
        scratch_shapes=[pltpu.VMEM((N_EXPERTS, LANES), F32)],
        compiler_params=_cp(("arbitrary",)),
        name="moe_rank",
    )(eid_flat.reshape(nb, 1, rb))
    return rank.reshape(a), cnt[:, 0].astype(I32)


def _row_copy(src, dst, sem):
    return pltpu.make_async_copy(src, dst, sem)


def _dispatch_kernel(dest_ref, hn_ref, xs_in_ref, xs_ref, sem, *, tm):
    del xs_in_ref
    base = pl.program_id(0) * (tm * TOP_K_INNER)

    def body(r, carry):
        for slot in range(TOP_K_INNER):
            d = dest_ref[base + r * TOP_K_INNER + slot]
            _row_copy(hn_ref.at[pl.ds(r, 1), :], xs_ref.at[pl.ds(d, 1), :], sem).start()
        return carry

    lax.fori_loop(0, tm, body, 0, unroll=8)
    for _ in range(TOP_K_INNER):
        _row_copy(hn_ref, xs_ref.at[pl.ds(0, tm), :], sem).wait()


def moe_dispatch(dest, hn, xs0):
    m, d = hn.shape
    tm = min(ROW_T, m)
    grid_spec = pltpu.PrefetchScalarGridSpec(
        num_scalar_prefetch=1,
        grid=(m // tm,),
        in_specs=[
            pl.BlockSpec((tm, d), lambda i, dst: (i, 0)),
            pl.BlockSpec(memory_space=pl.ANY),
        ],
        out_specs=pl.BlockSpec(memory_space=pl.ANY),
        scratch_shapes=[pltpu.SemaphoreType.DMA(())],
    )
    return pl.pallas_call(
        functools.partial(_dispatch_kernel, tm=tm),
        grid_spec=grid_spec,
        out_shape=jax.ShapeDtypeStruct(xs0.shape, xs0.dtype),
        input_output_aliases={2: 0},
        compiler_params=_cp(("arbitrary",), has_side_effects=True),
        name="moe_dispatch",
    )(dest, hn, xs0)


def _expert_kernel(be_ref, nv_ref, run_ref, nxt_ref, xs_ref, wg_hbm, wu_hbm, wd_hbm, y_ref,
                   wg_f, wu_f, wd_f, wg_bf, wu_bf, wd_bf, sem):
    i = pl.program_id(0)
    valid = i < nv_ref[0]
    e = be_ref[i]
    fresh = jnp.logical_and(valid, jnp.logical_or(i == 0, e != be_ref[jnp.maximum(i - 1, 0)]))
    slot = run_ref[i] % 2

    def weight_copies(expert, s_):
        return [pltpu.make_async_copy(wg_hbm.at[expert], wg_f.at[s_], sem.at[s_, 0]),
                pltpu.make_async_copy(wu_hbm.at[expert], wu_f.at[s_], sem.at[s_, 1]),
                pltpu.make_async_copy(wd_hbm.at[expert], wd_f.at[s_], sem.at[s_, 2])]

    @pl.when(i == 0)
    def _():
        for c in weight_copies(e, 0):
            c.start()

    @pl.when(fresh)
    def _():
        nxt = nxt_ref[i]

        @pl.when(nxt >= 0)
        def _():
            for c in weight_copies(nxt, 1 - slot):
                c.start()

        for c in weight_copies(e, slot):
            c.wait()
        wg_bf[...] = wg_f[slot].astype(BF16)
        wu_bf[...] = wu_f[slot].astype(BF16)
        wd_bf[...] = wd_f[slot].astype(BF16)

    @pl.when(valid)
    def _():
        xb = xs_ref[...].astype(BF16)
        hg = _dot(xb, wg_bf[...])
        hu = _dot(xb, wu_bf[...])
        act = (hg * _sigmoid(hg) * hu).astype(BF16)
        y_ref[...] = _dot(act, wd_bf[...])

    @pl.when(jnp.logical_not(valid))
    def _():
        y_ref[...] = jnp.zeros_like(y_ref)


def moe_experts(blk_expert, nvalid, has_rows, xs, w_gate, w_up, w_down):
    p, d = xs.shape
    nblk = p // MOE_BLK
    ff = w_gate.shape[2]
    first = jnp.concatenate([jnp.ones((1,), bool), blk_expert[1:] != blk_expert[:-1]])
    run = (jnp.cumsum(first.astype(I32)) - 1).astype(I32)
    eids = jnp.arange(N_EXPERTS, dtype=I32)
    later = jnp.logical_and(eids[None, :] > eids[:, None], has_rows[None, :])
    nxt_tab = jnp.min(jnp.where(later, eids[None, :], N_EXPERTS), axis=1)
    nxt_tab = jnp.where(nxt_tab >= N_EXPERTS, -1, nxt_tab)
    nxt = jnp.sum(jnp.where(blk_expert[:, None] == eids[None, :], nxt_tab[None, :], 0), axis=1).astype(I32)

    def blk(i, nv):
        return jnp.minimum(i, nv[0] - 1)

    grid_spec = pltpu.PrefetchScalarGridSpec(
        num_scalar_prefetch=4,
        grid=(nblk,),
        in_specs=[
            pl.BlockSpec((MOE_BLK, d), lambda i, be, nv, rn, nx: (blk(i, nv), 0)),
            pl.BlockSpec(memory_space=pl.ANY),
            pl.BlockSpec(memory_space=pl.ANY),
            pl.BlockSpec(memory_space=pl.ANY),
        ],
        out_specs=pl.BlockSpec((MOE_BLK, d), lambda i, be, nv, rn, nx: (i, 0)),
        scratch_shapes=[
            pltpu.VMEM((2, d, ff), F32), pltpu.VMEM((2, d, ff), F32), pltpu.VMEM((2, ff, d), F32),
            pltpu.VMEM((d, ff), BF16), pltpu.VMEM((d, ff), BF16), pltpu.VMEM((ff, d), BF16),
            pltpu.SemaphoreType.DMA((2, 3)),
        ],
    )
    return pl.pallas_call(
        _expert_kernel,
        grid_spec=grid_spec,
        out_shape=jax.ShapeDtypeStruct((p, d), F32),
        compiler_params=_cp(("arbitrary",)),
        name="moe_experts",
    )(blk_expert, nvalid, run, nxt, xs, w_gate, w_up, w_down)


def _combine_kernel(dest_ref, x_ref, rw_ref, g_ref, y_ref, o_ref, hn_ref, ybuf, sem, *, tm):
    i = pl.program_id(0)
    nsteps = pl.num_programs(0)

    def gather(step, buf):
        base = step * (tm * TOP_K_INNER)

        def body(r, carry):
            for slot in range(TOP_K_INNER):
                d = dest_ref[base + r * TOP_K_INNER + slot]
                _row_copy(y_ref.at[pl.ds(d, 1), :], ybuf.at[buf, slot, pl.ds(r, 1), :], sem.at[buf]).start()
            return carry

        lax.fori_loop(0, tm, body, 0, unroll=8)

    @pl.when(i == 0)
    def _():
        gather(0, 0)

    @pl.when(i + 1 < nsteps)
    def _():
        gather(i + 1, (i + 1) % 2)

    cur = i % 2
    for slot in range(TOP_K_INNER):
        _row_copy(y_ref.at[pl.ds(0, tm), :], ybuf.at[cur, slot], sem.at[cur]).wait()
    rw = rw_ref[...]
    o = x_ref[...] + rw[:, 0:1] * ybuf[cur, 0] + rw[:, 1:2] * ybuf[cur, 1]
    o_ref[...] = o
    ms = jnp.mean(o * o, axis=-1, keepdims=True)
    hn_ref[...] = (o * lax.rsqrt(ms + EPS) * g_ref[...]).astype(hn_ref.dtype)


def moe_combine(dest, x1, rw, g, y):
    m, d = x1.shape
    tm = min(ROW_T, m)
    grid_spec = pltpu.PrefetchScalarGridSpec(
        num_scalar_prefetch=1,
        grid=(m // tm,),
        in_specs=[
            pl.BlockSpec((tm, d), lambda i, dst: (i, 0)),
            pl.BlockSpec((tm, LANES), lambda i, dst: (i, 0)),
            pl.BlockSpec((1, d), lambda i, dst: (0, 0)),
            pl.BlockSpec(memory_space=pl.ANY),
        ],
        out_specs=[pl.BlockSpec((tm, d), lambda i, dst: (i, 0)), pl.BlockSpec((tm, d), lambda i, dst: (i, 0))],
        scratch_shapes=[pltpu.VMEM((2, TOP_K_INNER, tm, d), F32), pltpu.SemaphoreType.DMA((2,))],
    )
    return pl.pallas_call(
        functools.partial(_combine_kernel, tm=tm),
        grid_spec=grid_spec,
        out_shape=[jax.ShapeDtypeStruct((m, d), F32), jax.ShapeDtypeStruct((m, d), BF16)],
        compiler_params=_cp(("arbitrary",)),
        name="moe_combine",
    )(dest, x1, rw, g.reshape(1, d).astype(F32), y)


def _layer(x, p, norm_mix_g, w_in, gla_w_alpha, gla_b_alpha, gla_norm_g, w_out_gla, q_norm_g, k_norm_g, rel_bias,
           w_out_dsa, w_branch_gate, b_branch_gate, w_out, norm_ffn_g, w_group_router, b_group_router,
           w_expert_router, b_expert_router, w_exp_gate, w_exp_up, w_exp_down, norm_ple_g, w_ple_gate,
           b_ple_gate, w_ple_proj):
    b, s, d = x.shape
    t = b * s
    x2d = x.reshape(t, d)

    o_gq, o_gk, o_gv, o_gr = 0, GLA_QK, 2 * GLA_QK, 2 * GLA_QK + GLA_V
    o_glr = o_gr + GLA_V
    o_dq = o_glr + GLA_GATE_RANK
    o_dk, o_dv = o_dq + DSA_W, o_dq + 2 * DSA_W
    o_iq = o_dv + DSA_W
    o_ik = o_iq + IDX_Q
    o_iw = o_ik + IDX_DIM
    w_g = w_in[:, o_gq:o_glr].astype(BF16)
    w_qk = w_in[:, o_dq:o_dv].astype(BF16)
    w_vi = w_in[:, o_dv:o_ik].astype(BF16)
    pad = jnp.zeros((d, LANES - GLA_GATE_RANK - IDX_HEADS - IDX_DIM), F32)
    w_s = jnp.concatenate([w_in[:, o_glr:o_dq], w_in[:, o_iw:o_iw + IDX_HEADS], pad,
                           w_in[:, o_ik:o_iw]], axis=1).astype(BF16)

    h = rmsnorm(x2d, norm_mix_g, BF16)
    zg = matmul(h, w_g, _epi_plain, [], BF16, "proj_gla")
    qk_gain = jnp.concatenate([jnp.tile(q_norm_g * (DSA_HD ** -0.5 * LOG2E), DSA_HEADS),
                               jnp.tile(k_norm_g, DSA_HEADS)]).reshape(1, 2 * DSA_W).astype(F32)
    zqk = matmul(h, w_qk, _epi_headnorm, [(qk_gain, "row")], BF16, "proj_dsa_qk")
    zvi = matmul(h, w_vi, _epi_plain, [], BF16, "proj_dsa_vi")
    zs = matmul(h, w_s, _epi_plain, [], F32, "proj_small")
    gates = matmul(h, w_branch_gate.astype(BF16), _epi_sigmoid_bias,
                   [(b_branch_gate.reshape(1, -1).astype(F32), "row")], BF16, "branch_gates")

    wa = gla_w_alpha.reshape(GLA_GATE_RANK, GLA_HEADS, GLA_DK).transpose(1, 0, 2).astype(BF16)
    ba = gla_b_alpha.reshape(GLA_HEADS, 1, GLA_DK).astype(F32)
    o_a = gla(zg, zs, wa, ba, gla_norm_g.reshape(1, GLA_DV).astype(F32), b, s)

    ik = zs[:, LANES - IDX_DIM:].astype(BF16).reshape(b, s, IDX_DIM)
    w_hm = zs[:, GLA_GATE_RANK:GLA_GATE_RANK + IDX_HEADS].reshape(b, s, IDX_HEADS).transpose(0, 2, 1)
    mask = dsa_index(zvi, w_hm, ik, b, s)
    nbd, nbp = _near_bias_tables(rel_bias, min(ATT_T, s))
    vt = zvi[:, :DSA_W].reshape(b, s, DSA_HEADS, DSA_HD).transpose(0, 2, 3, 1)
    vt = jnp.concatenate([vt, jnp.ones((b, DSA_HEADS, ATT_VROWS - DSA_HD, s), BF16)], axis=2)
    vt = vt.reshape(b, DSA_HEADS * ATT_VROWS, s)
    o_b = dsa_attention(zqk, vt, mask, nbd, nbp, b, s).transpose(0, 2, 1).reshape(t, DSA_W)

    m = mix(o_a, o_b, w_out_gla.astype(BF16), w_out_dsa.astype(BF16), gates)
    x1 = matmul(m, w_out.astype(BF16), _epi_residual, [(x2d, "tile")], F32, "out_proj")

    w_r = jnp.concatenate([w_group_router, w_expert_router,
                           jnp.zeros((d, LANES - N_GROUPS - N_EXPERTS), F32)], axis=1).astype(BF16)
    b_r = jnp.concatenate([b_group_router, b_expert_router,
                           jnp.zeros((LANES - N_GROUPS - N_EXPERTS,), F32)]).reshape(1, LANES)
    hn, ri, rw = router(x1, norm_ffn_g, w_r, b_r)
    a = t * TOP_K_INNER
    eid = ri[:, :TOP_K_INNER].reshape(a)
    rank, counts = moe_rank(eid)
    padded = ((counts + MOE_BLK - 1) // MOE_BLK) * MOE_BLK
    pend = jnp.cumsum(padded)
    pstart = pend - padded
    dest = (pstart[eid] + rank).astype(I32)
    prow = a + N_EXPERTS * MOE_BLK
    nblk = prow // MOE_BLK
    blk_row0 = jnp.arange(nblk, dtype=I32)[:, None] * MOE_BLK
    blk_expert = jnp.minimum(jnp.sum((pend[None, :] <= blk_row0).astype(I32), axis=1), N_EXPERTS - 1)
    nvalid = (pend[-1:] // MOE_BLK).astype(I32)
    xs = moe_dispatch(dest, hn, jnp.zeros((prow, d), F32))
    y = moe_experts(blk_expert, nvalid, padded > 0, xs, w_exp_gate, w_exp_up, w_exp_down)
    x2, hp = moe_combine(dest, x1, rw, norm_ple_g, y)

    out = matmul(hp, w_ple_gate.astype(BF16), _epi_ple,
                 [(b_ple_gate.reshape(1, -1).astype(F32), "row"), (x2, "tile"),
                  (p.reshape(t, -1).astype(BF16), "mk"), (w_ple_proj.astype(BF16), "kn")],
                 F32, "ple")
    return out.reshape(b, s, d)


def kernel(x, p, norm_mix_g, w_in, gla_w_alpha, gla_b_alpha, gla_norm_g, w_out_gla, q_norm_g, k_norm_g, rel_bias, w_out_dsa, w_branch_gate, b_branch_gate, w_out, norm_ffn_g, w_group_router, b_group_router, w_expert_router, b_expert_router, w_exp_gate, w_exp_up, w_exp_down, norm_ple_g, w_ple_gate, b_ple_gate, w_ple_proj):
    depth = w_in.shape[0]
    for i in range(depth):
        x = _layer(x, p[i], norm_mix_g[i], w_in[i], gla_w_alpha[i], gla_b_alpha[i], gla_norm_g[i], w_out_gla[i],
                   q_norm_g[i], k_norm_g[i], rel_bias, w_out_dsa[i], w_branch_gate[i], b_branch_gate[i], w_out[i],
                   norm_ffn_g[i], w_group_router[i], b_group_router[i], w_expert_router[i], b_expert_router[i],
                   w_exp_gate[i], w_exp_up[i], w_exp_down[i], norm_ple_g[i], w_ple_gate[i], b_ple_gate[i],
                   w_ple_proj[i])
    return x
```

```python
import functools
import math

import jax
import jax.numpy as jnp
from jax import lax
from jax.experimental import pallas as pl
from jax.experimental.pallas import tpu as pltpu

F32 = jnp.float32
BF16 = jnp.bfloat16
I32 = jnp.int32
I16 = jnp.int16

EPS = 1e-6
CHUNK = 64
GLA_HEADS = 8
GLA_DK = 64
GLA_DV = 128
GLA_GATE_RANK = 16
GLA_GATE_TEMP = 16.0
GLA_QK = GLA_HEADS * GLA_DK
GLA_V = GLA_HEADS * GLA_DV
DSA_HEADS = 8
DSA_HD = 128
DSA_W = DSA_HEADS * DSA_HD
IDX_HEADS = 16
IDX_DIM = 64
IDX_Q = IDX_HEADS * IDX_DIM
INDEX_TOPK = 256
REL_BUCKETS = 32
REL_MAX_DIST = 128
N_GROUPS = 8
EXPERTS_PER_GROUP = 8
N_EXPERTS = N_GROUPS * EXPERTS_PER_GROUP
EXPERT_FF = 512
TOP_K_INNER = 2

LANES = 128
VMEM_LIMIT = 56 * 1024 * 1024
NEG_BIG = -1e30
LOG2E = math.log2(math.e)
KEY_NEG_INF = -2139095041
I16_MIN, I16_MAX = -32768, 32767
PACK16 = 16

GLA_CT = 512
GLA_HEADS_PER_STEP = 2
IDX_QG = 256
ATT_T = 512
ATT_VROWS = DSA_HD + PACK16
MOE_BLK = 256
RANK_RB = 1024
ROW_T = 256
MM_SPLIT = 4


def _cp(sem, **kw):
    return pltpu.CompilerParams(dimension_semantics=sem, vmem_limit_bytes=VMEM_LIMIT, **kw)


def _dot(a, b):
    return jnp.dot(a, b, preferred_element_type=F32)


def _dot_nt(a, b):
    return lax.dot_general(a, b, (((1,), (1,)), ((), ())), preferred_element_type=F32)


def _dot_tn(a, b):
    return lax.dot_general(a, b, (((0,), (0,)), ((), ())), preferred_element_type=F32)


def _sigmoid(x):
    return 1.0 / (1.0 + jnp.exp(-x))


def _rmsnorm_kernel(x_ref, g_ref, o_ref):
    x = x_ref[...]
    ms = jnp.mean(x * x, axis=-1, keepdims=True)
    o_ref[...] = (x * lax.rsqrt(ms + EPS) * g_ref[...]).astype(o_ref.dtype)


def rmsnorm(x, g, out_dtype, tm=512):
    m, d = x.shape
    tm = min(tm, m)
    return pl.pallas_call(
        _rmsnorm_kernel,
        grid=(m // tm,),
        in_specs=[pl.BlockSpec((tm, d), lambda i: (i, 0)), pl.BlockSpec((1, d), lambda i: (0, 0))],
        out_specs=pl.BlockSpec((tm, d), lambda i: (i, 0)),
        out_shape=jax.ShapeDtypeStruct((m, d), out_dtype),
        compiler_params=_cp(("parallel",)),
        name="rmsnorm",
    )(x, g.reshape(1, d).astype(F32))


def _row_chunks(tm):
    rc = tm // MM_SPLIT if tm % (MM_SPLIT * PACK16) == 0 else tm
    return [slice(r, r + rc) for r in range(0, tm, rc)]


def _mm_kernel(*refs, n_extra, epilogue):
    a_ref, w_ref = refs[0], refs[1]
    extras = refs[2:2 + n_extra]
    o_ref = refs[2 + n_extra]
    chunks = _row_chunks(a_ref.shape[0])
    acc = _dot(a_ref[chunks[0], :], w_ref[...])
    for c, rows in enumerate(chunks):
        nxt = _dot(a_ref[chunks[c + 1], :], w_ref[...]) if c + 1 < len(chunks) else None
        epilogue(acc, rows, o_ref, *extras)
        acc = nxt


def matmul(a, w, epilogue, extras, out_dtype, name, tm=1024, tn=512):
    m, k = a.shape
    n = w.shape[1]
    tm, tn = min(tm, m), min(tn, n)
    specs = [pl.BlockSpec((tm, k), lambda i, j: (i, 0)), pl.BlockSpec((k, tn), lambda i, j: (0, j))]
    args = [a, w]
    for arr, kind in extras:
        if kind == "tile":
            specs.append(pl.BlockSpec((tm, tn), lambda i, j: (i, j)))
        elif kind == "row":
            specs.append(pl.BlockSpec((1, tn), lambda i, j: (0, j)))
        elif kind == "mk":
            specs.append(pl.BlockSpec((tm, arr.shape[1]), lambda i, j: (i, 0)))
        elif kind == "kn":
            specs.append(pl.BlockSpec((arr.shape[0], tn), lambda i, j: (0, j)))
        else:
            raise ValueError(kind)
        args.append(arr)
    return pl.pallas_call(
        functools.partial(_mm_kernel, n_extra=len(extras), epilogue=epilogue),
        grid=(m // tm, n // tn),
        in_specs=specs,
        out_specs=pl.BlockSpec((tm, tn), lambda i, j: (i, j)),
        out_shape=jax.ShapeDtypeStruct((m, n), out_dtype),
        compiler_params=_cp(("parallel", "arbitrary")),
        name=name,
    )(*args)


def _epi_plain(acc, rows, o_ref):
    o_ref[rows, :] = acc.astype(o_ref.dtype)


def _epi_headnorm(acc, rows, o_ref, g_ref):
    for c in range(acc.shape[1] // DSA_HD):
        sl = slice(c * DSA_HD, (c + 1) * DSA_HD)
        blk = acc[:, sl]
        ms = jnp.mean(blk * blk, axis=-1, keepdims=True)
        o_ref[rows, sl] = (blk * lax.rsqrt(ms + EPS) * g_ref[:, sl]).astype(o_ref.dtype)


def _epi_sigmoid_bias(acc, rows, o_ref, b_ref):
    o_ref[rows, :] = _sigmoid(acc + b_ref[...]).astype(o_ref.dtype)


def _epi_residual(acc, rows, o_ref, x_ref):
    o_ref[rows, :] = (x_ref[rows, :] + acc).astype(o_ref.dtype)


def _epi_ple(acc, rows, o_ref, b_ref, x_ref, p_ref, wp_ref):
    ple = _dot(p_ref[rows, :], wp_ref[...])
    o_ref[rows, :] = (x_ref[rows, :] + _sigmoid(acc + b_ref[...]) * ple).astype(o_ref.dtype)


def _mix_kernel(oa_ref, ob_ref, wa_ref, wb_ref, ga_ref, gb_ref, o_ref):
    chunks = _row_chunks(oa_ref.shape[0])

    def both(rows):
        return _dot(oa_ref[rows, :], wa_ref[...]), _dot(ob_ref[rows, :], wb_ref[...])

    ya, yb = both(chunks[0])
    for c, rows in enumerate(chunks):
        nxt = both(chunks[c + 1]) if c + 1 < len(chunks) else None
        o_ref[rows, :] = (ga_ref[rows, :].astype(F32) * ya + gb_ref[rows, :].astype(F32) * yb).astype(o_ref.dtype)
        if nxt is not None:
            ya, yb = nxt


def mix(oa, ob, wa, wb, gates, tm=1024, tn=512):
    m, k = oa.shape
    n = wa.shape[1]
    tm, tn = min(tm, m), min(tn, n)
    nj = n // tn
    return pl.pallas_call(
        _mix_kernel,
        grid=(m // tm, nj),
        in_specs=[
            pl.BlockSpec((tm, k), lambda i, j: (i, 0)),
            pl.BlockSpec((tm, k), lambda i, j: (i, 0)),
            pl.BlockSpec((k, tn), lambda i, j: (0, j)),
            pl.BlockSpec((k, tn), lambda i, j: (0, j)),
            pl.BlockSpec((tm, tn), lambda i, j: (i, j)),
            pl.BlockSpec((tm, tn), lambda i, j: (i, j + nj)),
        ],
        out_specs=pl.BlockSpec((tm, tn), lambda i, j: (i, j)),
        out_shape=jax.ShapeDtypeStruct((m, n), BF16),
        compiler_params=_cp(("parallel", "arbitrary")),
        name="mix",
    )(oa, ob, wa, wb, gates, gates)


def _gla_kernel(q_ref, k_ref, v_ref, gr_ref, zs_ref, wa_ref, ba_ref, gn_ref, o_ref, st_ref, *, nchunk):
    @pl.when(pl.program_id(2) == 0)
    def _():
        st_ref[...] = jnp.zeros_like(st_ref)

    glr = zs_ref[:, 0:GLA_GATE_RANK].astype(BF16)
    row = lax.broadcasted_iota(I32, (CHUNK, CHUNK), 0)
    col = lax.broadcasted_iota(I32, (CHUNK, CHUNK), 1)
    causal = row >= col
    tril = jnp.where(causal, 1.0, 0.0).astype(BF16)
    gn = gn_ref[...]
    mid = CHUNK // 2 - 1
    nh = q_ref.shape[1] // GLA_DK
    sls = [slice(c * CHUNK, (c + 1) * CHUNK) for c in range(nchunk)]
    items = [(hh, c) for hh in range(nh) for c in range(nchunk)]

    def kcols(hh):
        return slice(hh * GLA_DK, (hh + 1) * GLA_DK)

    def vcols(hh):
        return slice(hh * GLA_DV, (hh + 1) * GLA_DV)

    g_hi, g_lo = [], []
    for hh in range(nh):
        logit = _dot(glr, wa_ref[hh]) + ba_ref[hh]
        g = (jnp.minimum(logit, 0.0) - jnp.log(1.0 + jnp.exp(-jnp.abs(logit)))) * (1.0 / GLA_GATE_TEMP)
        hi = g.astype(BF16)
        g_hi.append(hi)
        g_lo.append((g - hi.astype(F32)).astype(BF16))
    cums = {(hh, c): _dot(tril, g_hi[hh][sls[c]]) + _dot(tril, g_lo[hh][sls[c]]) for hh, c in items}
    qts, kts, qgs, kds, decs = {}, {}, {}, {}, {}
    for hh, c in items:
        cum = cums[hh, c]
        c0 = cum[mid:mid + 1, :]
        gl = cum[CHUNK - 1:CHUNK, :]
        qt = q_ref[sls[c], kcols(hh)].astype(F32) * (GLA_DK ** -0.5) * jnp.exp(cum - c0)
        kt = k_ref[sls[c], kcols(hh)].astype(F32) * jnp.exp(c0 - cum)
        qts[hh, c] = qt.astype(BF16)
        kts[hh, c] = kt.astype(BF16)
        qgs[hh, c] = (qt * jnp.exp(c0)).astype(BF16)
        kds[hh, c] = (kt * jnp.exp(gl - c0)).astype(BF16)
        decs[hh, c] = jnp.exp(gl)
    scs = {it: _dot_nt(qts[it], kts[it]) for it in items}
    kvs = {(hh, c): _dot_tn(v_ref[sls[c], vcols(hh)], kds[hh, c]) for hh, c in items}
    intra = {(hh, c): _dot(jnp.where(causal, scs[hh, c], 0.0).astype(BF16), v_ref[sls[c], vcols(hh)])
             for hh, c in items}
    sts = {}
    for hh in range(nh):
        st = st_ref[hh]
        for c in range(nchunk):
            sts[hh, c] = st.astype(BF16)
            st = st * decs[hh, c] + kvs[hh, c]
        st_ref[hh] = st
    inter = {it: _dot_nt(qgs[it], sts[it]) for it in items}
    for hh, c in items:
        o = intra[hh, c] + inter[hh, c]
        ms = jnp.mean(o * o, axis=-1, keepdims=True)
        gr = gr_ref[sls[c], vcols(hh)].astype(F32)
        o_ref[sls[c], vcols(hh)] = (o * lax.rsqrt(ms + EPS) * gn * (gr * _sigmoid(gr))).astype(o_ref.dtype)


def gla(zg, zs, wa, ba, gn, b, s):
    ct = min(GLA_CT, s)
    ns = s // ct
    hps = GLA_HEADS_PER_STEP
    kw, vw = hps * GLA_DK, hps * GLA_DV
    k_blk0 = GLA_QK // kw
    v_blk0 = (2 * GLA_QK) // vw
    gr_blk0 = (2 * GLA_QK + GLA_V) // vw
    return pl.pallas_call(
        functools.partial(_gla_kernel, nchunk=ct // CHUNK),
        grid=(b, GLA_HEADS // hps, ns),
        in_specs=[
            pl.BlockSpec((ct, kw), lambda bi, hp, si: (bi * ns + si, hp)),
            pl.BlockSpec((ct, kw), lambda bi, hp, si: (bi * ns + si, k_blk0 + hp)),
            pl.BlockSpec((ct, vw), lambda bi, hp, si: (bi * ns + si, v_blk0 + hp)),
            pl.BlockSpec((ct, vw), lambda bi, hp, si: (bi * ns + si, gr_blk0 + hp)),
            pl.BlockSpec((ct, LANES), lambda bi, hp, si: (bi * ns + si, 0)),
            pl.BlockSpec((hps, GLA_GATE_RANK, GLA_DK), lambda bi, hp, si: (hp, 0, 0)),
            pl.BlockSpec((hps, 1, GLA_DK), lambda bi, hp, si: (hp, 0, 0)),
            pl.BlockSpec((1, GLA_DV), lambda bi, hp, si: (0, 0)),
        ],
        out_specs=pl.BlockSpec((ct, vw), lambda bi, hp, si: (bi * ns + si, hp)),
        out_shape=jax.ShapeDtypeStruct((b * s, GLA_V), BF16),
        scratch_shapes=[pltpu.VMEM((hps, GLA_DV, GLA_DK), F32)],
        compiler_params=_cp(("parallel", "parallel", "arbitrary")),
        name="gla",
    )(zg, zg, zg, zg, zs, wa, ba, gn)


def _order_key(x):
    bits = lax.bitcast_convert_type(x, I32)
    return bits ^ ((bits >> 31) & 0x7FFFFFFF)


def _idx_kernel(iq_ref, w_ref, ik_ref, mask_ref, hi_ref, lo_ref, seg_ref, iqh_ref, *, qg, s):
    g = pl.program_id(1)
    nt = g + 1
    w = w_ref[...] * ((IDX_HEADS ** -0.5) * (IDX_DIM ** -0.5))
    kchunk = lax.broadcasted_iota(I32, (qg, qg), 0) // CHUNK
    qchunk = lax.broadcasted_iota(I32, (qg, qg), 1) // CHUNK
    adm_diag = kchunk <= qchunk

    half = qg // 2
    for j in range(IDX_HEADS):
        iqh_ref[j] = iq_ref[:, j * IDX_DIM:(j + 1) * IDX_DIM]

    step = 2 if (s // qg) % 2 == 0 else 1
    nit = (nt + step - 1) // step
    span = step * qg

    def score_tile(p, carry):
        for u in range(step):
            t = p * step + u
            for hf in range(2):
                off = pl.multiple_of(t * qg + hf * half, half)
                ikt = ik_ref[pl.ds(off, half), :]
                acc = jnp.zeros((half, qg), F32)
                for j in range(IDX_HEADS):
                    acc = acc + jnp.maximum(_dot_nt(ikt, iqh_ref[j]), 0.0) * w[j:j + 1, :]
                adm = jnp.logical_or(t < g, jnp.logical_and(t == g, adm_diag[hf * half:(hf + 1) * half, :]))
                acc = jnp.where(adm, acc, -jnp.inf)
                key = _order_key(acc)
                hi_ref[pl.ds(off, half), :] = (key >> 16).astype(I16)
                lo_ref[pl.ds(off, half), :] = ((key & 0xFFFF) + I16_MIN).astype(I16)
                rows = slice(hf * half, (hf + 1) * half)
                seg_ref[rows, :] = jnp.maximum(seg_ref[rows, :], acc)
        return carry

    seg_ref[...] = jnp.full((qg, qg), -jnp.inf, F32)
    lax.fori_loop(0, nit, score_tile, 0)
    segmax = seg_ref[...]

    topk = min(INDEX_TOPK, s // 4)
    kf = float(topk)
    lb = _order_key(jnp.min(segmax, axis=0, keepdims=True))
    mx = _order_key(jnp.max(segmax, axis=0, keepdims=True))
    qcol = lax.broadcasted_iota(I32, (1, qg), 1)
    n_adm = g * qg + (qcol // CHUNK + 1) * CHUNK
    small = n_adm <= topk

    def count_ge(ref, cand):
        c16 = cand.astype(I16)

        def body(p, acc):
            off = pl.multiple_of(p * span, span)
            m = jnp.where(ref[pl.ds(off, span), :] >= c16, jnp.int16(1), jnp.int16(0))
            for r in range(span // PACK16):
                acc = acc + m[r * PACK16:(r + 1) * PACK16, :]
            return acc

        acc = lax.fori_loop(0, nit, body, jnp.zeros((PACK16, qg), I16))
        return jnp.sum(acc.astype(F32), axis=0, keepdims=True)

    def bisect(ref, lo0, hi0, target, npass):
        def body(_, c):
            lo, hi, c_hi = c
            mid = jnp.right_shift(lo + hi, 1)
            cnt = count_ge(ref, mid)
            up = cnt >= target
            return jnp.where(up, mid, lo), jnp.where(up, hi, mid), jnp.where(up, c_hi, cnt)

        lo, _, c_hi = lax.fori_loop(0, npass, body, (lo0, hi0, jnp.zeros((1, qg), F32)))
        return lo, c_hi

    a_lo0 = jnp.right_shift(lb, 16)
    a_hi0 = jnp.right_shift(mx, 16) + 1
    width = jnp.where(small, 1, a_hi0 - a_lo0)
    nbits = jnp.zeros((1, qg), F32)
    for b in range(17):
        nbits = nbits + jnp.where(width > (1 << b), 1.0, 0.0)
    a_lo, ca_hi = bisect(hi_ref, a_lo0, a_hi0, kf, jnp.max(nbits).astype(I32))
    a16 = jnp.where(small, I16_MAX, a_lo).astype(I16)

    def keep_equal_hi(p, carry):
        off = pl.multiple_of(p * span, span)
        lo_ref[pl.ds(off, span), :] = jnp.where(hi_ref[pl.ds(off, span), :] == a16, lo_ref[pl.ds(off, span), :],
                                                jnp.int16(I16_MIN))
        return carry

    lax.fori_loop(0, nit, keep_equal_hi, 0)

    b_lo, _ = bisect(lo_ref, jnp.full((1, qg), I16_MIN, I32), jnp.full((1, qg), I16_MAX + 1, I32),
                     kf - ca_hi, 16)
    whole = b_lo == I16_MIN
    thr_h = jnp.where(small, jnp.int32(KEY_NEG_INF >> 16), jnp.where(whole, a_lo - 1, a_lo)).astype(I16)
    thr_l = jnp.where(jnp.logical_or(small, whole), I16_MAX, b_lo).astype(I16)

    def write_tile(p, acc):
        off = pl.multiple_of(p * span, span)
        one, nil = jnp.int16(1), jnp.int16(0)
        sel = jnp.where(hi_ref[pl.ds(off, span), :] > thr_h, one,
                        jnp.where(lo_ref[pl.ds(off, span), :] >= thr_l, one, nil))
        mask_ref[pl.ds(off, span), :] = jnp.where(sel > nil, jnp.zeros((span, qg), BF16),
                                                  jnp.full((span, qg), NEG_BIG, BF16))
        for r in range(span // PACK16):
            acc = acc + sel[r * PACK16:(r + 1) * PACK16, :]
        return acc

    n_sel = lax.fori_loop(0, nit, write_tile, jnp.zeros((PACK16, qg), I16))
    n_sel = jnp.sum(n_sel.astype(F32), axis=0, keepdims=True)
    excess = jnp.where(small, 0.0, n_sel - kf)

    @pl.when(jnp.max(excess) > 0.5)
    def _():
        h16 = a_lo.astype(I16)
        l16 = b_lo.astype(I16)

        def tied(off):
            same_lo = jnp.where(lo_ref[pl.ds(off, qg), :] == l16, jnp.ones((qg, qg), BF16), jnp.zeros((qg, qg), BF16))
            return jnp.where(hi_ref[pl.ds(off, qg), :] == h16, same_lo, jnp.zeros((qg, qg), BF16))

        def count_tied(t, acc):
            eq = tied(pl.multiple_of(t * qg, qg)).astype(F32)
            return acc + jnp.sum(eq.reshape(qg // 8, 8, qg), axis=0)

        n_eq = jnp.sum(lax.fori_loop(0, nit * step, count_tied, jnp.zeros((8, qg), F32)), axis=0, keepdims=True)
        need_eq = jnp.where(small, 1e9, kf - (n_sel - n_eq))
        krow = lax.broadcasted_iota(I32, (qg, qg), 0)
        kcol = lax.broadcasted_iota(I32, (qg, qg), 1)
        tril_incl = jnp.where(krow >= kcol, 1.0, 0.0).astype(BF16)

        def drop_late_ties(t, run):
            off = pl.multiple_of(t * qg, qg)
            eq = tied(off)
            rank = _dot(tril_incl, eq) + run
            late = jnp.where(rank > need_eq, 1.0, 0.0).astype(BF16) * eq
            old = mask_ref[pl.ds(off, qg), :]
            mask_ref[pl.ds(off, qg), :] = jnp.where(late > 0, jnp.full((qg, qg), NEG_BIG, BF16), old)
            return rank[qg - 1:qg, :]

        lax.fori_loop(0, nit * step, drop_late_ties, jnp.zeros((1, qg), F32))

    def fill_tile(t, carry):
        off = pl.multiple_of(t * qg, qg)
        mask_ref[pl.ds(off, qg), :] = jnp.full((qg, qg), NEG_BIG, BF16)
        return carry

    lax.fori_loop(nit * step, s // qg, fill_tile, 0)


def dsa_index(zvi, w_hm, ik, b, s):
    qg = min(IDX_QG, s)
    ng = s // qg
    return pl.pallas_call(
        functools.partial(_idx_kernel, qg=qg, s=s),
        grid=(b, ng),
        in_specs=[
            pl.BlockSpec((qg, IDX_Q), lambda bi, gi: (bi * ng + gi, DSA_W // IDX_Q)),
            pl.BlockSpec((None, IDX_HEADS, qg), lambda bi, gi: (bi, 0, gi)),
            pl.BlockSpec((None, s, IDX_DIM), lambda bi, gi: (bi, 0, 0)),
        ],
        out_specs=pl.BlockSpec((None, s, qg), lambda bi, gi: (bi, 0, gi)),
        out_shape=jax.ShapeDtypeStruct((b, s, s), BF16),
        scratch_shapes=[pltpu.VMEM((s, qg), I16), pltpu.VMEM((s, qg), I16), pltpu.VMEM((qg, qg), F32),
                        pltpu.VMEM((IDX_HEADS, qg, IDX_DIM), BF16)],
        compiler_params=_cp(("parallel", "arbitrary")),
        name="dsa_index",
    )(zvi, w_hm, ik)


def _attn_kernel(qi_ref, ki_ref, q_ref, k_ref, vt_ref, mask_ref, nbd_ref, nbp_ref, o_ref, m_ref, acc_ref):
    p = pl.program_id(1)
    qi = qi_ref[p]
    ki = ki_ref[p]

    @pl.when(ki == 0)
    def _():
        m_ref[...] = jnp.full_like(m_ref, NEG_BIG)
        acc_ref[...] = jnp.zeros_like(acc_ref)

    def hs(h):
        return slice(h * DSA_HD, (h + 1) * DSA_HD)

    def vs(h):
        return slice(h * ATT_VROWS, (h + 1) * ATT_VROWS)

    nrow = 2
    rch = q_ref.shape[0] // nrow
    kh = k_ref.shape[0] // 2

    def heads(bias_of_head):
        def score_chunk(h, r):
            return _dot_nt(k_ref[r * rch:(r + 1) * rch, hs(h)], q_ref[:, hs(h)])

        def value_half(h, pr, half):
            return _dot(vt_ref[vs(h), half * kh:(half + 1) * kh], pr)

        sc = {0: [score_chunk(0, r) for r in range(nrow)], 1: [score_chunk(1, r) for r in range(nrow)]}
        prev = None
        for h in range(DSA_HEADS + 1):
            if h < DSA_HEADS:
                bias = bias_of_head(h)
                cur = sc.pop(h)
                sb, cm = [], []
                nxt = []
                for r in range(nrow):
                    sb.append(cur[r] + bias[r * rch:(r + 1) * rch, :])
                    cm.append(jnp.max(sb[r], axis=0, keepdims=True))
                    if h + 2 < DSA_HEADS:
                        nxt.append(score_chunk(h + 2, r))
                if nxt:
                    sc[h + 2] = nxt
                m_old = m_ref[h]
                m_new = m_old
                for c in cm:
                    m_new = jnp.maximum(m_new, c)
                alpha = jnp.exp2(m_old - m_new)
                m_ref[h] = m_new
            pv = []
            pr = []
            for r in range(nrow):
                if h < DSA_HEADS:
                    pr.append(jnp.exp2(sb[r] - m_new).astype(BF16))
                if prev is not None and r % (nrow // 2) == 0:
                    pv.append(value_half(prev[0], prev[2][r // (nrow // 2)], r // (nrow // 2)))
            if prev is not None:
                ph = prev[0]
                acc_ref[vs(ph), :] = prev[1] * acc_ref[vs(ph), :] + pv[0] + pv[1]
            if h < DSA_HEADS:
                prev = (h, alpha, [jnp.concatenate(pr[:nrow // 2], axis=0), jnp.concatenate(pr[nrow // 2:], axis=0)])

    @pl.when(ki < qi - 1)
    def _():
        mb = mask_ref[...].astype(F32)
        heads(lambda h: mb)

    @pl.when(ki == qi - 1)
    def _():
        mb = mask_ref[...].astype(F32)
        heads(lambda h: mb + nbp_ref[h].astype(F32))

    @pl.when(ki == qi)
    def _():
        mb = mask_ref[...].astype(F32)
        heads(lambda h: mb + nbd_ref[h].astype(F32))
        for h in range(DSA_HEADS):
            num = acc_ref[h * ATT_VROWS:h * ATT_VROWS + DSA_HD, :]
            den = acc_ref[h * ATT_VROWS + DSA_HD:h * ATT_VROWS + DSA_HD + 1, :]
            o_ref[hs(h), :] = (num / den).astype(o_ref.dtype)


def dsa_attention(zqk, vt, mask, nbd, nbp, b, s):
    t = min(ATT_T, s)
    ng = s // t
    pairs = [(qi, ki) for qi in range(ng) for ki in range(qi + 1)]
    qi_tab = jnp.asarray([p[0] for p in pairs], I32)
    ki_tab = jnp.asarray([p[1] for p in pairs], I32)
    grid_spec = pltpu.PrefetchScalarGridSpec(
        num_scalar_prefetch=2,
        grid=(b, len(pairs)),
        in_specs=[
            pl.BlockSpec((t, DSA_W), lambda bi, p, qt, kt: (bi * ng + qt[p], 0)),
            pl.BlockSpec((t, DSA_W), lambda bi, p, qt, kt: (bi * ng + kt[p], 1)),
            pl.BlockSpec((None, DSA_HEADS * ATT_VROWS, t), lambda bi, p, qt, kt: (bi, 0, kt[p])),
            pl.BlockSpec((None, t, t), lambda bi, p, qt, kt: (bi, kt[p], qt[p])),
            pl.BlockSpec((DSA_HEADS, t, t), lambda bi, p, qt, kt: (0, 0, 0)),
            pl.BlockSpec((DSA_HEADS, t, t), lambda bi, p, qt, kt: (0, 0, 0)),
        ],
        out_specs=pl.BlockSpec((None, DSA_W, t), lambda bi, p, qt, kt: (bi, 0, qt[p])),
        scratch_shapes=[
            pltpu.VMEM((DSA_HEADS, 1, t), F32),
            pltpu.VMEM((DSA_HEADS * ATT_VROWS, t), F32),
        ],
    )
    return pl.pallas_call(
        _attn_kernel,
        grid_spec=grid_spec,
        out_shape=jax.ShapeDtypeStruct((b, DSA_W, s), BF16),
        compiler_params=_cp(("parallel", "arbitrary")),
        name="dsa_attention",
    )(qi_tab, ki_tab, zqk, zqk, vt, mask, nbd, nbp)


def _t5_bucket(rel):
    half = REL_BUCKETS // 2
    exact = half // 2
    sign = jnp.where(rel > 0, half, 0)
    n = jnp.abs(rel)
    nf = jnp.maximum(n, 1).astype(F32)
    large = exact + (jnp.log(nf / exact) / math.log(REL_MAX_DIST / exact) * (half - exact)).astype(I32)
    large = jnp.minimum(large, half - 1)
    return sign + jnp.where(n < exact, n, large)


def _near_bias_tables(rel_bias, t):
    kpos = jnp.arange(t, dtype=I32)[:, None]
    qpos = jnp.arange(t, dtype=I32)[None, :]
    shifted = (rel_bias - rel_bias[REL_BUCKETS // 2 - 1]) * LOG2E

    def table(rel):
        onehot = (_t5_bucket(rel)[:, :, None] == jnp.arange(REL_BUCKETS, dtype=I32)).astype(F32)
        return jnp.einsum("kqb,bh->hkq", onehot, shifted, precision=lax.Precision.HIGHEST).astype(BF16)

    return table(kpos - qpos), table(kpos - t - qpos)


def _router_kernel(x_ref, g_ref, w_ref, b_ref, hn_ref, ri_ref, rw_ref):
    x = x_ref[...]
    ms = jnp.mean(x * x, axis=-1, keepdims=True)
    hn = x * lax.rsqrt(ms + EPS) * g_ref[...]
    hn_ref[...] = hn
    logits = _dot(hn.astype(BF16), w_ref[...]) + b_ref[...]
    lane = lax.broadcasted_iota(I32, logits.shape, 1)
    lanef = lane.astype(F32)
    big = 1e9
    is_grp = lane < N_GROUPS
    gl = jnp.where(is_grp, logits, -jnp.inf)
    gmax = jnp.max(gl, axis=-1, keepdims=True)
    grp = jnp.min(jnp.where(gl == gmax, lanef, big), axis=-1, keepdims=True)
    gsum = jnp.sum(jnp.where(is_grp, jnp.exp(logits - gmax), 0.0), axis=-1, keepdims=True)
    grp_w = 1.0 / gsum
    elo = N_GROUPS + grp * EXPERTS_PER_GROUP
    in_grp = jnp.logical_and(lanef >= elo, lanef < elo + EXPERTS_PER_GROUP)
    el = jnp.where(in_grp, logits, -jnp.inf)
    v1 = jnp.max(el, axis=-1, keepdims=True)
    i1 = jnp.min(jnp.where(el == v1, lanef, big), axis=-1, keepdims=True)
    el2 = jnp.where(lanef == i1, -jnp.inf, el)
    v2 = jnp.max(el2, axis=-1, keepdims=True)
    i2 = jnp.min(jnp.where(el2 == v2, lanef, big), axis=-1, keepdims=True)
    e2 = jnp.exp(v2 - v1)
    w1 = grp_w * (1.0 / (1.0 + e2))
    w2 = grp_w * (e2 / (1.0 + e2))
    ri_ref[...] = jnp.where(lane == 0, i1 - N_GROUPS, jnp.where(lane == 1, i2 - N_GROUPS, 0.0)).astype(I32)
    rw_ref[...] = jnp.where(lane == 0, w1, jnp.where(lane == 1, w2, 0.0))


def router(x1, g, w_r, b_r, tm=256):
    m, d = x1.shape
    tm = min(tm, m)
    return pl.pallas_call(
        _router_kernel,
        grid=(m // tm,),
        in_specs=[
            pl.BlockSpec((tm, d), lambda i: (i, 0)),
            pl.BlockSpec((1, d), lambda i: (0, 0)),
            pl.BlockSpec((d, LANES), lambda i: (0, 0)),
            pl.BlockSpec((1, LANES), lambda i: (0, 0)),
        ],
        out_specs=[
            pl.BlockSpec((tm, d), lambda i: (i, 0)),
            pl.BlockSpec((tm, LANES), lambda i: (i, 0)),
            pl.BlockSpec((tm, LANES), lambda i: (i, 0)),
        ],
        out_shape=[
            jax.ShapeDtypeStruct((m, d), F32),
            jax.ShapeDtypeStruct((m, LANES), I32),
            jax.ShapeDtypeStruct((m, LANES), F32),
        ],
        compiler_params=_cp(("parallel",)),
        name="router",
    )(x1, g.reshape(1, d).astype(F32), w_r, b_r)


def _rank_kernel(e_ref, rank_ref, cnt_ref, run_ref, *, rb):
    @pl.when(pl.program_id(0) == 0)
    def _():
        run_ref[...] = jnp.zeros_like(run_ref)

    e = e_ref[...]
    sub = lax.broadcasted_iota(I32, (N_EXPERTS, rb), 0)
    hit = sub == e
    oh = jnp.where(hit, 1.0, 0.0)
    a = lax.broadcasted_iota(I32, (rb, rb), 0)
    bcol = lax.broadcasted_iota(I32, (rb, rb), 1)
    upper = jnp.where(a < bcol, 1.0, 0.0).astype(BF16)
    prefix = _dot(oh.astype(BF16), upper)
    run = run_ref[...]
    rank = jnp.sum(jnp.where(hit, prefix + run[:, 0:1], 0.0), axis=0, keepdims=True)
    rank_ref[...] = rank.astype(I32)
    new = run + jnp.sum(oh, axis=1, keepdims=True)
    run_ref[...] = new
    cnt_ref[...] = new


def moe_rank(eid_flat):
    a = eid_flat.shape[0]
    rb = min(RANK_RB, a)
    nb = a // rb
    rank, cnt = pl.pallas_call(
        functools.partial(_rank_kernel, rb=rb),
        grid=(nb,),
        in_specs=[pl.BlockSpec((None, 1, rb), lambda i: (i, 0, 0))],
        out_specs=[
            pl.BlockSpec((None, 1, rb), lambda i: (i, 0, 0)),
            pl.BlockSpec((N_EXPERTS, LANES), lambda i: (0, 0)),
        ],
        out_shape=[
            jax.ShapeDtypeStruct((nb, 1, rb), I32),
            jax.ShapeDtypeStruct((N_EXPERTS, LANES), F32),
        ],
        scratch_shapes=[pltpu.VMEM((N_EXPERTS, LANES), F32)],
        compiler_params=_cp(("arbitrary",)),
        name="moe_rank",
    )(eid_flat.reshape(nb, 1, rb))
    return rank.reshape(a), cnt[:, 0].astype(I32)


def _row_copy(src, dst, sem):
    return pltpu.make_async_copy(src, dst, sem)


def _dispatch_kernel(dest_ref, hn_ref, xs_in_ref, xs_ref, sem, *, tm):
    del xs_in_ref
    base = pl.program_id(0) * (tm * TOP_K_INNER)

    def body(r, carry):
        for slot in range(TOP_K_INNER):
            d = dest_ref[base + r * TOP_K_INNER + slot]
            _row_copy(hn_ref.at[pl.ds(r, 1), :], xs_ref.at[pl.ds(d, 1), :], sem).start()
        return carry

    lax.fori_loop(0, tm, body, 0, unroll=8)
    for _ in range(TOP_K_INNER):
        _row_copy(hn_ref, xs_ref.at[pl.ds(0, tm), :], sem).wait()


def moe_dispatch(dest, hn, xs0):
    m, d = hn.shape
    tm = min(ROW_T, m)
    grid_spec = pltpu.PrefetchScalarGridSpec(
        num_scalar_prefetch=1,
        grid=(m // tm,),
        in_specs=[
            pl.BlockSpec((tm, d), lambda i, dst: (i, 0)),
            pl.BlockSpec(memory_space=pl.ANY),
        ],
        out_specs=pl.BlockSpec(memory_space=pl.ANY),
        scratch_shapes=[pltpu.SemaphoreType.DMA(())],
    )
    return pl.pallas_call(
        functools.partial(_dispatch_kernel, tm=tm),
        grid_spec=grid_spec,
        out_shape=jax.ShapeDtypeStruct(xs0.shape, xs0.dtype),
        input_output_aliases={2: 0},
        compiler_params=_cp(("arbitrary",), has_side_effects=True),
        name="moe_dispatch",
    )(dest, hn, xs0)


def _expert_kernel(be_ref, nv_ref, run_ref, nxt_ref, xs_ref, wg_hbm, wu_hbm, wd_hbm, y_ref,
                   wg_f, wu_f, wd_f, wg_bf, wu_bf, wd_bf, sem):
    i = pl.program_id(0)
    valid = i < nv_ref[0]
    e = be_ref[i]
    fresh = jnp.logical_and(valid, jnp.logical_or(i == 0, e != be_ref[jnp.maximum(i - 1, 0)]))
    slot = run_ref[i] % 2

    def weight_copies(expert, s_):
        return [pltpu.make_async_copy(wg_hbm.at[expert], wg_f.at[s_], sem.at[s_, 0]),
                pltpu.make_async_copy(wu_hbm.at[expert], wu_f.at[s_], sem.at[s_, 1]),
                pltpu.make_async_copy(wd_hbm.at[expert], wd_f.at[s_], sem.at[s_, 2])]

    @pl.when(i == 0)
    def _():
        for c in weight_copies(e, 0):
            c.start()

    @pl.when(fresh)
    def _():
        nxt = nxt_ref[i]

        @pl.when(nxt >= 0)
        def _():
            for c in weight_copies(nxt, 1 - slot):
                c.start()

        for c in weight_copies(e, slot):
            c.wait()
        wg_bf[...] = wg_f[slot].astype(BF16)
        wu_bf[...] = wu_f[slot].astype(BF16)
        wd_bf[...] = wd_f[slot].astype(BF16)

    @pl.when(valid)
    def _():
        xb = xs_ref[...].astype(BF16)
        hg = _dot(xb, wg_bf[...])
        hu = _dot(xb, wu_bf[...])
        act = (hg * _sigmoid(hg) * hu).astype(BF16)
        y_ref[...] = _dot(act, wd_bf[...])

    @pl.when(jnp.logical_not(valid))
    def _():
        y_ref[...] = jnp.zeros_like(y_ref)


def moe_experts(blk_expert, nvalid, has_rows, xs, w_gate, w_up, w_down):
    p, d = xs.shape
    nblk = p // MOE_BLK
    ff = w_gate.shape[2]
    first = jnp.concatenate([jnp.ones((1,), bool), blk_expert[1:] != blk_expert[:-1]])
    run = (jnp.cumsum(first.astype(I32)) - 1).astype(I32)
    eids = jnp.arange(N_EXPERTS, dtype=I32)
    later = jnp.logical_and(eids[None, :] > eids[:, None], has_rows[None, :])
    nxt_tab = jnp.min(jnp.where(later, eids[None, :], N_EXPERTS), axis=1)
    nxt_tab = jnp.where(nxt_tab >= N_EXPERTS, -1, nxt_tab)
    nxt = jnp.sum(jnp.where(blk_expert[:, None] == eids[None, :], nxt_tab[None, :], 0), axis=1).astype(I32)

    def blk(i, nv):
        return jnp.minimum(i, nv[0] - 1)

    grid_spec = pltpu.PrefetchScalarGridSpec(
        num_scalar_prefetch=4,
        grid=(nblk,),
        in_specs=[
            pl.BlockSpec((MOE_BLK, d), lambda i, be, nv, rn, nx: (blk(i, nv), 0)),
            pl.BlockSpec(memory_space=pl.ANY),
            pl.BlockSpec(memory_space=pl.ANY),
            pl.BlockSpec(memory_space=pl.ANY),
        ],
        out_specs=pl.BlockSpec((MOE_BLK, d), lambda i, be, nv, rn, nx: (i, 0)),
        scratch_shapes=[
            pltpu.VMEM((2, d, ff), F32), pltpu.VMEM((2, d, ff), F32), pltpu.VMEM((2, ff, d), F32),
            pltpu.VMEM((d, ff), BF16), pltpu.VMEM((d, ff), BF16), pltpu.VMEM((ff, d), BF16),
            pltpu.SemaphoreType.DMA((2, 3)),
        ],
    )
    return pl.pallas_call(
        _expert_kernel,
        grid_spec=grid_spec,
        out_shape=jax.ShapeDtypeStruct((p, d), F32),
        compiler_params=_cp(("arbitrary",)),
        name="moe_experts",
    )(blk_expert, nvalid, run, nxt, xs, w_gate, w_up, w_down)


def _combine_kernel(dest_ref, x_ref, rw_ref, g_ref, y_ref, o_ref, hn_ref, ybuf, sem, *, tm):
    i = pl.program_id(0)
    nsteps = pl.num_programs(0)

    def gather(step, buf):
        base = step * (tm * TOP_K_INNER)

        def body(r, carry):
            for slot in range(TOP_K_INNER):
                d = dest_ref[base + r * TOP_K_INNER + slot]
                _row_copy(y_ref.at[pl.ds(d, 1), :], ybuf.at[buf, slot, pl.ds(r, 1), :], sem.at[buf]).start()
            return carry

        lax.fori_loop(0, tm, body, 0, unroll=8)

    @pl.when(i == 0)
    def _():
        gather(0, 0)

    @pl.when(i + 1 < nsteps)
    def _():
        gather(i + 1, (i + 1) % 2)

    cur = i % 2
    for slot in range(TOP_K_INNER):
        _row_copy(y_ref.at[pl.ds(0, tm), :], ybuf.at[cur, slot], sem.at[cur]).wait()
    rw = rw_ref[...]
    o = x_ref[...] + rw[:, 0:1] * ybuf[cur, 0] + rw[:, 1:2] * ybuf[cur, 1]
    o_ref[...] = o
    ms = jnp.mean(o * o, axis=-1, keepdims=True)
    hn_ref[...] = (o * lax.rsqrt(ms + EPS) * g_ref[...]).astype(hn_ref.dtype)


def moe_combine(dest, x1, rw, g, y):
    m, d = x1.shape
    tm = min(ROW_T, m)
    grid_spec = pltpu.PrefetchScalarGridSpec(
        num_scalar_prefetch=1,
        grid=(m // tm,),
        in_specs=[
            pl.BlockSpec((tm, d), lambda i, dst: (i, 0)),
            pl.BlockSpec((tm, LANES), lambda i, dst: (i, 0)),
            pl.BlockSpec((1, d), lambda i, dst: (0, 0)),
            pl.BlockSpec(memory_space=pl.ANY),
        ],
        out_specs=[pl.BlockSpec((tm, d), lambda i, dst: (i, 0)), pl.BlockSpec((tm, d), lambda i, dst: (i, 0))],
        scratch_shapes=[pltpu.VMEM((2, TOP_K_INNER, tm, d), F32), pltpu.SemaphoreType.DMA((2,))],
    )
    return pl.pallas_call(
        functools.partial(_combine_kernel, tm=tm),
        grid_spec=grid_spec,
        out_shape=[jax.ShapeDtypeStruct((m, d), F32), jax.ShapeDtypeStruct((m, d), BF16)],
        compiler_params=_cp(("arbitrary",)),
        name="moe_combine",
    )(dest, x1, rw, g.reshape(1, d).astype(F32), y)


def _layer(x, p, norm_mix_g, w_in, gla_w_alpha, gla_b_alpha, gla_norm_g, w_out_gla, q_norm_g, k_norm_g, rel_bias,
           w_out_dsa, w_branch_gate, b_branch_gate, w_out, norm_ffn_g, w_group_router, b_group_router,
           w_expert_router, b_expert_router, w_exp_gate, w_exp_up, w_exp_down, norm_ple_g, w_ple_gate,
           b_ple_gate, w_ple_proj):
    b, s, d = x.shape
    t = b * s
    x2d = x.reshape(t, d)

    o_gq, o_gk, o_gv, o_gr = 0, GLA_QK, 2 * GLA_QK, 2 * GLA_QK + GLA_V
    o_glr = o_gr + GLA_V
    o_dq = o_glr + GLA_GATE_RANK
    o_dk, o_dv = o_dq + DSA_W, o_dq + 2 * DSA_W
    o_iq = o_dv + DSA_W
    o_ik = o_iq + IDX_Q
    o_iw = o_ik + IDX_DIM
    w_g = w_in[:, o_gq:o_glr].astype(BF16)
    w_qk = w_in[:, o_dq:o_dv].astype(BF16)
    w_vi = w_in[:, o_dv:o_ik].astype(BF16)
    pad = jnp.zeros((d, LANES - GLA_GATE_RANK - IDX_HEADS - IDX_DIM), F32)
    w_s = jnp.concatenate([w_in[:, o_glr:o_dq], w_in[:, o_iw:o_iw + IDX_HEADS], pad,
                           w_in[:, o_ik:o_iw]], axis=1).astype(BF16)

    h = rmsnorm(x2d, norm_mix_g, BF16)
    zg = matmul(h, w_g, _epi_plain, [], BF16, "proj_gla")
    qk_gain = jnp.concatenate([jnp.tile(q_norm_g * (DSA_HD ** -0.5 * LOG2E), DSA_HEADS),
                               jnp.tile(k_norm_g, DSA_HEADS)]).reshape(1, 2 * DSA_W).astype(F32)
    zqk = matmul(h, w_qk, _epi_headnorm, [(qk_gain, "row")], BF16, "proj_dsa_qk")
    zvi = matmul(h, w_vi, _epi_plain, [], BF16, "proj_dsa_vi")
    zs = matmul(h, w_s, _epi_plain, [], F32, "proj_small")
    gates = matmul(h, w_branch_gate.astype(BF16), _epi_sigmoid_bias,
                   [(b_branch_gate.reshape(1, -1).astype(F32), "row")], BF16, "branch_gates")

    wa = gla_w_alpha.reshape(GLA_GATE_RANK, GLA_HEADS, GLA_DK).transpose(1, 0, 2).astype(BF16)
    ba = gla_b_alpha.reshape(GLA_HEADS, 1, GLA_DK).astype(F32)
    o_a = gla(zg, zs, wa, ba, gla_norm_g.reshape(1, GLA_DV).astype(F32), b, s)

    ik = zs[:, LANES - IDX_DIM:].astype(BF16).reshape(b, s, IDX_DIM)
    w_hm = zs[:, GLA_GATE_RANK:GLA_GATE_RANK + IDX_HEADS].reshape(b, s, IDX_HEADS).transpose(0, 2, 1)
    mask = dsa_index(zvi, w_hm, ik, b, s)
    nbd, nbp = _near_bias_tables(rel_bias, min(ATT_T, s))
    vt = zvi[:, :DSA_W].reshape(b, s, DSA_HEADS, DSA_HD).transpose(0, 2, 3, 1)
    vt = jnp.concatenate([vt, jnp.ones((b, DSA_HEADS, ATT_VROWS - DSA_HD, s), BF16)], axis=2)
    vt = vt.reshape(b, DSA_HEADS * ATT_VROWS, s)
    o_b = dsa_attention(zqk, vt, mask, nbd, nbp, b, s).transpose(0, 2, 1).reshape(t, DSA_W)

    m = mix(o_a, o_b, w_out_gla.astype(BF16), w_out_dsa.astype(BF16), gates)
    x1 = matmul(m, w_out.astype(BF16), _epi_residual, [(x2d, "tile")], F32, "out_proj")

    w_r = jnp.concatenate([w_group_router, w_expert_router,
                           jnp.zeros((d, LANES - N_GROUPS - N_EXPERTS), F32)], axis=1).astype(BF16)
    b_r = jnp.concatenate([b_group_router, b_expert_router,
                           jnp.zeros((LANES - N_GROUPS - N_EXPERTS,), F32)]).reshape(1, LANES)
    hn, ri, rw = router(x1, norm_ffn_g, w_r, b_r)
    a = t * TOP_K_INNER
    eid = ri[:, :TOP_K_INNER].reshape(a)
    rank, counts = moe_rank(eid)
    padded = ((counts + MOE_BLK - 1) // MOE_BLK) * MOE_BLK
    pend = jnp.cumsum(padded)
    pstart = pend - padded
    dest = (pstart[eid] + rank).astype(I32)
    prow = a + N_EXPERTS * MOE_BLK
    nblk = prow // MOE_BLK
    blk_row0 = jnp.arange(nblk, dtype=I32)[:, None] * MOE_BLK
    blk_expert = jnp.minimum(jnp.sum((pend[None, :] <= blk_row0).astype(I32), axis=1), N_EXPERTS - 1)
    nvalid = (pend[-1:] // MOE_BLK).astype(I32)
    xs = moe_dispatch(dest, hn, jnp.zeros((prow, d), F32))
    y = moe_experts(blk_expert, nvalid, padded > 0, xs, w_exp_gate, w_exp_up, w_exp_down)
    x2, hp = moe_combine(dest, x1, rw, norm_ple_g, y)

    out = matmul(hp, w_ple_gate.astype(BF16), _epi_ple,
                 [(b_ple_gate.reshape(1, -1).astype(F32), "row"), (x2, "tile"),
                  (p.reshape(t, -1).astype(BF16), "mk"), (w_ple_proj.astype(BF16), "kn")],
                 F32, "ple")
    return out.reshape(b, s, d)


def kernel(x, p, norm_mix_g, w_in, gla_w_alpha, gla_b_alpha, gla_norm_g, w_out_gla, q_norm_g, k_norm_g, rel_bias, w_out_dsa, w_branch_gate, b_branch_gate, w_out, norm_ffn_g, w_group_router, b_group_router, w_expert_router, b_expert_router, w_exp_gate, w_exp_up, w_exp_down, norm_ple_g, w_ple_gate, b_ple_gate, w_ple_proj):
    depth = w_in.shape[0]
    for i in range(depth):
        x = _layer(x, p[i], norm_mix_g[i], w_in[i], gla_w_alpha[i], gla_b_alpha[i], gla_norm_g[i], w_out_gla[i],
                   q_norm_g[i], k_norm_g[i], rel_bias, w_out_dsa[i], w_branch_gate[i], b_branch_gate[i], w_out[i],
                   norm_ffn_g[i], w_group_router[i], b_group_router[i], w_expert_router[i], b_expert_router[i],
                   w_exp_gate[i], w_exp_up[i], w_exp_down[i], norm_ple_g[i], w_ple_gate[i], b_ple_gate[i],
                   w_ple_proj[i])
    return x
```

```python
import functools
import math

import jax
import jax.numpy as jnp
from jax import lax
from jax.experimental import pallas as pl
from jax.experimental.pallas import tpu as pltpu

F32 = jnp.float32
BF16 = jnp.bfloat16
I32 = jnp.int32
I16 = jnp.int16

EPS = 1e-6
CHUNK = 64
GLA_HEADS = 8
GLA_DK = 64
GLA_DV = 128
GLA_GATE_RANK = 16
GLA_GATE_TEMP = 16.0
GLA_QK = GLA_HEADS * GLA_DK
GLA_V = GLA_HEADS * GLA_DV
DSA_HEADS = 8
DSA_HD = 128
DSA_W = DSA_HEADS * DSA_HD
IDX_HEADS = 16
IDX_DIM = 64
IDX_Q = IDX_HEADS * IDX_DIM
INDEX_TOPK = 256
REL_BUCKETS = 32
REL_MAX_DIST = 128
N_GROUPS = 8
EXPERTS_PER_GROUP = 8
N_EXPERTS = N_GROUPS * EXPERTS_PER_GROUP
EXPERT_FF = 512
TOP_K_INNER = 2

LANES = 128
VMEM_LIMIT = 56 * 1024 * 1024
NEG_BIG = -1e30
LOG2E = math.log2(math.e)
KEY_NEG_INF = -2139095041
I16_MIN, I16_MAX = -32768, 32767
PACK16 = 16

GLA_CT = 1024
GLA_HEADS_PER_STEP = 2
IDX_QG = 256
ATT_T = 512
ATT_VROWS = DSA_HD + PACK16
MOE_BLK = 256
RANK_RB = 1024
ROW_T = 512
MM_SPLIT = 4


def _cp(sem, **kw):
    return pltpu.CompilerParams(dimension_semantics=sem, vmem_limit_bytes=VMEM_LIMIT, **kw)


def _dot(a, b):
    return jnp.dot(a, b, preferred_element_type=F32)


def _dot_nt(a, b):
    return lax.dot_general(a, b, (((1,), (1,)), ((), ())), preferred_element_type=F32)


def _dot_tn(a, b):
    return lax.dot_general(a, b, (((0,), (0,)), ((), ())), preferred_element_type=F32)


def _sigmoid(x):
    return 1.0 / (1.0 + jnp.exp(-x))


def _rmsnorm_kernel(x_ref, g_ref, o_ref):
    x = x_ref[...]
    ms = jnp.mean(x * x, axis=-1, keepdims=True)
    o_ref[...] = (x * lax.rsqrt(ms + EPS) * g_ref[...]).astype(o_ref.dtype)


def rmsnorm(x, g, out_dtype, tm=512):
    m, d = x.shape
    tm = min(tm, m)
    return pl.pallas_call(
        _rmsnorm_kernel,
        grid=(m // tm,),
        in_specs=[pl.BlockSpec((tm, d), lambda i: (i, 0)), pl.BlockSpec((1, d), lambda i: (0, 0))],
        out_specs=pl.BlockSpec((tm, d), lambda i: (i, 0)),
        out_shape=jax.ShapeDtypeStruct((m, d), out_dtype),
        compiler_params=_cp(("parallel",)),
        name="rmsnorm",
    )(x, g.reshape(1, d).astype(F32))


def _row_chunks(tm):
    rc = tm // MM_SPLIT if tm % (MM_SPLIT * PACK16) == 0 else tm
    return [slice(r, r + rc) for r in range(0, tm, rc)]


def _mm_kernel(*refs, n_extra, epilogue):
    a_ref, w_ref = refs[0], refs[1]
    extras = refs[2:2 + n_extra]
    o_ref = refs[2 + n_extra]
    chunks = _row_chunks(a_ref.shape[0])
    acc = _dot(a_ref[chunks[0], :], w_ref[...])
    for c, rows in enumerate(chunks):
        nxt = _dot(a_ref[chunks[c + 1], :], w_ref[...]) if c + 1 < len(chunks) else None
        epilogue(acc, rows, o_ref, *extras)
        acc = nxt


def matmul(a, w, epilogue, extras, out_dtype, name, tm=1024, tn=512):
    m, k = a.shape
    n = w.shape[1]
    tm, tn = min(tm, m), min(tn, n)
    specs = [pl.BlockSpec((tm, k), lambda i, j: (i, 0)), pl.BlockSpec((k, tn), lambda i, j: (0, j))]
    args = [a, w]
    for arr, kind in extras:
        if kind == "tile":
            specs.append(pl.BlockSpec((tm, tn), lambda i, j: (i, j)))
        elif kind == "row":
            specs.append(pl.BlockSpec((1, tn), lambda i, j: (0, j)))
        elif kind == "mk":
            specs.append(pl.BlockSpec((tm, arr.shape[1]), lambda i, j: (i, 0)))
        elif kind == "kn":
            specs.append(pl.BlockSpec((arr.shape[0], tn), lambda i, j: (0, j)))
        else:
            raise ValueError(kind)
        args.append(arr)
    return pl.pallas_call(
        functools.partial(_mm_kernel, n_extra=len(extras), epilogue=epilogue),
        grid=(m // tm, n // tn),
        in_specs=specs,
        out_specs=pl.BlockSpec((tm, tn), lambda i, j: (i, j)),
        out_shape=jax.ShapeDtypeStruct((m, n), out_dtype),
        compiler_params=_cp(("parallel", "arbitrary")),
        name=name,
    )(*args)


def _epi_plain(acc, rows, o_ref):
    o_ref[rows, :] = acc.astype(o_ref.dtype)


def _epi_headnorm(acc, rows, o_ref, g_ref):
    for c in range(acc.shape[1] // DSA_HD):
        sl = slice(c * DSA_HD, (c + 1) * DSA_HD)
        blk = acc[:, sl]
        ms = jnp.mean(blk * blk, axis=-1, keepdims=True)
        o_ref[rows, sl] = (blk * lax.rsqrt(ms + EPS) * g_ref[:, sl]).astype(o_ref.dtype)


def _epi_sigmoid_bias(acc, rows, o_ref, b_ref):
    o_ref[rows, :] = _sigmoid(acc + b_ref[...]).astype(o_ref.dtype)


def _epi_residual(acc, rows, o_ref, x_ref):
    o_ref[rows, :] = (x_ref[rows, :] + acc).astype(o_ref.dtype)


def _epi_ple(acc, rows, o_ref, b_ref, x_ref, p_ref, wp_ref):
    ple = _dot(p_ref[rows, :], wp_ref[...])
    o_ref[rows, :] = (x_ref[rows, :] + _sigmoid(acc + b_ref[...]) * ple).astype(o_ref.dtype)


def _mix_kernel(oa_ref, ob_ref, wa_ref, wb_ref, ga_ref, gb_ref, o_ref):
    chunks = _row_chunks(oa_ref.shape[0])

    def both(rows):
        return _dot(oa_ref[rows, :], wa_ref[...]), _dot(ob_ref[rows, :], wb_ref[...])

    ya, yb = both(chunks[0])
    for c, rows in enumerate(chunks):
        nxt = both(chunks[c + 1]) if c + 1 < len(chunks) else None
        o_ref[rows, :] = (ga_ref[rows, :].astype(F32) * ya + gb_ref[rows, :].astype(F32) * yb).astype(o_ref.dtype)
        if nxt is not None:
            ya, yb = nxt


def mix(oa, ob, wa, wb, gates, tm=1024, tn=512):
    m, k = oa.shape
    n = wa.shape[1]
    tm, tn = min(tm, m), min(tn, n)
    nj = n // tn
    return pl.pallas_call(
        _mix_kernel,
        grid=(m // tm, nj),
        in_specs=[
            pl.BlockSpec((tm, k), lambda i, j: (i, 0)),
            pl.BlockSpec((tm, k), lambda i, j: (i, 0)),
            pl.BlockSpec((k, tn), lambda i, j: (0, j)),
            pl.BlockSpec((k, tn), lambda i, j: (0, j)),
            pl.BlockSpec((tm, tn), lambda i, j: (i, j)),
            pl.BlockSpec((tm, tn), lambda i, j: (i, j + nj)),
        ],
        out_specs=pl.BlockSpec((tm, tn), lambda i, j: (i, j)),
        out_shape=jax.ShapeDtypeStruct((m, n), BF16),
        compiler_params=_cp(("parallel", "arbitrary")),
        name="mix",
    )(oa, ob, wa, wb, gates, gates)


def _gla_kernel(q_ref, k_ref, v_ref, gr_ref, zs_ref, wa_ref, ba_ref, gn_ref, o_ref, st_ref, *, nchunk):
    @pl.when(pl.program_id(2) == 0)
    def _():
        st_ref[...] = jnp.zeros_like(st_ref)

    glr = zs_ref[:, 0:GLA_GATE_RANK].astype(BF16)
    row = lax.broadcasted_iota(I32, (CHUNK, CHUNK), 0)
    col = lax.broadcasted_iota(I32, (CHUNK, CHUNK), 1)
    causal = row >= col
    tril = jnp.where(causal, 1.0, 0.0).astype(BF16)
    gn = gn_ref[...]
    mid = CHUNK // 2 - 1
    nh = q_ref.shape[1] // GLA_DK
    sls = [slice(c * CHUNK, (c + 1) * CHUNK) for c in range(nchunk)]
    items = [(hh, c) for hh in range(nh) for c in range(nchunk)]

    def kcols(hh):
        return slice(hh * GLA_DK, (hh + 1) * GLA_DK)

    def vcols(hh):
        return slice(hh * GLA_DV, (hh + 1) * GLA_DV)

    g_hi, g_lo = [], []
    for hh in range(nh):
        logit = _dot(glr, wa_ref[hh]) + ba_ref[hh]
        g = (jnp.minimum(logit, 0.0) - jnp.log(1.0 + jnp.exp(-jnp.abs(logit)))) * (1.0 / GLA_GATE_TEMP)
        hi = g.astype(BF16)
        g_hi.append(hi)
        g_lo.append((g - hi.astype(F32)).astype(BF16))
    cums = {(hh, c): _dot(tril, g_hi[hh][sls[c]]) + _dot(tril, g_lo[hh][sls[c]]) for hh, c in items}
    qts, kts, qgs, kds, decs = {}, {}, {}, {}, {}
    for hh, c in items:
        cum = cums[hh, c]
        c0 = cum[mid:mid + 1, :]
        gl = cum[CHUNK - 1:CHUNK, :]
        qt = q_ref[sls[c], kcols(hh)].astype(F32) * (GLA_DK ** -0.5) * jnp.exp(cum - c0)
        kt = k_ref[sls[c], kcols(hh)].astype(F32) * jnp.exp(c0 - cum)
        qts[hh, c] = qt.astype(BF16)
        kts[hh, c] = kt.astype(BF16)
        qgs[hh, c] = (qt * jnp.exp(c0)).astype(BF16)
        kds[hh, c] = (kt * jnp.exp(gl - c0)).astype(BF16)
        decs[hh, c] = jnp.exp(gl)
    scs = {it: _dot_nt(qts[it], kts[it]) for it in items}
    kvs = {(hh, c): _dot_tn(v_ref[sls[c], vcols(hh)], kds[hh, c]) for hh, c in items}
    intra = {(hh, c): _dot(jnp.where(causal, scs[hh, c], 0.0).astype(BF16), v_ref[sls[c], vcols(hh)])
             for hh, c in items}
    sts = {}
    for hh in range(nh):
        st = st_ref[hh]
        for c in range(nchunk):
            sts[hh, c] = st.astype(BF16)
            st = st * decs[hh, c] + kvs[hh, c]
        st_ref[hh] = st
    inter = {it: _dot_nt(qgs[it], sts[it]) for it in items}
    for hh, c in items:
        o = intra[hh, c] + inter[hh, c]
        ms = jnp.mean(o * o, axis=-1, keepdims=True)
        gr = gr_ref[sls[c], vcols(hh)].astype(F32)
        o_ref[sls[c], vcols(hh)] = (o * lax.rsqrt(ms + EPS) * gn * (gr * _sigmoid(gr))).astype(o_ref.dtype)


def gla(zg, zs, wa, ba, gn, b, s):
    ct = min(GLA_CT, s)
    ns = s // ct
    hps = GLA_HEADS_PER_STEP
    kw, vw = hps * GLA_DK, hps * GLA_DV
    k_blk0 = GLA_QK // kw
    v_blk0 = (2 * GLA_QK) // vw
    gr_blk0 = (2 * GLA_QK + GLA_V) // vw
    return pl.pallas_call(
        functools.partial(_gla_kernel, nchunk=ct // CHUNK),
        grid=(b, GLA_HEADS // hps, ns),
        in_specs=[
            pl.BlockSpec((ct, kw), lambda bi, hp, si: (bi * ns + si, hp)),
            pl.BlockSpec((ct, kw), lambda bi, hp, si: (bi * ns + si, k_blk0 + hp)),
            pl.BlockSpec((ct, vw), lambda bi, hp, si: (bi * ns + si, v_blk0 + hp)),
            pl.BlockSpec((ct, vw), lambda bi, hp, si: (bi * ns + si, gr_blk0 + hp)),
            pl.BlockSpec((ct, LANES), lambda bi, hp, si: (bi * ns + si, 0)),
            pl.BlockSpec((hps, GLA_GATE_RANK, GLA_DK), lambda bi, hp, si: (hp, 0, 0)),
            pl.BlockSpec((hps, 1, GLA_DK), lambda bi, hp, si: (hp, 0, 0)),
            pl.BlockSpec((1, GLA_DV), lambda bi, hp, si: (0, 0)),
        ],
        out_specs=pl.BlockSpec((ct, vw), lambda bi, hp, si: (bi * ns + si, hp)),
        out_shape=jax.ShapeDtypeStruct((b * s, GLA_V), BF16),
        scratch_shapes=[pltpu.VMEM((hps, GLA_DV, GLA_DK), F32)],
        compiler_params=_cp(("parallel", "parallel", "arbitrary")),
        name="gla",
    )(zg, zg, zg, zg, zs, wa, ba, gn)


def _order_key(x):
    bits = lax.bitcast_convert_type(x, I32)
    return bits ^ ((bits >> 31) & 0x7FFFFFFF)


def _idx_kernel(iq_ref, w_ref, ik_ref, mask_ref, hi_ref, lo_ref, seg_ref, iqh_ref, *, qg, s):
    g = pl.program_id(1)
    nt = g + 1
    w = w_ref[...] * ((IDX_HEADS ** -0.5) * (IDX_DIM ** -0.5))
    kchunk = lax.broadcasted_iota(I32, (qg, qg), 0) // CHUNK
    qchunk = lax.broadcasted_iota(I32, (qg, qg), 1) // CHUNK
    adm_diag = kchunk <= qchunk

    half = qg // 2
    for j in range(IDX_HEADS):
        iqh_ref[j] = iq_ref[:, j * IDX_DIM:(j + 1) * IDX_DIM]

    step = 2 if (s // qg) % 2 == 0 else 1
    nit = (nt + step - 1) // step
    span = step * qg

    def score_tile(p, carry):
        for u in range(step):
            t = p * step + u
            for hf in range(2):
                off = pl.multiple_of(t * qg + hf * half, half)
                ikt = ik_ref[pl.ds(off, half), :]
                acc = jnp.zeros((half, qg), F32)
                for j in range(IDX_HEADS):
                    acc = acc + jnp.maximum(_dot_nt(ikt, iqh_ref[j]), 0.0) * w[j:j + 1, :]
                adm = jnp.logical_or(t < g, jnp.logical_and(t == g, adm_diag[hf * half:(hf + 1) * half, :]))
                acc = jnp.where(adm, acc, -jnp.inf)
                key = _order_key(acc)
                hi_ref[pl.ds(off, half), :] = (key >> 16).astype(I16)
                lo_ref[pl.ds(off, half), :] = ((key & 0xFFFF) + I16_MIN).astype(I16)
                rows = slice(hf * half, (hf + 1) * half)
                seg_ref[rows, :] = jnp.maximum(seg_ref[rows, :], acc)
        return carry

    seg_ref[...] = jnp.full((qg, qg), -jnp.inf, F32)
    lax.fori_loop(0, nit, score_tile, 0)
    segmax = seg_ref[...]

    topk = min(INDEX_TOPK, s // 4)
    kf = float(topk)
    lb = _order_key(jnp.min(segmax, axis=0, keepdims=True))
    mx = _order_key(jnp.max(segmax, axis=0, keepdims=True))
    qcol = lax.broadcasted_iota(I32, (1, qg), 1)
    n_adm = g * qg + (qcol // CHUNK + 1) * CHUNK
    small = n_adm <= topk

    def count_ge(ref, cand):
        c16 = cand.astype(I16)

        def body(p, acc):
            off = pl.multiple_of(p * span, span)
            m = jnp.where(ref[pl.ds(off, span), :] >= c16, jnp.int16(1), jnp.int16(0))
            for r in range(span // PACK16):
                acc = acc + m[r * PACK16:(r + 1) * PACK16, :]
            return acc

        acc = lax.fori_loop(0, nit, body, jnp.zeros((PACK16, qg), I16))
        return jnp.sum(acc.astype(F32), axis=0, keepdims=True)

    def bisect(ref, lo0, hi0, target, npass):
        def body(_, c):
            lo, hi, c_hi = c
            mid = jnp.right_shift(lo + hi, 1)
            cnt = count_ge(ref, mid)
            up = cnt >= target
            return jnp.where(up, mid, lo), jnp.where(up, hi, mid), jnp.where(up, c_hi, cnt)

        lo, _, c_hi = lax.fori_loop(0, npass, body, (lo0, hi0, jnp.zeros((1, qg), F32)))
        return lo, c_hi

    a_lo0 = jnp.right_shift(lb, 16)
    a_hi0 = jnp.right_shift(mx, 16) + 1
    width = jnp.where(small, 1, a_hi0 - a_lo0)
    nbits = jnp.zeros((1, qg), F32)
    for b in range(17):
        nbits = nbits + jnp.where(width > (1 << b), 1.0, 0.0)
    a_lo, ca_hi = bisect(hi_ref, a_lo0, a_hi0, kf, jnp.max(nbits).astype(I32))
    a16 = jnp.where(small, I16_MAX, a_lo).astype(I16)

    def keep_equal_hi(p, carry):
        off = pl.multiple_of(p * span, span)
        lo_ref[pl.ds(off, span), :] = jnp.where(hi_ref[pl.ds(off, span), :] == a16, lo_ref[pl.ds(off, span), :],
                                                jnp.int16(I16_MIN))
        return carry

    lax.fori_loop(0, nit, keep_equal_hi, 0)

    b_lo, _ = bisect(lo_ref, jnp.full((1, qg), I16_MIN, I32), jnp.full((1, qg), I16_MAX + 1, I32),
                     kf - ca_hi, 16)
    whole = b_lo == I16_MIN
    thr_h = jnp.where(small, jnp.int32(KEY_NEG_INF >> 16), jnp.where(whole, a_lo - 1, a_lo)).astype(I16)
    thr_l = jnp.where(jnp.logical_or(small, whole), I16_MAX, b_lo).astype(I16)

    def write_tile(p, acc):
        off = pl.multiple_of(p * span, span)
        one, nil = jnp.int16(1), jnp.int16(0)
        sel = jnp.where(hi_ref[pl.ds(off, span), :] > thr_h, one,
                        jnp.where(lo_ref[pl.ds(off, span), :] >= thr_l, one, nil))
        mask_ref[pl.ds(off, span), :] = jnp.where(sel > nil, jnp.zeros((span, qg), BF16),
                                                  jnp.full((span, qg), NEG_BIG, BF16))
        for r in range(span // PACK16):
            acc = acc + sel[r * PACK16:(r + 1) * PACK16, :]
        return acc

    n_sel = lax.fori_loop(0, nit, write_tile, jnp.zeros((PACK16, qg), I16))
    n_sel = jnp.sum(n_sel.astype(F32), axis=0, keepdims=True)
    excess = jnp.where(small, 0.0, n_sel - kf)

    @pl.when(jnp.max(excess) > 0.5)
    def _():
        h16 = a_lo.astype(I16)
        l16 = b_lo.astype(I16)

        def tied(off):
            same_lo = jnp.where(lo_ref[pl.ds(off, qg), :] == l16, jnp.ones((qg, qg), BF16), jnp.zeros((qg, qg), BF16))
            return jnp.where(hi_ref[pl.ds(off, qg), :] == h16, same_lo, jnp.zeros((qg, qg), BF16))

        def count_tied(t, acc):
            eq = tied(pl.multiple_of(t * qg, qg)).astype(F32)
            return acc + jnp.sum(eq.reshape(qg // 8, 8, qg), axis=0)

        n_eq = jnp.sum(lax.fori_loop(0, nit * step, count_tied, jnp.zeros((8, qg), F32)), axis=0, keepdims=True)
        need_eq = jnp.where(small, 1e9, kf - (n_sel - n_eq))
        krow = lax.broadcasted_iota(I32, (qg, qg), 0)
        kcol = lax.broadcasted_iota(I32, (qg, qg), 1)
        tril_incl = jnp.where(krow >= kcol, 1.0, 0.0).astype(BF16)

        def drop_late_ties(t, run):
            off = pl.multiple_of(t * qg, qg)
            eq = tied(off)
            rank = _dot(tril_incl, eq) + run
            late = jnp.where(rank > need_eq, 1.0, 0.0).astype(BF16) * eq
            old = mask_ref[pl.ds(off, qg), :]
            mask_ref[pl.ds(off, qg), :] = jnp.where(late > 0, jnp.full((qg, qg), NEG_BIG, BF16), old)
            return rank[qg - 1:qg, :]

        lax.fori_loop(0, nit * step, drop_late_ties, jnp.zeros((1, qg), F32))

    def fill_tile(t, carry):
        off = pl.multiple_of(t * qg, qg)
        mask_ref[pl.ds(off, qg), :] = jnp.full((qg, qg), NEG_BIG, BF16)
        return carry

    lax.fori_loop(nit * step, s // qg, fill_tile, 0)


def dsa_index(zvi, w_hm, ik, b, s):
    qg = min(IDX_QG, s)
    ng = s // qg
    return pl.pallas_call(
        functools.partial(_idx_kernel, qg=qg, s=s),
        grid=(b, ng),
        in_specs=[
            pl.BlockSpec((qg, IDX_Q), lambda bi, gi: (bi * ng + gi, DSA_W // IDX_Q)),
            pl.BlockSpec((None, IDX_HEADS, qg), lambda bi, gi: (bi, 0, gi)),
            pl.BlockSpec((None, s, IDX_DIM), lambda bi, gi: (bi, 0, 0)),
        ],
        out_specs=pl.BlockSpec((None, s, qg), lambda bi, gi: (bi, 0, gi)),
        out_shape=jax.ShapeDtypeStruct((b, s, s), BF16),
        scratch_shapes=[pltpu.VMEM((s, qg), I16), pltpu.VMEM((s, qg), I16), pltpu.VMEM((qg, qg), F32),
                        pltpu.VMEM((IDX_HEADS, qg, IDX_DIM), BF16)],
        compiler_params=_cp(("parallel", "arbitrary")),
        name="dsa_index",
    )(zvi, w_hm, ik)


def _attn_kernel(qi_ref, ki_ref, q_ref, k_ref, vt_ref, mask_ref, nbd_ref, nbp_ref, o_ref, m_ref, acc_ref):
    p = pl.program_id(1)
    qi = qi_ref[p]
    ki = ki_ref[p]

    @pl.when(ki == 0)
    def _():
        m_ref[...] = jnp.full_like(m_ref, NEG_BIG)
        acc_ref[...] = jnp.zeros_like(acc_ref)

    def hs(h):
        return slice(h * DSA_HD, (h + 1) * DSA_HD)

    def vs(h):
        return slice(h * ATT_VROWS, (h + 1) * ATT_VROWS)

    nrow = 2
    rch = q_ref.shape[0] // nrow
    kh = k_ref.shape[0] // 2

    def heads(bias_of_head):
        def score_chunk(h, r):
            return _dot_nt(k_ref[r * rch:(r + 1) * rch, hs(h)], q_ref[:, hs(h)])

        def value_half(h, pr, half):
            return _dot(vt_ref[vs(h), half * kh:(half + 1) * kh], pr)

        sc = {0: [score_chunk(0, r) for r in range(nrow)], 1: [score_chunk(1, r) for r in range(nrow)]}
        prev = None
        for h in range(DSA_HEADS + 1):
            if h < DSA_HEADS:
                bias = bias_of_head(h)
                cur = sc.pop(h)
                sb, cm = [], []
                nxt = []
                for r in range(nrow):
                    sb.append(cur[r] + bias[r * rch:(r + 1) * rch, :])
                    cm.append(jnp.max(sb[r], axis=0, keepdims=True))
                    if h + 2 < DSA_HEADS:
                        nxt.append(score_chunk(h + 2, r))
                if nxt:
                    sc[h + 2] = nxt
                m_old = m_ref[h]
                m_new = m_old
                for c in cm:
                    m_new = jnp.maximum(m_new, c)
                alpha = jnp.exp2(m_old - m_new)
                m_ref[h] = m_new
            pv = []
            pr = []
            for r in range(nrow):
                if h < DSA_HEADS:
                    pr.append(jnp.exp2(sb[r] - m_new).astype(BF16))
                if prev is not None and r % (nrow // 2) == 0:
                    pv.append(value_half(prev[0], prev[2][r // (nrow // 2)], r // (nrow // 2)))
            if prev is not None:
                ph = prev[0]
                acc_ref[vs(ph), :] = prev[1] * acc_ref[vs(ph), :] + pv[0] + pv[1]
            if h < DSA_HEADS:
                prev = (h, alpha, [jnp.concatenate(pr[:nrow // 2], axis=0), jnp.concatenate(pr[nrow // 2:], axis=0)])

    @pl.when(ki < qi - 1)
    def _():
        mb = mask_ref[...].astype(F32)
        heads(lambda h: mb)

    @pl.when(ki == qi - 1)
    def _():
        mb = mask_ref[...].astype(F32)
        heads(lambda h: mb + nbp_ref[h].astype(F32))

    @pl.when(ki == qi)
    def _():
        mb = mask_ref[...].astype(F32)
        heads(lambda h: mb + nbd_ref[h].astype(F32))
        for h in range(DSA_HEADS):
            num = acc_ref[h * ATT_VROWS:h * ATT_VROWS + DSA_HD, :]
            den = acc_ref[h * ATT_VROWS + DSA_HD:h * ATT_VROWS + DSA_HD + 1, :]
            o_ref[hs(h), :] = (num / den).astype(o_ref.dtype)


def dsa_attention(zqk, vt, mask, nbd, nbp, b, s):
    t = min(ATT_T, s)
    ng = s // t
    pairs = [(qi, ki) for qi in range(ng) for ki in range(qi + 1)]
    qi_tab = jnp.asarray([p[0] for p in pairs], I32)
    ki_tab = jnp.asarray([p[1] for p in pairs], I32)
    grid_spec = pltpu.PrefetchScalarGridSpec(
        num_scalar_prefetch=2,
        grid=(b, len(pairs)),
        in_specs=[
            pl.BlockSpec((t, DSA_W), lambda bi, p, qt, kt: (bi * ng + qt[p], 0)),
            pl.BlockSpec((t, DSA_W), lambda bi, p, qt, kt: (bi * ng + kt[p], 1)),
            pl.BlockSpec((None, DSA_HEADS * ATT_VROWS, t), lambda bi, p, qt, kt: (bi, 0, kt[p])),
            pl.BlockSpec((None, t, t), lambda bi, p, qt, kt: (bi, kt[p], qt[p])),
            pl.BlockSpec((DSA_HEADS, t, t), lambda bi, p, qt, kt: (0, 0, 0)),
            pl.BlockSpec((DSA_HEADS, t, t), lambda bi, p, qt, kt: (0, 0, 0)),
        ],
        out_specs=pl.BlockSpec((None, DSA_W, t), lambda bi, p, qt, kt: (bi, 0, qt[p])),
        scratch_shapes=[
            pltpu.VMEM((DSA_HEADS, 1, t), F32),
            pltpu.VMEM((DSA_HEADS * ATT_VROWS, t), F32),
        ],
    )
    return pl.pallas_call(
        _attn_kernel,
        grid_spec=grid_spec,
        out_shape=jax.ShapeDtypeStruct((b, DSA_W, s), BF16),
        compiler_params=_cp(("parallel", "arbitrary")),
        name="dsa_attention",
    )(qi_tab, ki_tab, zqk, zqk, vt, mask, nbd, nbp)


def _t5_bucket(rel):
    half = REL_BUCKETS // 2
    exact = half // 2
    sign = jnp.where(rel > 0, half, 0)
    n = jnp.abs(rel)
    nf = jnp.maximum(n, 1).astype(F32)
    large = exact + (jnp.log(nf / exact) / math.log(REL_MAX_DIST / exact) * (half - exact)).astype(I32)
    large = jnp.minimum(large, half - 1)
    return sign + jnp.where(n < exact, n, large)


def _near_bias_tables(rel_bias, t):
    kpos = jnp.arange(t, dtype=I32)[:, None]
    qpos = jnp.arange(t, dtype=I32)[None, :]
    shifted = (rel_bias - rel_bias[REL_BUCKETS // 2 - 1]) * LOG2E

    def table(rel):
        onehot = (_t5_bucket(rel)[:, :, None] == jnp.arange(REL_BUCKETS, dtype=I32)).astype(F32)
        return jnp.einsum("kqb,bh->hkq", onehot, shifted, precision=lax.Precision.HIGHEST).astype(BF16)

    return table(kpos - qpos), table(kpos - t - qpos)


def _router_kernel(x_ref, g_ref, w_ref, b_ref, hn_ref, ri_ref, rw_ref):
    x = x_ref[...]
    ms = jnp.mean(x * x, axis=-1, keepdims=True)
    hn = x * lax.rsqrt(ms + EPS) * g_ref[...]
    hn_ref[...] = hn
    logits = _dot(hn.astype(BF16), w_ref[...]) + b_ref[...]
    lane = lax.broadcasted_iota(I32, logits.shape, 1)
    lanef = lane.astype(F32)
    big = 1e9
    is_grp = lane < N_GROUPS
    gl = jnp.where(is_grp, logits, -jnp.inf)
    gmax = jnp.max(gl, axis=-1, keepdims=True)
    grp = jnp.min(jnp.where(gl == gmax, lanef, big), axis=-1, keepdims=True)
    gsum = jnp.sum(jnp.where(is_grp, jnp.exp(logits - gmax), 0.0), axis=-1, keepdims=True)
    grp_w = 1.0 / gsum
    elo = N_GROUPS + grp * EXPERTS_PER_GROUP
    in_grp = jnp.logical_and(lanef >= elo, lanef < elo + EXPERTS_PER_GROUP)
    el = jnp.where(in_grp, logits, -jnp.inf)
    v1 = jnp.max(el, axis=-1, keepdims=True)
    i1 = jnp.min(jnp.where(el == v1, lanef, big), axis=-1, keepdims=True)
    el2 = jnp.where(lanef == i1, -jnp.inf, el)
    v2 = jnp.max(el2, axis=-1, keepdims=True)
    i2 = jnp.min(jnp.where(el2 == v2, lanef, big), axis=-1, keepdims=True)
    e2 = jnp.exp(v2 - v1)
    w1 = grp_w * (1.0 / (1.0 + e2))
    w2 = grp_w * (e2 / (1.0 + e2))
    ri_ref[...] = jnp.where(lane == 0, i1 - N_GROUPS, jnp.where(lane == 1, i2 - N_GROUPS, 0.0)).astype(I32)
    rw_ref[...] = jnp.where(lane == 0, w1, jnp.where(lane == 1, w2, 0.0))


def router(x1, g, w_r, b_r, tm=256):
    m, d = x1.shape
    tm = min(tm, m)
    return pl.pallas_call(
        _router_kernel,
        grid=(m // tm,),
        in_specs=[
            pl.BlockSpec((tm, d), lambda i: (i, 0)),
            pl.BlockSpec((1, d), lambda i: (0, 0)),
            pl.BlockSpec((d, LANES), lambda i: (0, 0)),
            pl.BlockSpec((1, LANES), lambda i: (0, 0)),
        ],
        out_specs=[
            pl.BlockSpec((tm, d), lambda i: (i, 0)),
            pl.BlockSpec((tm, LANES), lambda i: (i, 0)),
            pl.BlockSpec((tm, LANES), lambda i: (i, 0)),
        ],
        out_shape=[
            jax.ShapeDtypeStruct((m, d), F32),
            jax.ShapeDtypeStruct((m, LANES), I32),
            jax.ShapeDtypeStruct((m, LANES), F32),
        ],
        compiler_params=_cp(("parallel",)),
        name="router",
    )(x1, g.reshape(1, d).astype(F32), w_r, b_r)


def _rank_kernel(e_ref, rank_ref, cnt_ref, run_ref, *, rb):
    @pl.when(pl.program_id(0) == 0)
    def _():
        run_ref[...] = jnp.zeros_like(run_ref)

    e = e_ref[...]
    sub = lax.broadcasted_iota(I32, (N_EXPERTS, rb), 0)
    hit = sub == e
    oh = jnp.where(hit, 1.0, 0.0)
    a = lax.broadcasted_iota(I32, (rb, rb), 0)
    bcol = lax.broadcasted_iota(I32, (rb, rb), 1)
    upper = jnp.where(a < bcol, 1.0, 0.0).astype(BF16)
    prefix = _dot(oh.astype(BF16), upper)
    run = run_ref[...]
    rank = jnp.sum(jnp.where(hit, prefix + run[:, 0:1], 0.0), axis=0, keepdims=True)
    rank_ref[...] = rank.astype(I32)
    new = run + jnp.sum(oh, axis=1, keepdims=True)
    run_ref[...] = new
    cnt_ref[...] = new


def moe_rank(eid_flat):
    a = eid_flat.shape[0]
    rb = min(RANK_RB, a)
    nb = a // rb
    rank, cnt = pl.pallas_call(
        functools.partial(_rank_kernel, rb=rb),
        grid=(nb,),
        in_specs=[pl.BlockSpec((None, 1, rb), lambda i: (i, 0, 0))],
        out_specs=[
            pl.BlockSpec((None, 1, rb), lambda i: (i, 0, 0)),
            pl.BlockSpec((N_EXPERTS, LANES), lambda i: (0, 0)),
        ],
        out_shape=[
            jax.ShapeDtypeStruct((nb, 1, rb), I32),
            jax.ShapeDtypeStruct((N_EXPERTS, LANES), F32),
        ],
        scratch_shapes=[pltpu.VMEM((N_EXPERTS, LANES), F32)],
        compiler_params=_cp(("arbitrary",)),
        name="moe_rank",
    )(eid_flat.reshape(nb, 1, rb))
    return rank.reshape(a), cnt[:, 0].astype(I32)


def _row_copy(src, dst, sem):
    return pltpu.make_async_copy(src, dst, sem)


def _dispatch_kernel(dest_ref, hn_ref, xs_in_ref, xs_ref, sem, *, tm):
    del xs_in_ref
    base = pl.program_id(0) * (tm * TOP_K_INNER)

    def body(r, carry):
        for slot in range(TOP_K_INNER):
            d = dest_ref[base + r * TOP_K_INNER + slot]
            _row_copy(hn_ref.at[pl.ds(r, 1), :], xs_ref.at[pl.ds(d, 1), :], sem).start()
        return carry

    lax.fori_loop(0, tm, body, 0, unroll=8)
    for _ in range(TOP_K_INNER):
        _row_copy(hn_ref, xs_ref.at[pl.ds(0, tm), :], sem).wait()


def moe_dispatch(dest, hn, xs0):
    m, d = hn.shape
    tm = min(ROW_T, m)
    grid_spec = pltpu.PrefetchScalarGridSpec(
        num_scalar_prefetch=1,
        grid=(m // tm,),
        in_specs=[
            pl.BlockSpec((tm, d), lambda i, dst: (i, 0)),
            pl.BlockSpec(memory_space=pl.ANY),
        ],
        out_specs=pl.BlockSpec(memory_space=pl.ANY),
        scratch_shapes=[pltpu.SemaphoreType.DMA(())],
    )
    return pl.pallas_call(
        functools.partial(_dispatch_kernel, tm=tm),
        grid_spec=grid_spec,
        out_shape=jax.ShapeDtypeStruct(xs0.shape, xs0.dtype),
        input_output_aliases={2: 0},
        compiler_params=_cp(("arbitrary",), has_side_effects=True),
        name="moe_dispatch",
    )(dest, hn, xs0)


def _expert_kernel(be_ref, nv_ref, run_ref, nxt_ref, xs_ref, wg_hbm, wu_hbm, wd_hbm, y_ref,
                   wg_f, wu_f, wd_f, wg_bf, wu_bf, wd_bf, sem):
    i = pl.program_id(0)
    valid = i < nv_ref[0]
    e = be_ref[i]
    fresh = jnp.logical_and(valid, jnp.logical_or(i == 0, e != be_ref[jnp.maximum(i - 1, 0)]))
    slot = run_ref[i] % 2

    def weight_copies(expert, s_):
        return [pltpu.make_async_copy(wg_hbm.at[expert], wg_f.at[s_], sem.at[s_, 0]),
                pltpu.make_async_copy(wu_hbm.at[expert], wu_f.at[s_], sem.at[s_, 1]),
                pltpu.make_async_copy(wd_hbm.at[expert], wd_f.at[s_], sem.at[s_, 2])]

    @pl.when(i == 0)
    def _():
        for c in weight_copies(e, 0):
            c.start()

    @pl.when(fresh)
    def _():
        nxt = nxt_ref[i]

        @pl.when(nxt >= 0)
        def _():
            for c in weight_copies(nxt, 1 - slot):
                c.start()

        for c in weight_copies(e, slot):
            c.wait()
        wg_bf[...] = wg_f[slot].astype(BF16)
        wu_bf[...] = wu_f[slot].astype(BF16)
        wd_bf[...] = wd_f[slot].astype(BF16)

    @pl.when(valid)
    def _():
        xb = xs_ref[...].astype(BF16)
        hg = _dot(xb, wg_bf[...])
        hu = _dot(xb, wu_bf[...])
        act = (hg * _sigmoid(hg) * hu).astype(BF16)
        y_ref[...] = _dot(act, wd_bf[...])

    @pl.when(jnp.logical_not(valid))
    def _():
        y_ref[...] = jnp.zeros_like(y_ref)


def moe_experts(blk_expert, nvalid, has_rows, xs, w_gate, w_up, w_down):
    p, d = xs.shape
    nblk = p // MOE_BLK
    ff = w_gate.shape[2]
    first = jnp.concatenate([jnp.ones((1,), bool), blk_expert[1:] != blk_expert[:-1]])
    run = (jnp.cumsum(first.astype(I32)) - 1).astype(I32)
    eids = jnp.arange(N_EXPERTS, dtype=I32)
    later = jnp.logical_and(eids[None, :] > eids[:, None], has_rows[None, :])
    nxt_tab = jnp.min(jnp.where(later, eids[None, :], N_EXPERTS), axis=1)
    nxt_tab = jnp.where(nxt_tab >= N_EXPERTS, -1, nxt_tab)
    nxt = jnp.sum(jnp.where(blk_expert[:, None] == eids[None, :], nxt_tab[None, :], 0), axis=1).astype(I32)

    def blk(i, nv):
        return jnp.minimum(i, nv[0] - 1)

    grid_spec = pltpu.PrefetchScalarGridSpec(
        num_scalar_prefetch=4,
        grid=(nblk,),
        in_specs=[
            pl.BlockSpec((MOE_BLK, d), lambda i, be, nv, rn, nx: (blk(i, nv), 0)),
            pl.BlockSpec(memory_space=pl.ANY),
            pl.BlockSpec(memory_space=pl.ANY),
            pl.BlockSpec(memory_space=pl.ANY),
        ],
        out_specs=pl.BlockSpec((MOE_BLK, d), lambda i, be, nv, rn, nx: (i, 0)),
        scratch_shapes=[
            pltpu.VMEM((2, d, ff), F32), pltpu.VMEM((2, d, ff), F32), pltpu.VMEM((2, ff, d), F32),
            pltpu.VMEM((d, ff), BF16), pltpu.VMEM((d, ff), BF16), pltpu.VMEM((ff, d), BF16),
            pltpu.SemaphoreType.DMA((2, 3)),
        ],
    )
    return pl.pallas_call(
        _expert_kernel,
        grid_spec=grid_spec,
        out_shape=jax.ShapeDtypeStruct((p, d), F32),
        compiler_params=_cp(("arbitrary",)),
        name="moe_experts",
    )(blk_expert, nvalid, run, nxt, xs, w_gate, w_up, w_down)


def _combine_kernel(dest_ref, x_ref, rw_ref, g_ref, y_ref, o_ref, hn_ref, ybuf, sem, *, tm):
    i = pl.program_id(0)
    nsteps = pl.num_programs(0)

    def gather(step, buf):
        base = step * (tm * TOP_K_INNER)

        def body(r, carry):
            for slot in range(TOP_K_INNER):
                d = dest_ref[base + r * TOP_K_INNER + slot]
                _row_copy(y_ref.at[pl.ds(d, 1), :], ybuf.at[buf, slot, pl.ds(r, 1), :], sem.at[buf]).start()
            return carry

        lax.fori_loop(0, tm, body, 0, unroll=8)

    @pl.when(i == 0)
    def _():
        gather(0, 0)

    @pl.when(i + 1 < nsteps)
    def _():
        gather(i + 1, (i + 1) % 2)

    cur = i % 2
    for slot in range(TOP_K_INNER):
        _row_copy(y_ref.at[pl.ds(0, tm), :], ybuf.at[cur, slot], sem.at[cur]).wait()
    rw = rw_ref[...]
    o = x_ref[...] + rw[:, 0:1] * ybuf[cur, 0] + rw[:, 1:2] * ybuf[cur, 1]
    o_ref[...] = o
    ms = jnp.mean(o * o, axis=-1, keepdims=True)
    hn_ref[...] = (o * lax.rsqrt(ms + EPS) * g_ref[...]).astype(hn_ref.dtype)


def moe_combine(dest, x1, rw, g, y):
    m, d = x1.shape
    tm = min(ROW_T, m)
    grid_spec = pltpu.PrefetchScalarGridSpec(
        num_scalar_prefetch=1,
        grid=(m // tm,),
        in_specs=[
            pl.BlockSpec((tm, d), lambda i, dst: (i, 0)),
            pl.BlockSpec((tm, LANES), lambda i, dst: (i, 0)),
            pl.BlockSpec((1, d), lambda i, dst: (0, 0)),
            pl.BlockSpec(memory_space=pl.ANY),
        ],
        out_specs=[pl.BlockSpec((tm, d), lambda i, dst: (i, 0)), pl.BlockSpec((tm, d), lambda i, dst: (i, 0))],
        scratch_shapes=[pltpu.VMEM((2, TOP_K_INNER, tm, d), F32), pltpu.SemaphoreType.DMA((2,))],
    )
    return pl.pallas_call(
        functools.partial(_combine_kernel, tm=tm),
        grid_spec=grid_spec,
        out_shape=[jax.ShapeDtypeStruct((m, d), F32), jax.ShapeDtypeStruct((m, d), BF16)],
        compiler_params=_cp(("arbitrary",)),
        name="moe_combine",
    )(dest, x1, rw, g.reshape(1, d).astype(F32), y)


def _layer(x, p, norm_mix_g, w_in, gla_w_alpha, gla_b_alpha, gla_norm_g, w_out_gla, q_norm_g, k_norm_g, rel_bias,
           w_out_dsa, w_branch_gate, b_branch_gate, w_out, norm_ffn_g, w_group_router, b_group_router,
           w_expert_router, b_expert_router, w_exp_gate, w_exp_up, w_exp_down, norm_ple_g, w_ple_gate,
           b_ple_gate, w_ple_proj):
    b, s, d = x.shape
    t = b * s
    x2d = x.reshape(t, d)

    o_gq, o_gk, o_gv, o_gr = 0, GLA_QK, 2 * GLA_QK, 2 * GLA_QK + GLA_V
    o_glr = o_gr + GLA_V
    o_dq = o_glr + GLA_GATE_RANK
    o_dk, o_dv = o_dq + DSA_W, o_dq + 2 * DSA_W
    o_iq = o_dv + DSA_W
    o_ik = o_iq + IDX_Q
    o_iw = o_ik + IDX_DIM
    w_g = w_in[:, o_gq:o_glr].astype(BF16)
    w_qk = w_in[:, o_dq:o_dv].astype(BF16)
    w_vi = w_in[:, o_dv:o_ik].astype(BF16)
    pad = jnp.zeros((d, LANES - GLA_GATE_RANK - IDX_HEADS - IDX_DIM), F32)
    w_s = jnp.concatenate([w_in[:, o_glr:o_dq], w_in[:, o_iw:o_iw + IDX_HEADS], pad,
                           w_in[:, o_ik:o_iw]], axis=1).astype(BF16)

    h = rmsnorm(x2d, norm_mix_g, BF16)
    zg = matmul(h, w_g, _epi_plain, [], BF16, "proj_gla")
    qk_gain = jnp.concatenate([jnp.tile(q_norm_g * (DSA_HD ** -0.5 * LOG2E), DSA_HEADS),
                               jnp.tile(k_norm_g, DSA_HEADS)]).reshape(1, 2 * DSA_W).astype(F32)
    zqk = matmul(h, w_qk, _epi_headnorm, [(qk_gain, "row")], BF16, "proj_dsa_qk")
    zvi = matmul(h, w_vi, _epi_plain, [], BF16, "proj_dsa_vi")
    zs = matmul(h, w_s, _epi_plain, [], F32, "proj_small")
    gates = matmul(h, w_branch_gate.astype(BF16), _epi_sigmoid_bias,
                   [(b_branch_gate.reshape(1, -1).astype(F32), "row")], BF16, "branch_gates")

    wa = gla_w_alpha.reshape(GLA_GATE_RANK, GLA_HEADS, GLA_DK).transpose(1, 0, 2).astype(BF16)
    ba = gla_b_alpha.reshape(GLA_HEADS, 1, GLA_DK).astype(F32)
    o_a = gla(zg, zs, wa, ba, gla_norm_g.reshape(1, GLA_DV).astype(F32), b, s)

    ik = zs[:, LANES - IDX_DIM:].astype(BF16).reshape(b, s, IDX_DIM)
    w_hm = zs[:, GLA_GATE_RANK:GLA_GATE_RANK + IDX_HEADS].reshape(b, s, IDX_HEADS).transpose(0, 2, 1)
    mask = dsa_index(zvi, w_hm, ik, b, s)
    nbd, nbp = _near_bias_tables(rel_bias, min(ATT_T, s))
    vt = zvi[:, :DSA_W].reshape(b, s, DSA_HEADS, DSA_HD).transpose(0, 2, 3, 1)
    vt = jnp.concatenate([vt, jnp.ones((b, DSA_HEADS, ATT_VROWS - DSA_HD, s), BF16)], axis=2)
    vt = vt.reshape(b, DSA_HEADS * ATT_VROWS, s)
    o_b = dsa_attention(zqk, vt, mask, nbd, nbp, b, s).transpose(0, 2, 1).reshape(t, DSA_W)

    m = mix(o_a, o_b, w_out_gla.astype(BF16), w_out_dsa.astype(BF16), gates)
    x1 = matmul(m, w_out.astype(BF16), _epi_residual, [(x2d, "tile")], F32, "out_proj")

    w_r = jnp.concatenate([w_group_router, w_expert_router,
                           jnp.zeros((d, LANES - N_GROUPS - N_EXPERTS), F32)], axis=1).astype(BF16)
    b_r = jnp.concatenate([b_group_router, b_expert_router,
                           jnp.zeros((LANES - N_GROUPS - N_EXPERTS,), F32)]).reshape(1, LANES)
    hn, ri, rw = router(x1, norm_ffn_g, w_r, b_r)
    a = t * TOP_K_INNER
    eid = ri[:, :TOP_K_INNER].reshape(a)
    rank, counts = moe_rank(eid)
    padded = ((counts + MOE_BLK - 1) // MOE_BLK) * MOE_BLK
    pend = jnp.cumsum(padded)
    pstart = pend - padded
    dest = (pstart[eid] + rank).astype(I32)
    prow = a + N_EXPERTS * MOE_BLK
    nblk = prow // MOE_BLK
    blk_row0 = jnp.arange(nblk, dtype=I32)[:, None] * MOE_BLK
    blk_expert = jnp.minimum(jnp.sum((pend[None, :] <= blk_row0).astype(I32), axis=1), N_EXPERTS - 1)
    nvalid = (pend[-1:] // MOE_BLK).astype(I32)
    xs = moe_dispatch(dest, hn, jnp.zeros((prow, d), F32))
    y = moe_experts(blk_expert, nvalid, padded > 0, xs, w_exp_gate, w_exp_up, w_exp_down)
    x2, hp = moe_combine(dest, x1, rw, norm_ple_g, y)

    out = matmul(hp, w_ple_gate.astype(BF16), _epi_ple,
                 [(b_ple_gate.reshape(1, -1).astype(F32), "row"), (x2, "tile"),
                  (p.reshape(t, -1).astype(BF16), "mk"), (w_ple_proj.astype(BF16), "kn")],
                 F32, "ple")
    return out.reshape(b, s, d)


def kernel(x, p, norm_mix_g, w_in, gla_w_alpha, gla_b_alpha, gla_norm_g, w_out_gla, q_norm_g, k_norm_g, rel_bias, w_out_dsa, w_branch_gate, b_branch_gate, w_out, norm_ffn_g, w_group_router, b_group_router, w_expert_router, b_expert_router, w_exp_gate, w_exp_up, w_exp_down, norm_ple_g, w_ple_gate, b_ple_gate, w_ple_proj):
    depth = w_in.shape[0]
    for i in range(depth):
        x = _layer(x, p[i], norm_mix_g[i], w_in[i], gla_w_alpha[i], gla_b_alpha[i], gla_norm_g[i], w_out_gla[i],
                   q_norm_g[i], k_norm_g[i], rel_bias, w_out_dsa[i], w_branch_gate[i], b_branch_gate[i], w_out[i],
                   norm_ffn_g[i], w_group_router[i], b_group_router[i], w_expert_router[i], b_expert_router[i],
                   w_exp_gate[i], w_exp_up[i], w_exp_down[i], norm_ple_g[i], w_ple_gate[i], b_ple_gate[i],
                   w_ple_proj[i])
    return x
```

```python
import functools
import math

import jax
import jax.numpy as jnp
from jax import lax
from jax.experimental import pallas as pl
from jax.experimental.pallas import tpu as pltpu

F32 = jnp.float32
BF16 = jnp.bfloat16
I32 = jnp.int32
I16 = jnp.int16

EPS = 1e-6
CHUNK = 64
GLA_HEADS = 8
GLA_DK = 64
GLA_DV = 128
GLA_GATE_RANK = 16
GLA_GATE_TEMP = 16.0
GLA_QK = GLA_HEADS * GLA_DK
GLA_V = GLA_HEADS * GLA_DV
DSA_HEADS = 8
DSA_HD = 128
DSA_W = DSA_HEADS * DSA_HD
IDX_HEADS = 16
IDX_DIM = 64
IDX_Q = IDX_HEADS * IDX_DIM
INDEX_TOPK = 256
REL_BUCKETS = 32
REL_MAX_DIST = 128
N_GROUPS = 8
EXPERTS_PER_GROUP = 8
N_EXPERTS = N_GROUPS * EXPERTS_PER_GROUP
EXPERT_FF = 512
TOP_K_INNER = 2

LANES = 128
VMEM_LIMIT = 56 * 1024 * 1024
NEG_BIG = -1e30
LOG2E = math.log2(math.e)
KEY_NEG_INF = -2139095041
I16_MIN, I16_MAX = -32768, 32767
PACK16 = 16

GLA_CT = 1024
GLA_HEADS_PER_STEP = 2
IDX_QG = 256
ATT_T = 512
ATT_VROWS = DSA_HD + PACK16
MOE_BLK = 256
RANK_RB = 1024
ROW_T = 512
MM_SPLIT = 4


def _cp(sem, **kw):
    return pltpu.CompilerParams(dimension_semantics=sem, vmem_limit_bytes=VMEM_LIMIT, **kw)


def _dot(a, b):
    return jnp.dot(a, b, preferred_element_type=F32)


def _dot_nt(a, b):
    return lax.dot_general(a, b, (((1,), (1,)), ((), ())), preferred_element_type=F32)


def _dot_tn(a, b):
    return lax.dot_general(a, b, (((0,), (0,)), ((), ())), preferred_element_type=F32)


def _sigmoid(x):
    return 1.0 / (1.0 + jnp.exp(-x))


def _rmsnorm_kernel(x_ref, g_ref, o_ref):
    x = x_ref[...]
    ms = jnp.mean(x * x, axis=-1, keepdims=True)
    o_ref[...] = (x * lax.rsqrt(ms + EPS) * g_ref[...]).astype(o_ref.dtype)


def rmsnorm(x, g, out_dtype, tm=512):
    m, d = x.shape
    tm = min(tm, m)
    return pl.pallas_call(
        _rmsnorm_kernel,
        grid=(m // tm,),
        in_specs=[pl.BlockSpec((tm, d), lambda i: (i, 0)), pl.BlockSpec((1, d), lambda i: (0, 0))],
        out_specs=pl.BlockSpec((tm, d), lambda i: (i, 0)),
        out_shape=jax.ShapeDtypeStruct((m, d), out_dtype),
        compiler_params=_cp(("parallel",)),
        name="rmsnorm",
    )(x, g.reshape(1, d).astype(F32))


def _row_chunks(tm):
    rc = tm // MM_SPLIT if tm % (MM_SPLIT * PACK16) == 0 else tm
    return [slice(r, r + rc) for r in range(0, tm, rc)]


def _mm_kernel(*refs, n_extra, epilogue):
    a_ref, w_ref = refs[0], refs[1]
    extras = refs[2:2 + n_extra]
    o_ref = refs[2 + n_extra]
    chunks = _row_chunks(a_ref.shape[0])
    acc = _dot(a_ref[chunks[0], :], w_ref[...])
    for c, rows in enumerate(chunks):
        nxt = _dot(a_ref[chunks[c + 1], :], w_ref[...]) if c + 1 < len(chunks) else None
        epilogue(acc, rows, o_ref, *extras)
        acc = nxt


def matmul(a, w, epilogue, extras, out_dtype, name, tm=1024, tn=512):
    m, k = a.shape
    n = w.shape[1]
    tm, tn = min(tm, m), min(tn, n)
    specs = [pl.BlockSpec((tm, k), lambda i, j: (i, 0)), pl.BlockSpec((k, tn), lambda i, j: (0, j))]
    args = [a, w]
    for arr, kind in extras:
        if kind == "tile":
            specs.append(pl.BlockSpec((tm, tn), lambda i, j: (i, j)))
        elif kind == "row":
            specs.append(pl.BlockSpec((1, tn), lambda i, j: (0, j)))
        elif kind == "mk":
            specs.append(pl.BlockSpec((tm, arr.shape[1]), lambda i, j: (i, 0)))
        elif kind == "kn":
            specs.append(pl.BlockSpec((arr.shape[0], tn), lambda i, j: (0, j)))
        else:
            raise ValueError(kind)
        args.append(arr)
    return pl.pallas_call(
        functools.partial(_mm_kernel, n_extra=len(extras), epilogue=epilogue),
        grid=(m // tm, n // tn),
        in_specs=specs,
        out_specs=pl.BlockSpec((tm, tn), lambda i, j: (i, j)),
        out_shape=jax.ShapeDtypeStruct((m, n), out_dtype),
        compiler_params=_cp(("parallel", "arbitrary")),
        name=name,
    )(*args)


def _epi_plain(acc, rows, o_ref):
    o_ref[rows, :] = acc.astype(o_ref.dtype)


def _epi_headnorm(acc, rows, o_ref, g_ref):
    for c in range(acc.shape[1] // DSA_HD):
        sl = slice(c * DSA_HD, (c + 1) * DSA_HD)
        blk = acc[:, sl]
        ms = jnp.mean(blk * blk, axis=-1, keepdims=True)
        o_ref[rows, sl] = (blk * lax.rsqrt(ms + EPS) * g_ref[:, sl]).astype(o_ref.dtype)


def _epi_sigmoid_bias(acc, rows, o_ref, b_ref):
    o_ref[rows, :] = _sigmoid(acc + b_ref[...]).astype(o_ref.dtype)


def _epi_residual(acc, rows, o_ref, x_ref):
    o_ref[rows, :] = (x_ref[rows, :] + acc).astype(o_ref.dtype)


def _epi_ple(acc, rows, o_ref, b_ref, x_ref, p_ref, wp_ref):
    ple = _dot(p_ref[rows, :], wp_ref[...])
    o_ref[rows, :] = (x_ref[rows, :] + _sigmoid(acc + b_ref[...]) * ple).astype(o_ref.dtype)


def _mix_kernel(oa_ref, ob_ref, wa_ref, wb_ref, ga_ref, gb_ref, o_ref):
    chunks = _row_chunks(oa_ref.shape[0])

    def both(rows):
        return _dot(oa_ref[rows, :], wa_ref[...]), _dot(ob_ref[rows, :], wb_ref[...])

    ya, yb = both(chunks[0])
    for c, rows in enumerate(chunks):
        nxt = both(chunks[c + 1]) if c + 1 < len(chunks) else None
        o_ref[rows, :] = (ga_ref[rows, :].astype(F32) * ya + gb_ref[rows, :].astype(F32) * yb).astype(o_ref.dtype)
        if nxt is not None:
            ya, yb = nxt


def mix(oa, ob, wa, wb, gates, tm=1024, tn=512):
    m, k = oa.shape
    n = wa.shape[1]
    tm, tn = min(tm, m), min(tn, n)
    nj = n // tn
    return pl.pallas_call(
        _mix_kernel,
        grid=(m // tm, nj),
        in_specs=[
            pl.BlockSpec((tm, k), lambda i, j: (i, 0)),
            pl.BlockSpec((tm, k), lambda i, j: (i, 0)),
            pl.BlockSpec((k, tn), lambda i, j: (0, j)),
            pl.BlockSpec((k, tn), lambda i, j: (0, j)),
            pl.BlockSpec((tm, tn), lambda i, j: (i, j)),
            pl.BlockSpec((tm, tn), lambda i, j: (i, j + nj)),
        ],
        out_specs=pl.BlockSpec((tm, tn), lambda i, j: (i, j)),
        out_shape=jax.ShapeDtypeStruct((m, n), BF16),
        compiler_params=_cp(("parallel", "arbitrary")),
        name="mix",
    )(oa, ob, wa, wb, gates, gates)


def _gla_kernel(q_ref, k_ref, v_ref, gr_ref, zs_ref, wa_ref, ba_ref, gn_ref, o_ref, st_ref, *, nchunk):
    @pl.when(pl.program_id(2) == 0)
    def _():
        st_ref[...] = jnp.zeros_like(st_ref)

    glr = zs_ref[:, 0:GLA_GATE_RANK].astype(BF16)
    row = lax.broadcasted_iota(I32, (CHUNK, CHUNK), 0)
    col = lax.broadcasted_iota(I32, (CHUNK, CHUNK), 1)
    causal = row >= col
    tril = jnp.where(causal, 1.0, 0.0).astype(BF16)
    gn = gn_ref[...]
    mid = CHUNK // 2 - 1
    nh = q_ref.shape[1] // GLA_DK
    sls = [slice(c * CHUNK, (c + 1) * CHUNK) for c in range(nchunk)]
    items = [(hh, c) for hh in range(nh) for c in range(nchunk)]

    def kcols(hh):
        return slice(hh * GLA_DK, (hh + 1) * GLA_DK)

    def vcols(hh):
        return slice(hh * GLA_DV, (hh + 1) * GLA_DV)

    g_hi, g_lo = [], []
    for hh in range(nh):
        logit = _dot(glr, wa_ref[hh]) + ba_ref[hh]
        g = (jnp.minimum(logit, 0.0) - jnp.log(1.0 + jnp.exp(-jnp.abs(logit)))) * (1.0 / GLA_GATE_TEMP)
        hi = g.astype(BF16)
        g_hi.append(hi)
        g_lo.append((g - hi.astype(F32)).astype(BF16))
    cums = {(hh, c): _dot(tril, g_hi[hh][sls[c]]) + _dot(tril, g_lo[hh][sls[c]]) for hh, c in items}
    qts, kts, qgs, kds, decs = {}, {}, {}, {}, {}
    for hh, c in items:
        cum = cums[hh, c]
        c0 = cum[mid:mid + 1, :]
        gl = cum[CHUNK - 1:CHUNK, :]
        qt = q_ref[sls[c], kcols(hh)].astype(F32) * (GLA_DK ** -0.5) * jnp.exp(cum - c0)
        kt = k_ref[sls[c], kcols(hh)].astype(F32) * jnp.exp(c0 - cum)
        qts[hh, c] = qt.astype(BF16)
        kts[hh, c] = kt.astype(BF16)
        qgs[hh, c] = (qt * jnp.exp(c0)).astype(BF16)
        kds[hh, c] = (kt * jnp.exp(gl - c0)).astype(BF16)
        decs[hh, c] = jnp.exp(gl)
    scs = {it: _dot_nt(qts[it], kts[it]) for it in items}
    kvs = {(hh, c): _dot_tn(v_ref[sls[c], vcols(hh)], kds[hh, c]) for hh, c in items}
    intra = {(hh, c): _dot(jnp.where(causal, scs[hh, c], 0.0).astype(BF16), v_ref[sls[c], vcols(hh)])
             for hh, c in items}
    sts = {}
    for hh in range(nh):
        st = st_ref[hh]
        for c in range(nchunk):
            sts[hh, c] = st.astype(BF16)
            st = st * decs[hh, c] + kvs[hh, c]
        st_ref[hh] = st
    inter = {it: _dot_nt(qgs[it], sts[it]) for it in items}
    for hh, c in items:
        o = intra[hh, c] + inter[hh, c]
        ms = jnp.mean(o * o, axis=-1, keepdims=True)
        gr = gr_ref[sls[c], vcols(hh)].astype(F32)
        o_ref[sls[c], vcols(hh)] = (o * lax.rsqrt(ms + EPS) * gn * (gr * _sigmoid(gr))).astype(o_ref.dtype)


def gla(zg, zs, wa, ba, gn, b, s):
    ct = min(GLA_CT, s)
    ns = s // ct
    hps = GLA_HEADS_PER_STEP
    kw, vw = hps * GLA_DK, hps * GLA_DV
    k_blk0 = GLA_QK // kw
    v_blk0 = (2 * GLA_QK) // vw
    gr_blk0 = (2 * GLA_QK + GLA_V) // vw
    return pl.pallas_call(
        functools.partial(_gla_kernel, nchunk=ct // CHUNK),
        grid=(b, GLA_HEADS // hps, ns),
        in_specs=[
            pl.BlockSpec((ct, kw), lambda bi, hp, si: (bi * ns + si, hp)),
            pl.BlockSpec((ct, kw), lambda bi, hp, si: (bi * ns + si, k_blk0 + hp)),
            pl.BlockSpec((ct, vw), lambda bi, hp, si: (bi * ns + si, v_blk0 + hp)),
            pl.BlockSpec((ct, vw), lambda bi, hp, si: (bi * ns + si, gr_blk0 + hp)),
            pl.BlockSpec((ct, LANES), lambda bi, hp, si: (bi * ns + si, 0)),
            pl.BlockSpec((hps, GLA_GATE_RANK, GLA_DK), lambda bi, hp, si: (hp, 0, 0)),
            pl.BlockSpec((hps, 1, GLA_DK), lambda bi, hp, si: (hp, 0, 0)),
            pl.BlockSpec((1, GLA_DV), lambda bi, hp, si: (0, 0)),
        ],
        out_specs=pl.BlockSpec((ct, vw), lambda bi, hp, si: (bi * ns + si, hp)),
        out_shape=jax.ShapeDtypeStruct((b * s, GLA_V), BF16),
        scratch_shapes=[pltpu.VMEM((hps, GLA_DV, GLA_DK), F32)],
        compiler_params=_cp(("parallel", "parallel", "arbitrary")),
        name="gla",
    )(zg, zg, zg, zg, zs, wa, ba, gn)


def _order_key(x):
    bits = lax.bitcast_convert_type(x, I32)
    return bits ^ ((bits >> 31) & 0x7FFFFFFF)


def _idx_kernel(iq_ref, w_ref, ik_ref, mask_ref, hi_ref, lo_ref, seg_ref, iqh_ref, *, qg, s):
    g = pl.program_id(1)
    nt = g + 1
    w = w_ref[...] * ((IDX_HEADS ** -0.5) * (IDX_DIM ** -0.5))
    kchunk = lax.broadcasted_iota(I32, (qg, qg), 0) // CHUNK
    qchunk = lax.broadcasted_iota(I32, (qg, qg), 1) // CHUNK
    adm_diag = kchunk <= qchunk

    half = qg // 2
    for j in range(IDX_HEADS):
        iqh_ref[j] = iq_ref[:, j * IDX_DIM:(j + 1) * IDX_DIM]

    step = 2 if (s // qg) % 2 == 0 else 1
    nit = (nt + step - 1) // step
    span = step * qg

    def score_tile(p, carry):
        for u in range(step):
            t = p * step + u
            for hf in range(2):
                off = pl.multiple_of(t * qg + hf * half, half)
                ikt = ik_ref[pl.ds(off, half), :]
                acc = jnp.zeros((half, qg), F32)
                for j in range(IDX_HEADS):
                    acc = acc + jnp.maximum(_dot_nt(ikt, iqh_ref[j]), 0.0) * w[j:j + 1, :]
                adm = jnp.logical_or(t < g, jnp.logical_and(t == g, adm_diag[hf * half:(hf + 1) * half, :]))
                acc = jnp.where(adm, acc, -jnp.inf)
                key = _order_key(acc)
                hi_ref[pl.ds(off, half), :] = (key >> 16).astype(I16)
                lo_ref[pl.ds(off, half), :] = ((key & 0xFFFF) + I16_MIN).astype(I16)
                rows = slice(hf * half, (hf + 1) * half)
                seg_ref[rows, :] = jnp.maximum(seg_ref[rows, :], acc)
        return carry

    seg_ref[...] = jnp.full((qg, qg), -jnp.inf, F32)
    lax.fori_loop(0, nit, score_tile, 0)
    segmax = seg_ref[...]

    topk = min(INDEX_TOPK, s // 4)
    kf = float(topk)
    lb = _order_key(jnp.min(segmax, axis=0, keepdims=True))
    mx = _order_key(jnp.max(segmax, axis=0, keepdims=True))
    qcol = lax.broadcasted_iota(I32, (1, qg), 1)
    n_adm = g * qg + (qcol // CHUNK + 1) * CHUNK
    small = n_adm <= topk

    def count_ge(ref, cand):
        c16 = cand.astype(I16)

        def body(p, acc):
            off = pl.multiple_of(p * span, span)
            m = jnp.where(ref[pl.ds(off, span), :] >= c16, jnp.int16(1), jnp.int16(0))
            for r in range(span // PACK16):
                acc = acc + m[r * PACK16:(r + 1) * PACK16, :]
            return acc

        acc = lax.fori_loop(0, nit, body, jnp.zeros((PACK16, qg), I16))
        return jnp.sum(acc.astype(F32), axis=0, keepdims=True)

    def bisect(ref, lo0, hi0, target, npass):
        def body(_, c):
            lo, hi, c_hi = c
            mid = jnp.right_shift(lo + hi, 1)
            cnt = count_ge(ref, mid)
            up = cnt >= target
            return jnp.where(up, mid, lo), jnp.where(up, hi, mid), jnp.where(up, c_hi, cnt)

        lo, _, c_hi = lax.fori_loop(0, npass, body, (lo0, hi0, jnp.zeros((1, qg), F32)))
        return lo, c_hi

    a_lo0 = jnp.right_shift(lb, 16)
    a_hi0 = jnp.right_shift(mx, 16) + 1
    width = jnp.where(small, 1, a_hi0 - a_lo0)
    nbits = jnp.zeros((1, qg), F32)
    for b in range(17):
        nbits = nbits + jnp.where(width > (1 << b), 1.0, 0.0)
    a_lo, ca_hi = bisect(hi_ref, a_lo0, a_hi0, kf, jnp.max(nbits).astype(I32))
    a16 = jnp.where(small, I16_MAX, a_lo).astype(I16)

    def keep_equal_hi(p, carry):
        off = pl.multiple_of(p * span, span)
        lo_ref[pl.ds(off, span), :] = jnp.where(hi_ref[pl.ds(off, span), :] == a16, lo_ref[pl.ds(off, span), :],
                                                jnp.int16(I16_MIN))
        return carry

    lax.fori_loop(0, nit, keep_equal_hi, 0)

    b_lo, _ = bisect(lo_ref, jnp.full((1, qg), I16_MIN, I32), jnp.full((1, qg), I16_MAX + 1, I32),
                     kf - ca_hi, 16)
    whole = b_lo == I16_MIN
    thr_h = jnp.where(small, jnp.int32(KEY_NEG_INF >> 16), jnp.where(whole, a_lo - 1, a_lo)).astype(I16)
    thr_l = jnp.where(jnp.logical_or(small, whole), I16_MAX, b_lo).astype(I16)

    def write_tile(p, acc):
        off = pl.multiple_of(p * span, span)
        one, nil = jnp.int16(1), jnp.int16(0)
        sel = jnp.where(hi_ref[pl.ds(off, span), :] > thr_h, one,
                        jnp.where(lo_ref[pl.ds(off, span), :] >= thr_l, one, nil))
        mask_ref[pl.ds(off, span), :] = jnp.where(sel > nil, jnp.zeros((span, qg), BF16),
                                                  jnp.full((span, qg), NEG_BIG, BF16))
        for r in range(span // PACK16):
            acc = acc + sel[r * PACK16:(r + 1) * PACK16, :]
        return acc

    n_sel = lax.fori_loop(0, nit, write_tile, jnp.zeros((PACK16, qg), I16))
    n_sel = jnp.sum(n_sel.astype(F32), axis=0, keepdims=True)
    excess = jnp.where(small, 0.0, n_sel - kf)

    @pl.when(jnp.max(excess) > 0.5)
    def _():
        h16 = a_lo.astype(I16)
        l16 = b_lo.astype(I16)

        def tied(off):
            same_lo = jnp.where(lo_ref[pl.ds(off, qg), :] == l16, jnp.ones((qg, qg), BF16), jnp.zeros((qg, qg), BF16))
            return jnp.where(hi_ref[pl.ds(off, qg), :] == h16, same_lo, jnp.zeros((qg, qg), BF16))

        def count_tied(t, acc):
            eq = tied(pl.multiple_of(t * qg, qg)).astype(F32)
            return acc + jnp.sum(eq.reshape(qg // 8, 8, qg), axis=0)

        n_eq = jnp.sum(lax.fori_loop(0, nit * step, count_tied, jnp.zeros((8, qg), F32)), axis=0, keepdims=True)
        need_eq = jnp.where(small, 1e9, kf - (n_sel - n_eq))
        krow = lax.broadcasted_iota(I32, (qg, qg), 0)
        kcol = lax.broadcasted_iota(I32, (qg, qg), 1)
        tril_incl = jnp.where(krow >= kcol, 1.0, 0.0).astype(BF16)

        def drop_late_ties(t, run):
            off = pl.multiple_of(t * qg, qg)
            eq = tied(off)
            rank = _dot(tril_incl, eq) + run
            late = jnp.where(rank > need_eq, 1.0, 0.0).astype(BF16) * eq
            old = mask_ref[pl.ds(off, qg), :]
            mask_ref[pl.ds(off, qg), :] = jnp.where(late > 0, jnp.full((qg, qg), NEG_BIG, BF16), old)
            return rank[qg - 1:qg, :]

        lax.fori_loop(0, nit * step, drop_late_ties, jnp.zeros((1, qg), F32))

    def fill_tile(t, carry):
        off = pl.multiple_of(t * qg, qg)
        mask_ref[pl.ds(off, qg), :] = jnp.full((qg, qg), NEG_BIG, BF16)
        return carry

    lax.fori_loop(nit * step, s // qg, fill_tile, 0)


def dsa_index(zvi, w_hm, ik, b, s):
    qg = min(IDX_QG, s)
    ng = s // qg
    return pl.pallas_call(
        functools.partial(_idx_kernel, qg=qg, s=s),
        grid=(b, ng),
        in_specs=[
            pl.BlockSpec((qg, IDX_Q), lambda bi, gi: (bi * ng + gi, DSA_W // IDX_Q)),
            pl.BlockSpec((None, IDX_HEADS, qg), lambda bi, gi: (bi, 0, gi)),
            pl.BlockSpec((None, s, IDX_DIM), lambda bi, gi: (bi, 0, 0)),
        ],
        out_specs=pl.BlockSpec((None, s, qg), lambda bi, gi: (bi, 0, gi)),
        out_shape=jax.ShapeDtypeStruct((b, s, s), BF16),
        scratch_shapes=[pltpu.VMEM((s, qg), I16), pltpu.VMEM((s, qg), I16), pltpu.VMEM((qg, qg), F32),
                        pltpu.VMEM((IDX_HEADS, qg, IDX_DIM), BF16)],
        compiler_params=_cp(("parallel", "arbitrary")),
        name="dsa_index",
    )(zvi, w_hm, ik)


def _attn_kernel(qi_ref, ki_ref, q_ref, k_ref, vt_ref, mask_ref, nbd_ref, nbp_ref, o_ref, m_ref, acc_ref):
    p = pl.program_id(1)
    qi = qi_ref[p]
    ki = ki_ref[p]

    @pl.when(ki == 0)
    def _():
        m_ref[...] = jnp.full_like(m_ref, NEG_BIG)
        acc_ref[...] = jnp.zeros_like(acc_ref)

    def hs(h):
        return slice(h * DSA_HD, (h + 1) * DSA_HD)

    def vs(h):
        return slice(h * ATT_VROWS, (h + 1) * ATT_VROWS)

    nrow = 2
    rch = q_ref.shape[0] // nrow
    kh = k_ref.shape[0] // 2

    def heads(bias_of_head):
        def score_chunk(h, r):
            return _dot_nt(k_ref[r * rch:(r + 1) * rch, hs(h)], q_ref[:, hs(h)])

        def value_half(h, pr, half):
            return _dot(vt_ref[vs(h), half * kh:(half + 1) * kh], pr)

        sc = {0: [score_chunk(0, r) for r in range(nrow)], 1: [score_chunk(1, r) for r in range(nrow)]}
        prev = None
        for h in range(DSA_HEADS + 1):
            if h < DSA_HEADS:
                bias = bias_of_head(h)
                cur = sc.pop(h)
                sb, cm = [], []
                nxt = []
                for r in range(nrow):
                    sb.append(cur[r] + bias[r * rch:(r + 1) * rch, :])
                    cm.append(jnp.max(sb[r], axis=0, keepdims=True))
                    if h + 2 < DSA_HEADS:
                        nxt.append(score_chunk(h + 2, r))
                if nxt:
                    sc[h + 2] = nxt
                m_old = m_ref[h]
                m_new = m_old
                for c in cm:
                    m_new = jnp.maximum(m_new, c)
                alpha = jnp.exp2(m_old - m_new)
                m_ref[h] = m_new
            pv = []
            pr = []
            for r in range(nrow):
                if h < DSA_HEADS:
                    pr.append(jnp.exp2(sb[r] - m_new).astype(BF16))
                if prev is not None and r % (nrow // 2) == 0:
                    pv.append(value_half(prev[0], prev[2][r // (nrow // 2)], r // (nrow // 2)))
            if prev is not None:
                ph = prev[0]
                acc_ref[vs(ph), :] = prev[1] * acc_ref[vs(ph), :] + pv[0] + pv[1]
            if h < DSA_HEADS:
                prev = (h, alpha, [jnp.concatenate(pr[:nrow // 2], axis=0), jnp.concatenate(pr[nrow // 2:], axis=0)])

    @pl.when(ki < qi - 1)
    def _():
        mb = mask_ref[...].astype(F32)
        heads(lambda h: mb)

    @pl.when(ki == qi - 1)
    def _():
        mb = mask_ref[...].astype(F32)
        heads(lambda h: mb + nbp_ref[h].astype(F32))

    @pl.when(ki == qi)
    def _():
        mb = mask_ref[...].astype(F32)
        heads(lambda h: mb + nbd_ref[h].astype(F32))
        for h in range(DSA_HEADS):
            num = acc_ref[h * ATT_VROWS:h * ATT_VROWS + DSA_HD, :]
            den = acc_ref[h * ATT_VROWS + DSA_HD:h * ATT_VROWS + DSA_HD + 1, :]
            o_ref[hs(h), :] = (num / den).astype(o_ref.dtype)


def dsa_attention(zqk, vt, mask, nbd, nbp, b, s):
    t = min(ATT_T, s)
    ng = s // t
    pairs = [(qi, ki) for qi in range(ng) for ki in range(qi + 1)]
    qi_tab = jnp.asarray([p[0] for p in pairs], I32)
    ki_tab = jnp.asarray([p[1] for p in pairs], I32)
    grid_spec = pltpu.PrefetchScalarGridSpec(
        num_scalar_prefetch=2,
        grid=(b, len(pairs)),
        in_specs=[
            pl.BlockSpec((t, DSA_W), lambda bi, p, qt, kt: (bi * ng + qt[p], 0)),
            pl.BlockSpec((t, DSA_W), lambda bi, p, qt, kt: (bi * ng + kt[p], 1)),
            pl.BlockSpec((None, DSA_HEADS * ATT_VROWS, t), lambda bi, p, qt, kt: (bi, 0, kt[p])),
            pl.BlockSpec((None, t, t), lambda bi, p, qt, kt: (bi, kt[p], qt[p])),
            pl.BlockSpec((DSA_HEADS, t, t), lambda bi, p, qt, kt: (0, 0, 0)),
            pl.BlockSpec((DSA_HEADS, t, t), lambda bi, p, qt, kt: (0, 0, 0)),
        ],
        out_specs=pl.BlockSpec((None, DSA_W, t), lambda bi, p, qt, kt: (bi, 0, qt[p])),
        scratch_shapes=[
            pltpu.VMEM((DSA_HEADS, 1, t), F32),
            pltpu.VMEM((DSA_HEADS * ATT_VROWS, t), F32),
        ],
    )
    return pl.pallas_call(
        _attn_kernel,
        grid_spec=grid_spec,
        out_shape=jax.ShapeDtypeStruct((b, DSA_W, s), BF16),
        compiler_params=_cp(("parallel", "arbitrary")),
        name="dsa_attention",
    )(qi_tab, ki_tab, zqk, zqk, vt, mask, nbd, nbp)


def _t5_bucket(rel):
    half = REL_BUCKETS // 2
    exact = half // 2
    sign = jnp.where(rel > 0, half, 0)
    n = jnp.abs(rel)
    nf = jnp.maximum(n, 1).astype(F32)
    large = exact + (jnp.log(nf / exact) / math.log(REL_MAX_DIST / exact) * (half - exact)).astype(I32)
    large = jnp.minimum(large, half - 1)
    return sign + jnp.where(n < exact, n, large)


def _near_bias_tables(rel_bias, t):
    kpos = jnp.arange(t, dtype=I32)[:, None]
    qpos = jnp.arange(t, dtype=I32)[None, :]
    shifted = (rel_bias - rel_bias[REL_BUCKETS // 2 - 1]) * LOG2E

    def table(rel):
        onehot = (_t5_bucket(rel)[:, :, None] == jnp.arange(REL_BUCKETS, dtype=I32)).astype(F32)
        return jnp.einsum("kqb,bh->hkq", onehot, shifted, precision=lax.Precision.HIGHEST).astype(BF16)

    return table(kpos - qpos), table(kpos - t - qpos)


def _router_kernel(x_ref, g_ref, w_ref, b_ref, hn_ref, ri_ref, rw_ref):
    x = x_ref[...]
    ms = jnp.mean(x * x, axis=-1, keepdims=True)
    hn = x * lax.rsqrt(ms + EPS) * g_ref[...]
    hn_ref[...] = hn
    logits = _dot(hn.astype(BF16), w_ref[...]) + b_ref[...]
    lane = lax.broadcasted_iota(I32, logits.shape, 1)
    lanef = lane.astype(F32)
    big = 1e9
    is_grp = lane < N_GROUPS
    gl = jnp.where(is_grp, logits, -jnp.inf)
    gmax = jnp.max(gl, axis=-1, keepdims=True)
    grp = jnp.min(jnp.where(gl == gmax, lanef, big), axis=-1, keepdims=True)
    gsum = jnp.sum(jnp.where(is_grp, jnp.exp(logits - gmax), 0.0), axis=-1, keepdims=True)
    grp_w = 1.0 / gsum
    elo = N_GROUPS + grp * EXPERTS_PER_GROUP
    in_grp = jnp.logical_and(lanef >= elo, lanef < elo + EXPERTS_PER_GROUP)
    el = jnp.where(in_grp, logits, -jnp.inf)
    v1 = jnp.max(el, axis=-1, keepdims=True)
    i1 = jnp.min(jnp.where(el == v1, lanef, big), axis=-1, keepdims=True)
    el2 = jnp.where(lanef == i1, -jnp.inf, el)
    v2 = jnp.max(el2, axis=-1, keepdims=True)
    i2 = jnp.min(jnp.where(el2 == v2, lanef, big), axis=-1, keepdims=True)
    e2 = jnp.exp(v2 - v1)
    w1 = grp_w * (1.0 / (1.0 + e2))
    w2 = grp_w * (e2 / (1.0 + e2))
    ri_ref[...] = jnp.where(lane == 0, i1 - N_GROUPS, jnp.where(lane == 1, i2 - N_GROUPS, 0.0)).astype(I32)
    rw_ref[...] = jnp.where(lane == 0, w1, jnp.where(lane == 1, w2, 0.0))


def router(x1, g, w_r, b_r, tm=256):
    m, d = x1.shape
    tm = min(tm, m)
    return pl.pallas_call(
        _router_kernel,
        grid=(m // tm,),
        in_specs=[
            pl.BlockSpec((tm, d), lambda i: (i, 0)),
            pl.BlockSpec((1, d), lambda i: (0, 0)),
            pl.BlockSpec((d, LANES), lambda i: (0, 0)),
            pl.BlockSpec((1, LANES), lambda i: (0, 0)),
        ],
        out_specs=[
            pl.BlockSpec((tm, d), lambda i: (i, 0)),
            pl.BlockSpec((tm, LANES), lambda i: (i, 0)),
            pl.BlockSpec((tm, LANES), lambda i: (i, 0)),
        ],
        out_shape=[
            jax.ShapeDtypeStruct((m, d), F32),
            jax.ShapeDtypeStruct((m, LANES), I32),
            jax.ShapeDtypeStruct((m, LANES), F32),
        ],
        compiler_params=_cp(("parallel",)),
        name="router",
    )(x1, g.reshape(1, d).astype(F32), w_r, b_r)


def _rank_kernel(e_ref, rank_ref, cnt_ref, run_ref, *, rb):
    @pl.when(pl.program_id(0) == 0)
    def _():
        run_ref[...] = jnp.zeros_like(run_ref)

    e = e_ref[...]
    sub = lax.broadcasted_iota(I32, (N_EXPERTS, rb), 0)
    hit = sub == e
    oh = jnp.where(hit, 1.0, 0.0)
    a = lax.broadcasted_iota(I32, (rb, rb), 0)
    bcol = lax.broadcasted_iota(I32, (rb, rb), 1)
    upper = jnp.where(a < bcol, 1.0, 0.0).astype(BF16)
    prefix = _dot(oh.astype(BF16), upper)
    run = run_ref[...]
    rank = jnp.sum(jnp.where(hit, prefix + run[:, 0:1], 0.0), axis=0, keepdims=True)
    rank_ref[...] = rank.astype(I32)
    new = run + jnp.sum(oh, axis=1, keepdims=True)
    run_ref[...] = new
    cnt_ref[...] = new


def moe_rank(eid_flat):
    a = eid_flat.shape[0]
    rb = min(RANK_RB, a)
    nb = a // rb
    rank, cnt = pl.pallas_call(
        functools.partial(_rank_kernel, rb=rb),
        grid=(nb,),
        in_specs=[pl.BlockSpec((None, 1, rb), lambda i: (i, 0, 0))],
        out_specs=[
            pl.BlockSpec((None, 1, rb), lambda i: (i, 0, 0)),
            pl.BlockSpec((N_EXPERTS, LANES), lambda i: (0, 0)),
        ],
        out_shape=[
            jax.ShapeDtypeStruct((nb, 1, rb), I32),
            jax.ShapeDtypeStruct((N_EXPERTS, LANES), F32),
        ],
        scratch_shapes=[pltpu.VMEM((N_EXPERTS, LANES), F32)],
        compiler_params=_cp(("arbitrary",)),
        name="moe_rank",
    )(eid_flat.reshape(nb, 1, rb))
    return rank.reshape(a), cnt[:, 0].astype(I32)


def _row_copy(src, dst, sem):
    return pltpu.make_async_copy(src, dst, sem)


def _dispatch_kernel(dest_ref, hn_ref, xs_in_ref, xs_ref, sem, *, tm):
    del xs_in_ref
    base = pl.program_id(0) * (tm * TOP_K_INNER)

    def body(r, carry):
        for slot in range(TOP_K_INNER):
            d = dest_ref[base + r * TOP_K_INNER + slot]
            _row_copy(hn_ref.at[pl.ds(r, 1), :], xs_ref.at[pl.ds(d, 1), :], sem).start(priority=slot % 2)
        return carry

    lax.fori_loop(0, tm, body, 0, unroll=8)
    for _ in range(TOP_K_INNER):
        _row_copy(hn_ref, xs_ref.at[pl.ds(0, tm), :], sem).wait()


def moe_dispatch(dest, hn, xs0):
    m, d = hn.shape
    tm = min(ROW_T, m)
    grid_spec = pltpu.PrefetchScalarGridSpec(
        num_scalar_prefetch=1,
        grid=(m // tm,),
        in_specs=[
            pl.BlockSpec((tm, d), lambda i, dst: (i, 0)),
            pl.BlockSpec(memory_space=pl.ANY),
        ],
        out_specs=pl.BlockSpec(memory_space=pl.ANY),
        scratch_shapes=[pltpu.SemaphoreType.DMA(())],
    )
    return pl.pallas_call(
        functools.partial(_dispatch_kernel, tm=tm),
        grid_spec=grid_spec,
        out_shape=jax.ShapeDtypeStruct(xs0.shape, xs0.dtype),
        input_output_aliases={2: 0},
        compiler_params=_cp(("arbitrary",), has_side_effects=True),
        name="moe_dispatch",
    )(dest, hn, xs0)


def _expert_kernel(be_ref, nv_ref, run_ref, nxt_ref, xs_ref, wg_hbm, wu_hbm, wd_hbm, y_ref,
                   wg_f, wu_f, wd_f, wg_bf, wu_bf, wd_bf, sem):
    i = pl.program_id(0)
    valid = i < nv_ref[0]
    e = be_ref[i]
    fresh = jnp.logical_and(valid, jnp.logical_or(i == 0, e != be_ref[jnp.maximum(i - 1, 0)]))
    slot = run_ref[i] % 2

    def weight_copies(expert, s_):
        return [pltpu.make_async_copy(wg_hbm.at[expert], wg_f.at[s_], sem.at[s_, 0]),
                pltpu.make_async_copy(wu_hbm.at[expert], wu_f.at[s_], sem.at[s_, 1]),
                pltpu.make_async_copy(wd_hbm.at[expert], wd_f.at[s_], sem.at[s_, 2])]

    @pl.when(i == 0)
    def _():
        for c in weight_copies(e, 0):
            c.start()

    @pl.when(fresh)
    def _():
        nxt = nxt_ref[i]

        @pl.when(nxt >= 0)
        def _():
            for c in weight_copies(nxt, 1 - slot):
                c.start()

        for c in weight_copies(e, slot):
            c.wait()
        wg_bf[...] = wg_f[slot].astype(BF16)
        wu_bf[...] = wu_f[slot].astype(BF16)
        wd_bf[...] = wd_f[slot].astype(BF16)

    @pl.when(valid)
    def _():
        xb = xs_ref[...].astype(BF16)
        hg = _dot(xb, wg_bf[...])
        hu = _dot(xb, wu_bf[...])
        act = (hg * _sigmoid(hg) * hu).astype(BF16)
        y_ref[...] = _dot(act, wd_bf[...])

    @pl.when(jnp.logical_not(valid))
    def _():
        y_ref[...] = jnp.zeros_like(y_ref)


def moe_experts(blk_expert, nvalid, has_rows, xs, w_gate, w_up, w_down):
    p, d = xs.shape
    nblk = p // MOE_BLK
    ff = w_gate.shape[2]
    first = jnp.concatenate([jnp.ones((1,), bool), blk_expert[1:] != blk_expert[:-1]])
    run = (jnp.cumsum(first.astype(I32)) - 1).astype(I32)
    eids = jnp.arange(N_EXPERTS, dtype=I32)
    later = jnp.logical_and(eids[None, :] > eids[:, None], has_rows[None, :])
    nxt_tab = jnp.min(jnp.where(later, eids[None, :], N_EXPERTS), axis=1)
    nxt_tab = jnp.where(nxt_tab >= N_EXPERTS, -1, nxt_tab)
    nxt = jnp.sum(jnp.where(blk_expert[:, None] == eids[None, :], nxt_tab[None, :], 0), axis=1).astype(I32)

    def blk(i, nv):
        return jnp.minimum(i, nv[0] - 1)

    grid_spec = pltpu.PrefetchScalarGridSpec(
        num_scalar_prefetch=4,
        grid=(nblk,),
        in_specs=[
            pl.BlockSpec((MOE_BLK, d), lambda i, be, nv, rn, nx: (blk(i, nv), 0)),
            pl.BlockSpec(memory_space=pl.ANY),
            pl.BlockSpec(memory_space=pl.ANY),
            pl.BlockSpec(memory_space=pl.ANY),
        ],
        out_specs=pl.BlockSpec((MOE_BLK, d), lambda i, be, nv, rn, nx: (i, 0)),
        scratch_shapes=[
            pltpu.VMEM((2, d, ff), F32), pltpu.VMEM((2, d, ff), F32), pltpu.VMEM((2, ff, d), F32),
            pltpu.VMEM((d, ff), BF16), pltpu.VMEM((d, ff), BF16), pltpu.VMEM((ff, d), BF16),
            pltpu.SemaphoreType.DMA((2, 3)),
        ],
    )
    return pl.pallas_call(
        _expert_kernel,
        grid_spec=grid_spec,
        out_shape=jax.ShapeDtypeStruct((p, d), F32),
        compiler_params=_cp(("arbitrary",)),
        name="moe_experts",
    )(blk_expert, nvalid, run, nxt, xs, w_gate, w_up, w_down)


def _combine_kernel(dest_ref, x_ref, rw_ref, g_ref, y_ref, o_ref, hn_ref, ybuf, sem, *, tm):
    i = pl.program_id(0)
    nsteps = pl.num_programs(0)

    def gather(step, buf):
        base = step * (tm * TOP_K_INNER)

        def body(r, carry):
            for slot in range(TOP_K_INNER):
                d = dest_ref[base + r * TOP_K_INNER + slot]
                _row_copy(y_ref.at[pl.ds(d, 1), :], ybuf.at[buf, slot, pl.ds(r, 1), :],
                          sem.at[buf]).start(priority=slot % 2)
            return carry

        lax.fori_loop(0, tm, body, 0, unroll=8)

    @pl.when(i == 0)
    def _():
        gather(0, 0)

    @pl.when(i + 1 < nsteps)
    def _():
        gather(i + 1, (i + 1) % 2)

    cur = i % 2
    for slot in range(TOP_K_INNER):
        _row_copy(y_ref.at[pl.ds(0, tm), :], ybuf.at[cur, slot], sem.at[cur]).wait()
    rw = rw_ref[...]
    o = x_ref[...] + rw[:, 0:1] * ybuf[cur, 0] + rw[:, 1:2] * ybuf[cur, 1]
    o_ref[...] = o
    ms = jnp.mean(o * o, axis=-1, keepdims=True)
    hn_ref[...] = (o * lax.rsqrt(ms + EPS) * g_ref[...]).astype(hn_ref.dtype)


def moe_combine(dest, x1, rw, g, y):
    m, d = x1.shape
    tm = min(ROW_T, m)
    grid_spec = pltpu.PrefetchScalarGridSpec(
        num_scalar_prefetch=1,
        grid=(m // tm,),
        in_specs=[
            pl.BlockSpec((tm, d), lambda i, dst: (i, 0)),
            pl.BlockSpec((tm, LANES), lambda i, dst: (i, 0)),
            pl.BlockSpec((1, d), lambda i, dst: (0, 0)),
            pl.BlockSpec(memory_space=pl.ANY),
        ],
        out_specs=[pl.BlockSpec((tm, d), lambda i, dst: (i, 0)), pl.BlockSpec((tm, d), lambda i, dst: (i, 0))],
        scratch_shapes=[pltpu.VMEM((2, TOP_K_INNER, tm, d), F32), pltpu.SemaphoreType.DMA((2,))],
    )
    return pl.pallas_call(
        functools.partial(_combine_kernel, tm=tm),
        grid_spec=grid_spec,
        out_shape=[jax.ShapeDtypeStruct((m, d), F32), jax.ShapeDtypeStruct((m, d), BF16)],
        compiler_params=_cp(("arbitrary",)),
        name="moe_combine",
    )(dest, x1, rw, g.reshape(1, d).astype(F32), y)


def _layer(x, p, norm_mix_g, w_in, gla_w_alpha, gla_b_alpha, gla_norm_g, w_out_gla, q_norm_g, k_norm_g, rel_bias,
           w_out_dsa, w_branch_gate, b_branch_gate, w_out, norm_ffn_g, w_group_router, b_group_router,
           w_expert_router, b_expert_router, w_exp_gate, w_exp_up, w_exp_down, norm_ple_g, w_ple_gate,
           b_ple_gate, w_ple_proj):
    b, s, d = x.shape
    t = b * s
    x2d = x.reshape(t, d)

    o_gq, o_gk, o_gv, o_gr = 0, GLA_QK, 2 * GLA_QK, 2 * GLA_QK + GLA_V
    o_glr = o_gr + GLA_V
    o_dq = o_glr + GLA_GATE_RANK
    o_dk, o_dv = o_dq + DSA_W, o_dq + 2 * DSA_W
    o_iq = o_dv + DSA_W
    o_ik = o_iq + IDX_Q
    o_iw = o_ik + IDX_DIM
    w_g = w_in[:, o_gq:o_glr].astype(BF16)
    w_qk = w_in[:, o_dq:o_dv].astype(BF16)
    w_vi = w_in[:, o_dv:o_ik].astype(BF16)
    pad = jnp.zeros((d, LANES - GLA_GATE_RANK - IDX_HEADS - IDX_DIM), F32)
    w_s = jnp.concatenate([w_in[:, o_glr:o_dq], w_in[:, o_iw:o_iw + IDX_HEADS], pad,
                           w_in[:, o_ik:o_iw]], axis=1).astype(BF16)

    h = rmsnorm(x2d, norm_mix_g, BF16)
    zg = matmul(h, w_g, _epi_plain, [], BF16, "proj_gla")
    qk_gain = jnp.concatenate([jnp.tile(q_norm_g * (DSA_HD ** -0.5 * LOG2E), DSA_HEADS),
                               jnp.tile(k_norm_g, DSA_HEADS)]).reshape(1, 2 * DSA_W).astype(F32)
    zqk = matmul(h, w_qk, _epi_headnorm, [(qk_gain, "row")], BF16, "proj_dsa_qk")
    zvi = matmul(h, w_vi, _epi_plain, [], BF16, "proj_dsa_vi")
    zs = matmul(h, w_s, _epi_plain, [], F32, "proj_small")
    gates = matmul(h, w_branch_gate.astype(BF16), _epi_sigmoid_bias,
                   [(b_branch_gate.reshape(1, -1).astype(F32), "row")], BF16, "branch_gates")

    wa = gla_w_alpha.reshape(GLA_GATE_RANK, GLA_HEADS, GLA_DK).transpose(1, 0, 2).astype(BF16)
    ba = gla_b_alpha.reshape(GLA_HEADS, 1, GLA_DK).astype(F32)
    o_a = gla(zg, zs, wa, ba, gla_norm_g.reshape(1, GLA_DV).astype(F32), b, s)

    ik = zs[:, LANES - IDX_DIM:].astype(BF16).reshape(b, s, IDX_DIM)
    w_hm = zs[:, GLA_GATE_RANK:GLA_GATE_RANK + IDX_HEADS].reshape(b, s, IDX_HEADS).transpose(0, 2, 1)
    mask = dsa_index(zvi, w_hm, ik, b, s)
    nbd, nbp = _near_bias_tables(rel_bias, min(ATT_T, s))
    vt = zvi[:, :DSA_W].reshape(b, s, DSA_HEADS, DSA_HD).transpose(0, 2, 3, 1)
    vt = jnp.concatenate([vt, jnp.ones((b, DSA_HEADS, ATT_VROWS - DSA_HD, s), BF16)], axis=2)
    vt = vt.reshape(b, DSA_HEADS * ATT_VROWS, s)
    o_b = dsa_attention(zqk, vt, mask, nbd, nbp, b, s).transpose(0, 2, 1).reshape(t, DSA_W)

    m = mix(o_a, o_b, w_out_gla.astype(BF16), w_out_dsa.astype(BF16), gates)
    x1 = matmul(m, w_out.astype(BF16), _epi_residual, [(x2d, "tile")], F32, "out_proj")

    w_r = jnp.concatenate([w_group_router, w_expert_router,
                           jnp.zeros((d, LANES - N_GROUPS - N_EXPERTS), F32)], axis=1).astype(BF16)
    b_r = jnp.concatenate([b_group_router, b_expert_router,
                           jnp.zeros((LANES - N_GROUPS - N_EXPERTS,), F32)]).reshape(1, LANES)
    hn, ri, rw = router(x1, norm_ffn_g, w_r, b_r)
    a = t * TOP_K_INNER
    eid = ri[:, :TOP_K_INNER].reshape(a)
    rank, counts = moe_rank(eid)
    padded = ((counts + MOE_BLK - 1) // MOE_BLK) * MOE_BLK
    pend = jnp.cumsum(padded)
    pstart = pend - padded
    dest = (pstart[eid] + rank).astype(I32)
    prow = a + N_EXPERTS * MOE_BLK
    nblk = prow // MOE_BLK
    blk_row0 = jnp.arange(nblk, dtype=I32)[:, None] * MOE_BLK
    blk_expert = jnp.minimum(jnp.sum((pend[None, :] <= blk_row0).astype(I32), axis=1), N_EXPERTS - 1)
    nvalid = (pend[-1:] // MOE_BLK).astype(I32)
    xs = moe_dispatch(dest, hn, jnp.zeros((prow, d), F32))
    y = moe_experts(blk_expert, nvalid, padded > 0, xs, w_exp_gate, w_exp_up, w_exp_down)
    x2, hp = moe_combine(dest, x1, rw, norm_ple_g, y)

    out = matmul(hp, w_ple_gate.astype(BF16), _epi_ple,
                 [(b_ple_gate.reshape(1, -1).astype(F32), "row"), (x2, "tile"),
                  (p.reshape(t, -1).astype(BF16), "mk"), (w_ple_proj.astype(BF16), "kn")],
                 F32, "ple")
    return out.reshape(b, s, d)


def kernel(x, p, norm_mix_g, w_in, gla_w_alpha, gla_b_alpha, gla_norm_g, w_out_gla, q_norm_g, k_norm_g, rel_bias, w_out_dsa, w_branch_gate, b_branch_gate, w_out, norm_ffn_g, w_group_router, b_group_router, w_expert_router, b_expert_router, w_exp_gate, w_exp_up, w_exp_down, norm_ple_g, w_ple_gate, b_ple_gate, w_ple_proj):
    depth = w_in.shape[0]
    for i in range(depth):
        x = _layer(x, p[i], norm_mix_g[i], w_in[i], gla_w_alpha[i], gla_b_alpha[i], gla_norm_g[i], w_out_gla[i],
                   q_norm_g[i], k_norm_g[i], rel_bias, w_out_dsa[i], w_branch_gate[i], b_branch_gate[i], w_out[i],
                   norm_ffn_g[i], w_group_router[i], b_group_router[i], w_expert_router[i], b_expert_router[i],
                   w_exp_gate[i], w_exp_up[i], w_exp_down[i], norm_ple_g[i], w_ple_gate[i], b_ple_gate[i],
                   w_ple_proj[i])
    return x
```
